```python
import math
import jax
import jax.numpy as jnp
from jax import lax
import numpy as np

D_MODEL = 1024
BATCH = 4
SEQ = 8192
DEPTH = 1
DEC_BATCH = 128
DEC_SEQ = 4
PAST_LEN = 8192
PAGE_SIZE = 128

D_MIX = D_MODEL
D_A = D_MIX // 2
A_GROUPS = 8
A_GDIM = D_A // A_GROUPS
CHUNK = 128
D_B = D_MIX - D_A
NSA_HEADS = 8
HEAD_DIM = D_B // NSA_HEADS
NSA_KV_HEADS = 2
HPG = NSA_HEADS // NSA_KV_HEADS
KV_DIM = NSA_KV_HEADS * HEAD_DIM
CMP_LEN = 32
CMP_STRIDE = 16
CMP_HID = 2 * HEAD_DIM
SEL_BLOCK = 64
SEL_TOP = 16
WINDOW = 512
QBLOCK = 128
REL_BUCKETS = 32
REL_MAX_DIST = 128
MEM_LEN = 256
XA_HEADS = 4
XA_HEAD_DIM = D_MODEL // XA_HEADS
D_FF = 2816
CONV_W = 3
D_IN = 2 * D_A + D_B + 6 * KV_DIM + 3 * NSA_HEADS
NORM_EPS = 1e-6
NEG_INF = -1e30
FORCE_BONUS = 1e4

kernel_name = 'nsa_gmlp_hybrid_step'


def _rmsnorm(x, g):
    xf = x.astype(jnp.float32)
    y = xf * lax.rsqrt(jnp.mean(xf * xf, axis=-1, keepdims=True) + NORM_EPS)
    return y.astype(x.dtype) * g


def _layernorm(x, g, b):
    xf = x.astype(jnp.float32)
    mu = jnp.mean(xf, axis=-1, keepdims=True)
    var = jnp.mean(jnp.square(xf - mu), axis=-1, keepdims=True)
    return ((xf - mu) * lax.rsqrt(var + NORM_EPS)).astype(x.dtype) * g + b


def _masked_softmax(logits, mask):
    p = jax.nn.softmax(jnp.where(mask, logits, NEG_INF), axis=-1)
    return p * mask


def _t5_bucket(dist):
    n = jnp.maximum(dist, 0)
    max_exact = REL_BUCKETS // 2
    nf = jnp.maximum(n, 1).astype(jnp.float32)
    large = max_exact + (jnp.log(nf / max_exact) / math.log(REL_MAX_DIST / max_exact)
                         * (REL_BUCKETS - max_exact)).astype(jnp.int32)
    large = jnp.minimum(large, REL_BUCKETS - 1)
    return jnp.where(n < max_exact, n, large)


def _rel_bias(dist, table):
    return jnp.moveaxis(table[_t5_bucket(dist)], -1, -3)


def _split_in(z):
    n, t = z.shape[:2]
    o2 = 2 * D_A
    o3 = o2 + D_B
    o4 = o3 + 6 * KV_DIM
    u = z[..., :D_A]
    v = z[..., D_A:o2]
    q = z[..., o2:o3].reshape(n, t, NSA_HEADS, HEAD_DIM)
    kv = z[..., o3:o4].reshape(n, t, 6, NSA_KV_HEADS, HEAD_DIM)
    gates = jax.nn.sigmoid(z[..., o4:].reshape(n, t, NSA_HEADS, 3))
    return (u, v, q, kv[:, :, 0], kv[:, :, 1], kv[:, :, 2], kv[:, :, 3],
            kv[:, :, 4], kv[:, :, 5], gates)


def _sgu(u, v, w_s, b_s):
    n, t = u.shape[:2]
    tc = min(t, CHUNK)
    n_c = t // tc
    causal = jnp.tril(jnp.ones((tc, tc), dtype=w_s.dtype))
    w = w_s[:, :tc, :tc] * causal
    vb = v.reshape(n, n_c, tc, A_GROUPS, A_GDIM)
    mixed = jnp.einsum('gts,ncsgd->nctgd', w, vb) + b_s[:, :tc].T[None, None, :, :, None]
    return u * mixed.reshape(n, t, D_A)


def _gmlp_mixer(u, v, p):
    u = jax.nn.gelu(u)
    v = _layernorm(jax.nn.gelu(v), p['ln_v_g'], p['ln_v_b'])
    return _sgu(u, v, p['w_s'], p['b_s']), v


def _compress(rows, w1, b1, w2, b2, pe):
    n, L = rows.shape[:2]
    n_sub = L // CMP_STRIDE
    sub = rows[:, :n_sub * CMP_STRIDE].reshape(n, n_sub, CMP_STRIDE, NSA_KV_HEADS, HEAD_DIM)
    first = jnp.einsum('nspgd,pdh->nsgh', sub, w1[:CMP_STRIDE])
    second = jnp.einsum('nspgd,pdh->nsgh', sub, w1[CMP_STRIDE:])
    hid = first[:, :-1] + second[:, 1:] + (jnp.einsum('pd,pdh->h', pe, w1) + b1)
    return jnp.einsum('ncgh,hd->ncgd', jax.nn.gelu(hid), w2) + b2


def _compress_kv(k_rows, v_rows, p):
    kc = _compress(k_rows, p['w_cmp1_k'], p['b_cmp1_k'], p['w_cmp2_k'], p['b_cmp2_k'], p['pe_cmp_k'])
    vc = _compress(v_rows, p['w_cmp1_v'], p['b_cmp1_v'], p['w_cmp2_v'], p['b_cmp2_v'], p['pe_cmp_v'])
    return kc, vc


def _sel_blocks(rows):
    n, L = rows.shape[:2]
    n_sel = -(-L // SEL_BLOCK)
    rows = jnp.pad(rows, ((0, 0), (0, n_sel * SEL_BLOCK - L), (0, 0), (0, 0)))
    return rows.reshape(n, n_sel, SEL_BLOCK, NSA_KV_HEADS, HEAD_DIM).transpose(0, 3, 1, 2, 4)


def _cmp_sel_overlap(n_cmp, n_sel):
    c0 = jnp.arange(n_cmp, dtype=jnp.int32)[:, None] * CMP_STRIDE
    s0 = jnp.arange(n_sel, dtype=jnp.int32)[None, :] * SEL_BLOCK
    return ((c0 < s0 + SEL_BLOCK) & (c0 + CMP_LEN > s0)).astype(jnp.float32)


def _cmp_sel_attend(q, q_pos, kc, vc, ks_blk, vs_blk, rel_bias):
    n, tq = q.shape[:2]
    n_cmp = kc.shape[1]
    n_sel = ks_blk.shape[2]
    scale = HEAD_DIM ** -0.5
    qg = q.reshape(n, tq, NSA_KV_HEADS, HPG, HEAD_DIM)
    cmp_end = jnp.arange(n_cmp, dtype=jnp.int32) * CMP_STRIDE + (CMP_LEN - 1)
    d_cmp = q_pos[:, None] - cmp_end[None, :]
    b_cmp = _rel_bias(d_cmp, rel_bias).reshape(NSA_KV_HEADS, HPG, tq, n_cmp).astype(jnp.float32)
    l_cmp = jnp.einsum('nqghd,ncgd->nghqc', qg, kc).astype(jnp.float32) * scale + b_cmp
    p_cmp = _masked_softmax(l_cmp, d_cmp >= 0)
    o_cmp = jnp.einsum('nghqc,ncgd->nqghd', p_cmp.astype(vc.dtype), vc)
    imp = jnp.einsum('nghqc,cs->ngqs', p_cmp, _cmp_sel_overlap(n_cmp, n_sel))
    blk = jnp.arange(n_sel, dtype=jnp.int32)[None, :]
    cur = (q_pos // SEL_BLOCK)[:, None]
    valid = blk * SEL_BLOCK <= q_pos[:, None]
    forced = (blk == 0) | (blk == cur) | (blk == cur - 1)
    score = jnp.where(valid, imp + FORCE_BONUS * forced, NEG_INF)
    n_top = min(SEL_TOP, n_sel)
    top_val, top_idx = lax.top_k(score, n_top)
    picked = top_val > 0.5 * NEG_INF
    n_ix = jnp.arange(n)[:, None, None, None]
    g_ix = jnp.arange(NSA_KV_HEADS)[None, :, None, None]
    k_sel = ks_blk[n_ix, g_ix, top_idx]
    v_sel = vs_blk[n_ix, g_ix, top_idx]
    k_pos = top_idx[..., None] * SEL_BLOCK + jnp.arange(SEL_BLOCK, dtype=jnp.int32)
    d_sel = q_pos[:, None, None] - k_pos
    m_sel = (picked[..., None] & (d_sel >= 0)).reshape(n, NSA_KV_HEADS, 1, tq, n_top * SEL_BLOCK)
    table = rel_bias.reshape(REL_BUCKETS, NSA_KV_HEADS, HPG)
    b_sel = table[_t5_bucket(d_sel), g_ix[..., None]]
    b_sel = jnp.moveaxis(b_sel, -1, 2).reshape(n, NSA_KV_HEADS, HPG, tq, n_top * SEL_BLOCK)
    l_sel = jnp.einsum('nqghd,ngqkpd->nghqkp', qg, k_sel).astype(jnp.float32)
    l_sel = l_sel.reshape(n, NSA_KV_HEADS, HPG, tq, n_top * SEL_BLOCK) * scale + b_sel.astype(jnp.float32)
    p_sel = _masked_softmax(l_sel, m_sel).reshape(n, NSA_KV_HEADS, HPG, tq, n_top, SEL_BLOCK)
    o_sel = jnp.einsum('nghqkp,ngqkpd->nqghd', p_sel.astype(v_sel.dtype), v_sel)
    return (o_cmp.reshape(n, tq, NSA_HEADS, HEAD_DIM), o_sel.reshape(n, tq, NSA_HEADS, HEAD_DIM))


def _window_attend(q, k, v, q_pos, k_pos, rel_bias):
    n, tq = q.shape[:2]
    s = k.shape[1]
    qg = q.reshape(n, tq, NSA_KV_HEADS, HPG, HEAD_DIM)
    dist = q_pos[:, None] - k_pos[None, :]
    bias = _rel_bias(dist, rel_bias).reshape(NSA_KV_HEADS, HPG, tq, s).astype(jnp.float32)
    logits = jnp.einsum('nqghd,nsgd->nghqs', qg, k).astype(jnp.float32) * HEAD_DIM ** -0.5 + bias
    mask = (dist >= 0) & (dist < WINDOW) & (k_pos[None, :] >= 0)
    p = _masked_softmax(logits, mask)
    o = jnp.einsum('nghqs,nsgd->nqghd', p.astype(v.dtype), v)
    return o.reshape(n, tq, NSA_HEADS, HEAD_DIM)


def _gate_branches(g, o_cmp, o_sel, o_win):
    return g[..., 0:1] * o_cmp + g[..., 1:2] * o_sel + g[..., 2:3] * o_win


def _nsa_prompt(q, kc_rows, vc_rows, ks, vs, kw, vw, gates, p, rel_bias):
    b, t = q.shape[:2]
    kc, vc = _compress_kv(kc_rows, vc_rows, p)
    ks_blk = _sel_blocks(ks)
    vs_blk = _sel_blocks(vs)
    pad = ((0, 0), (WINDOW, 0), (0, 0), (0, 0))
    kw_pad = jnp.pad(kw, pad)
    vw_pad = jnp.pad(vw, pad)
    n_qb = t // QBLOCK
    q_blocks = q.reshape(b, n_qb, QBLOCK, NSA_HEADS, HEAD_DIM).transpose(1, 0, 2, 3, 4)
    g_blocks = gates.reshape(b, n_qb, QBLOCK, NSA_HEADS, 3).transpose(1, 0, 2, 3, 4)

    def block(args):
        qb, gb, c = args
        start = c * QBLOCK
        q_pos = start + jnp.arange(QBLOCK, dtype=jnp.int32)
        o_cmp, o_sel = _cmp_sel_attend(qb, q_pos, kc, vc, ks_blk, vs_blk, rel_bias)
        k_band = lax.dynamic_slice_in_dim(kw_pad, start, WINDOW + QBLOCK, axis=1)
        v_band = lax.dynamic_slice_in_dim(vw_pad, start, WINDOW + QBLOCK, axis=1)
        k_pos = start - WINDOW + jnp.arange(WINDOW + QBLOCK, dtype=jnp.int32)
        o_win = _window_attend(qb, k_band, v_band, q_pos, k_pos, rel_bias)
        return _gate_branches(gb, o_cmp, o_sel, o_win)

    out = lax.map(block, (q_blocks, g_blocks, jnp.arange(n_qb, dtype=jnp.int32)))
    return out.transpose(1, 0, 2, 3, 4).reshape(b, t, D_B)


def _gather_pages(pool, page_table):
    n, n_pages = page_table.shape
    g = pool[page_table]
    return g.reshape((n, n_pages * pool.shape[1]) + pool.shape[2:])


def _nsa_sample(q, kc_new, vc_new, ks_new, vs_new, kw_new, vw_new, gates, pool_cmp_k, pool_cmp_v,
                pool_sel_k, pool_sel_v, win_k, win_v, page_table, p, rel_bias):
    n, tn = q.shape[:2]
    past = page_table.shape[1] * pool_cmp_k.shape[1]

    def full(pool, new):
        return jnp.concatenate([_gather_pages(pool, page_table), new], axis=1)

    kc, vc = _compress_kv(full(pool_cmp_k, kc_new), full(pool_cmp_v, vc_new), p)
    ks_blk = _sel_blocks(full(pool_sel_k, ks_new))
    vs_blk = _sel_blocks(full(pool_sel_v, vs_new))
    q_pos = past + jnp.arange(tn, dtype=jnp.int32)
    o_cmp, o_sel = _cmp_sel_attend(q, q_pos, kc, vc, ks_blk, vs_blk, rel_bias)
    win_buf = win_k.shape[1]
    kw_ext = jnp.concatenate([win_k, kw_new], axis=1)
    vw_ext = jnp.concatenate([win_v, vw_new], axis=1)
    k_pos = past - win_buf + jnp.arange(win_buf + tn, dtype=jnp.int32)
    o_win = _window_attend(q, kw_ext, vw_ext, q_pos, k_pos, rel_bias)
    keep = min(WINDOW, past + tn)
    out = _gate_branches(gates, o_cmp, o_sel, o_win).reshape(n, tn, D_B)
    return out, kw_ext[:, -keep:], vw_ext[:, -keep:]


def _mix_out(a_out, b_out, p):
    merged = jnp.concatenate([_rmsnorm(a_out, p['g_a']), _rmsnorm(b_out, p['g_b'])], axis=-1)
    return merged @ p['w_o']


def _mem_kv(mem, p):
    n, m = mem.shape[:2]
    mn = _rmsnorm(mem, p['g_mem'])
    mk = (mn @ p['w_mk']).reshape(n, m, XA_HEADS, XA_HEAD_DIM)
    mv = (mn @ p['w_mv']).reshape(n, m, XA_HEADS, XA_HEAD_DIM)
    return mk, mv


def _cross_attend(h, mk, mv, p):
    n, t = h.shape[:2]
    q = (h @ p['w_xq']).reshape(n, t, XA_HEADS, XA_HEAD_DIM)
    logits = jnp.einsum('nthd,nmhd->nhtm', q, mk).astype(jnp.float32) * XA_HEAD_DIM ** -0.5
    pr = jax.nn.softmax(logits, axis=-1).astype(mv.dtype)
    o = jnp.einsum('nhtm,nmhd->nthd', pr, mv).reshape(n, t, D_MODEL)
    return o @ p['w_xo']


def _conv_ffn(h, prev, p):
    t = h.shape[1]
    a = h @ p['w_up']
    g = h @ p['w_gate']
    ext = jnp.concatenate([prev, a], axis=1)
    c = p['conv_b']
    for i in range(CONV_W):
        c = c + ext[:, i:i + t] * p['conv_w'][i]
    return (jax.nn.gelu(c) * g) @ p['w_down'], ext[:, t:]


def _layer_prompt(x, mem, p, rel_bias):
    b, t = x.shape[:2]
    h = _rmsnorm(x, p['g_mix'])
    u, v, q, kc, vc, ks, vs, kw, vw, gates = _split_in(h @ p['w_in'])
    a_out, _ = _gmlp_mixer(u, v, p)
    b_out = _nsa_prompt(q, kc, vc, ks, vs, kw, vw, gates, p, rel_bias)
    x = x + _mix_out(a_out, b_out, p)
    mk, mv = _mem_kv(mem, p)
    x = x + _cross_attend(_rmsnorm(x, p['g_xa']), mk, mv, p)
    f, conv_new = _conv_ffn(_rmsnorm(x, p['g_ffn']), jnp.zeros((b, CONV_W - 1, D_FF), x.dtype), p)
    x = x + f
    keep = min(WINDOW, t)
    return x, (kc, vc, ks, vs, kw[:, -keep:], vw[:, -keep:], mk, mv, conv_new)


def _layer_sample(x, pool_cmp_k, pool_cmp_v, pool_sel_k, pool_sel_v, win_k, win_v, mem_k, mem_v,
                  conv_prev, page_table, p, rel_bias):
    h = _rmsnorm(x, p['g_mix'])
    u, v, q, kc, vc, ks, vs, kw, vw, gates = _split_in(h @ p['w_in'])
    a_out, v_rows = _gmlp_mixer(u, v, p)
    b_out, win_k_new, win_v_new = _nsa_sample(q, kc, vc, ks, vs, kw, vw, gates, pool_cmp_k, pool_cmp_v,
                                              pool_sel_k, pool_sel_v, win_k, win_v, page_table, p, rel_bias)
    x = x + _mix_out(a_out, b_out, p)
    x = x + _cross_attend(_rmsnorm(x, p['g_xa']), mem_k, mem_v, p)
    f, conv_new = _conv_ffn(_rmsnorm(x, p['g_ffn']), conv_prev, p)
    x = x + f
    return x, (kc, vc, ks, vs, win_k_new, win_v_new, v_rows, conv_new)


def setup_inputs(seed: int = 0) -> dict:
    key = jax.random.key(seed)
    keys = jax.random.split(key, 80)
    counter = [0]

    def nxt():
        counter[0] += 1
        return keys[counter[0] - 1]

    def nrm(shape, scale=1.0):
        return jax.random.normal(nxt(), shape, jnp.float32) * scale

    def gain(shape):
        return 1.0 + nrm(shape, 0.02)

    n_pages = PAST_LEN // PAGE_SIZE
    n_used = DEC_BATCH * n_pages
    n_pool = n_used + n_used // 4
    win_buf = min(WINDOW, PAST_LEN)
    pool_shape = (DEPTH, n_pool, PAGE_SIZE, NSA_KV_HEADS, HEAD_DIM)
    page_table = jax.random.permutation(nxt(), n_pool)[:n_used].reshape(DEC_BATCH, n_pages).astype(jnp.int32)
    return {
        'x_prompt': nrm((BATCH, SEQ, D_MODEL)),
        'x_sample': nrm((DEC_BATCH, DEC_SEQ, D_MODEL)),
        'mem_prompt': nrm((BATCH, MEM_LEN, D_MODEL)),
        'cache_cmp_k': nrm(pool_shape),
        'cache_cmp_v': nrm(pool_shape),
        'cache_sel_k': nrm(pool_shape),
        'cache_sel_v': nrm(pool_shape),
        'cache_win_k': nrm((DEPTH, DEC_BATCH, win_buf, NSA_KV_HEADS, HEAD_DIM)),
        'cache_win_v': nrm((DEPTH, DEC_BATCH, win_buf, NSA_KV_HEADS, HEAD_DIM)),
        'cache_mem_k': nrm((DEPTH, DEC_BATCH, MEM_LEN, XA_HEADS, XA_HEAD_DIM)),
        'cache_mem_v': nrm((DEPTH, DEC_BATCH, MEM_LEN, XA_HEADS, XA_HEAD_DIM)),
        'state_conv': nrm((DEPTH, DEC_BATCH, CONV_W - 1, D_FF)),
        'page_table': page_table,
        'g_mix': gain((DEPTH, D_MODEL)),
        'w_in': nrm((DEPTH, D_MODEL, D_IN), D_MODEL ** -0.5),
        'w_s': nrm((DEPTH, A_GROUPS, CHUNK, CHUNK), CHUNK ** -0.5),
        'b_s': gain((DEPTH, A_GROUPS, CHUNK)),
        'ln_v_g': gain((DEPTH, D_A)),
        'ln_v_b': nrm((DEPTH, D_A), 0.02),
        'w_cmp1_k': nrm((DEPTH, CMP_LEN, HEAD_DIM, CMP_HID), (CMP_LEN * HEAD_DIM) ** -0.5),
        'b_cmp1_k': nrm((DEPTH, CMP_HID), 0.02),
        'w_cmp2_k': nrm((DEPTH, CMP_HID, HEAD_DIM), CMP_HID ** -0.5),
        'b_cmp2_k': nrm((DEPTH, HEAD_DIM), 0.02),
        'pe_cmp_k': nrm((DEPTH, CMP_LEN, HEAD_DIM), 0.1),
        'w_cmp1_v': nrm((DEPTH, CMP_LEN, HEAD_DIM, CMP_HID), (CMP_LEN * HEAD_DIM) ** -0.5),
        'b_cmp1_v': nrm((DEPTH, CMP_HID), 0.02),
        'w_cmp2_v': nrm((DEPTH, CMP_HID, HEAD_DIM), CMP_HID ** -0.5),
        'b_cmp2_v': nrm((DEPTH, HEAD_DIM), 0.02),
        'pe_cmp_v': nrm((DEPTH, CMP_LEN, HEAD_DIM), 0.1),
        'rel_bias': nrm((REL_BUCKETS, NSA_HEADS), 0.5),
        'g_a': gain((DEPTH, D_A)),
        'g_b': gain((DEPTH, D_B)),
        'w_o': nrm((DEPTH, D_MIX, D_MODEL), D_MIX ** -0.5),
        'g_xa': gain((DEPTH, D_MODEL)),
        'g_mem': gain((DEPTH, D_MODEL)),
        'w_xq': nrm((DEPTH, D_MODEL, D_MODEL), D_MODEL ** -0.5),
        'w_mk': nrm((DEPTH, D_MODEL, D_MODEL), D_MODEL ** -0.5),
        'w_mv': nrm((DEPTH, D_MODEL, D_MODEL), D_MODEL ** -0.5),
        'w_xo': nrm((DEPTH, D_MODEL, D_MODEL), D_MODEL ** -0.5),
        'g_ffn': gain((DEPTH, D_MODEL)),
        'w_up': nrm((DEPTH, D_MODEL, D_FF), D_MODEL ** -0.5),
        'w_gate': nrm((DEPTH, D_MODEL, D_FF), D_MODEL ** -0.5),
        'conv_w': nrm((DEPTH, CONV_W, D_FF), CONV_W ** -0.5),
        'conv_b': nrm((DEPTH, D_FF), 0.02),
        'w_down': nrm((DEPTH, D_FF, D_MODEL), D_FF ** -0.5),
        'g_final': gain((D_MODEL,)),
    }


def reference(x_prompt, x_sample, mem_prompt, cache_cmp_k, cache_cmp_v, cache_sel_k, cache_sel_v,
              cache_win_k, cache_win_v, cache_mem_k, cache_mem_v, state_conv, page_table,
              g_mix, w_in, w_s, b_s, ln_v_g, ln_v_b,
              w_cmp1_k, b_cmp1_k, w_cmp2_k, b_cmp2_k, pe_cmp_k,
              w_cmp1_v, b_cmp1_v, w_cmp2_v, b_cmp2_v, pe_cmp_v,
              rel_bias, g_a, g_b, w_o, g_xa, g_mem, w_xq, w_mk, w_mv, w_xo,
              g_ffn, w_up, w_gate, conv_w, conv_b, w_down, g_final):
    yp = x_prompt
    ys = x_sample
    st_p = []
    st_s = []
    for l in range(DEPTH):
        p = dict(g_mix=g_mix[l], w_in=w_in[l], w_s=w_s[l], b_s=b_s[l], ln_v_g=ln_v_g[l], ln_v_b=ln_v_b[l],
                 w_cmp1_k=w_cmp1_k[l], b_cmp1_k=b_cmp1_k[l], w_cmp2_k=w_cmp2_k[l], b_cmp2_k=b_cmp2_k[l],
                 pe_cmp_k=pe_cmp_k[l], w_cmp1_v=w_cmp1_v[l], b_cmp1_v=b_cmp1_v[l], w_cmp2_v=w_cmp2_v[l],
                 b_cmp2_v=b_cmp2_v[l], pe_cmp_v=pe_cmp_v[l], g_a=g_a[l], g_b=g_b[l], w_o=w_o[l],
                 g_xa=g_xa[l], g_mem=g_mem[l], w_xq=w_xq[l], w_mk=w_mk[l], w_mv=w_mv[l], w_xo=w_xo[l],
                 g_ffn=g_ffn[l], w_up=w_up[l], w_gate=w_gate[l], conv_w=conv_w[l], conv_b=conv_b[l],
                 w_down=w_down[l])
        yp, sp = _layer_prompt(yp, mem_prompt, p, rel_bias)
        ys, ss = _layer_sample(ys, cache_cmp_k[l], cache_cmp_v[l], cache_sel_k[l], cache_sel_v[l],
                               cache_win_k[l], cache_win_v[l], cache_mem_k[l], cache_mem_v[l],
                               state_conv[l], page_table, p, rel_bias)
        st_p.append(sp)
        st_s.append(ss)
    y_prompt = _rmsnorm(yp, g_final)
    y_sample = _rmsnorm(ys, g_final)

    def stk(states, i):
        return jnp.stack([s[i] for s in states])

    return (y_prompt, y_sample,
            stk(st_p, 0), stk(st_p, 1), stk(st_p, 2), stk(st_p, 3), stk(st_p, 4), stk(st_p, 5),
            stk(st_p, 6), stk(st_p, 7), stk(st_p, 8),
            stk(st_s, 0), stk(st_s, 1), stk(st_s, 2), stk(st_s, 3), stk(st_s, 4), stk(st_s, 5),
            stk(st_s, 6), stk(st_s, 7))
```

```python
import functools
import math

import numpy as np
import jax
import jax.numpy as jnp
from jax import lax
from jax.experimental import pallas as pl
from jax.experimental.pallas import tpu as pltpu

F32 = jnp.float32
BF16 = jnp.bfloat16

LANES = 128
SUBLANES = 8
VMEM_LIMIT_BYTES = 56 * 1024 * 1024

NORM_EPS = 1e-6
NEG_INF = -1e30
KNOCKED_OUT = -3e38
FORCE_BONUS = 1e4

A_GROUPS = 8
CHUNK = 128
NSA_HEADS = 8
HEAD_DIM = 64
NSA_KV_HEADS = 2
HPG = NSA_HEADS // NSA_KV_HEADS
CMP_LEN = 32
CMP_STRIDE = 16
SEL_BLOCK = 64
SEL_TOP = 16
WINDOW = 512
REL_BUCKETS = 32
REL_MAX_DIST = 128
XA_HEADS = 4
CONV_W = 3
ATT_SCALE = HEAD_DIM ** -0.5

_NT = (((1,), (1,)), ((), ()))


def _cparams(*sem):
    return pltpu.CompilerParams(dimension_semantics=sem, vmem_limit_bytes=VMEM_LIMIT_BYTES)


def _gelu(x):
    return 0.5 * x * (1.0 + jnp.tanh(0.7978845608028654 * (x + 0.044715 * (x * x * x))))


def _rms(x, g):
    return x * lax.rsqrt(jnp.mean(x * x, axis=-1, keepdims=True) + NORM_EPS) * g


def _dot(a, b):
    return jnp.dot(a, b, preferred_element_type=F32)


def _dot_nt(a, b):
    return lax.dot_general(a, b, _NT, preferred_element_type=F32)


def _inproj_body(x_ref, g_ref, w_ref, u_ref, v_ref, q_ref, kc_ref, vc_ref, ks_ref, vs_ref, kw_ref,
                 vw_ref, kvb_ref, gate_ref, *, d_a):
    h = _rms(x_ref[...], g_ref[...]).astype(BF16)

    def mm(lo, hi):
        return _dot(h, w_ref[:, lo:hi])

    u_ref[...] = mm(0, d_a)
    v_ref[...] = mm(d_a, 2 * d_a)
    o = 2 * d_a
    q_ref[...] = mm(o, o + NSA_HEADS * LANES).astype(BF16)
    o += NSA_HEADS * LANES
    for i, r in enumerate((kc_ref, vc_ref, ks_ref, vs_ref, kw_ref, vw_ref)):
        z = mm(o + LANES * i, o + LANES * (i + 1))
        r[...] = z
        kvb_ref[:, LANES * i:LANES * (i + 1)] = z.astype(BF16)
    o += 6 * LANES
    gate_ref[...] = 1.0 / (1.0 + jnp.exp(-mm(o, o + LANES)))


def _in_proj(x, g, w_pad, d_a, tm):
    m, d = x.shape
    npad = w_pad.shape[1]
    row = lambda n: pl.BlockSpec((tm, n), lambda i: (i, 0))
    full = lambda a: pl.BlockSpec(a.shape, lambda i: (0,) * a.ndim)
    kv = jax.ShapeDtypeStruct((m, LANES), F32)
    return pl.pallas_call(
        functools.partial(_inproj_body, d_a=d_a),
        grid=(m // tm,),
        in_specs=[row(d), full(g), full(w_pad)],
        out_specs=[row(d_a), row(d_a), row(NSA_HEADS * LANES)] + [row(LANES)] * 6 + [row(6 * LANES), row(LANES)],
        out_shape=[jax.ShapeDtypeStruct((m, d_a), F32), jax.ShapeDtypeStruct((m, d_a), F32),
                   jax.ShapeDtypeStruct((m, NSA_HEADS * LANES), BF16)] + [kv] * 6
                  + [jax.ShapeDtypeStruct((m, 6 * LANES), BF16), jax.ShapeDtypeStruct((m, LANES), F32)],
        compiler_params=_cparams("parallel"),
        name="in_proj",
    )(x, g, w_pad)


def _gmlp_body(u_ref, v_ref, lng_ref, lnb_ref, w_ref, bias_ref, ga_ref, a_ref, vr_ref, *, period_log2, rows):
    ri = lax.broadcasted_iota(jnp.int32, (CHUNK, CHUNK), 0)
    ci = lax.broadcasted_iota(jnp.int32, (CHUNK, CHUNK), 1)
    mask = (ci <= ri) & ((ri >> period_log2) == (ci >> period_log2))
    wm = [jnp.where(mask, w_ref[g], 0.0).astype(BF16) for g in range(A_GROUPS)]
    lane = lax.broadcasted_iota(jnp.int32, (CHUNK, LANES), 1)
    for c in range(rows // CHUNK):
        rs = slice(CHUNK * c, CHUNK * (c + 1))
        gv = _gelu(v_ref[rs, :])
        mu = jnp.mean(gv, axis=-1, keepdims=True)
        var = jnp.mean(jnp.square(gv - mu), axis=-1, keepdims=True)
        vn = (gv - mu) * lax.rsqrt(var + NORM_EPS) * lng_ref[...] + lnb_ref[...]
        vr_ref[rs, :] = vn
        vb = vn.astype(BF16)
        tiles = []
        for j in range(A_GROUPS // 2):
            vj = vb[:, LANES * j:LANES * (j + 1)]
            tiles.append(jnp.where(lane < LANES // 2, _dot(wm[2 * j], vj), _dot(wm[2 * j + 1], vj)))
        mixed = jnp.concatenate(tiles, axis=1) + bias_ref[...]
        a_ref[rs, :] = _rms(_gelu(u_ref[rs, :]) * mixed, ga_ref[...])


def _gmlp(u, v, ln_g, ln_b, w_mix, bias_full, g_a, period, rows):
    m, d_a = u.shape
    row = pl.BlockSpec((rows, d_a), lambda i: (i, 0))
    full = lambda a: pl.BlockSpec(a.shape, lambda i: (0,) * a.ndim)
    return pl.pallas_call(
        functools.partial(_gmlp_body, period_log2=int(math.log2(period)), rows=rows),
        grid=(m // rows,),
        in_specs=[row, row, full(ln_g), full(ln_b), full(w_mix), full(bias_full), full(g_a)],
        out_specs=[row, row],
        out_shape=[jax.ShapeDtypeStruct((m, d_a), F32)] * 2,
        compiler_params=_cparams("parallel"),
        name="gmlp",
    )(u, v, ln_g, ln_b, w_mix, bias_full, g_a)


def _compress_body(xk_ref, xv_ref, w1k_ref, w1v_ref, pek_ref, pev_ref, b1k_ref, b1v_ref, w2k_ref, w2vt_ref,
                   b2k_ref, b2vt_ref, kc_ref, vct_ref):
    n_sub = xk_ref.shape[1]
    hid_w = b1k_ref.shape[1]

    def hidden(x_ref, w1_ref, pe_ref, b1_ref):
        fs = _dot(x_ref[0].astype(BF16), w1_ref[...])
        pc = _dot(pe_ref[...], w1_ref[...])
        const = pc[0:1, :hid_w] + pc[1:2, hid_w:] + b1_ref[...]
        hid = fs[:, :hid_w] + pltpu.roll(fs[:, hid_w:], n_sub - 1, 0) + const
        return _gelu(hid).astype(BF16)

    kc_ref[0] = (_dot(hidden(xk_ref, w1k_ref, pek_ref, b1k_ref), w2k_ref[...]) + b2k_ref[...]).astype(BF16)
    vct_ref[0] = (_dot_nt(w2vt_ref[...], hidden(xv_ref, w1v_ref, pev_ref, b1v_ref)) + b2vt_ref[...]).astype(BF16)


def _compress(xk, xv, cw):
    n, n_sub, width = xk.shape
    blk = pl.BlockSpec((1, n_sub, width), lambda i: (i, 0, 0))
    full = lambda a: pl.BlockSpec(a.shape, lambda i: (0,) * a.ndim)
    ws = (cw["w1k"], cw["w1v"], cw["pek"], cw["pev"], cw["b1k"], cw["b1v"], cw["w2k"], cw["w2vt"], cw["b2k"],
          cw["b2vt"])
    return pl.pallas_call(
        _compress_body,
        grid=(n,),
        in_specs=[blk, blk] + [full(a) for a in ws],
        out_specs=[pl.BlockSpec((1, n_sub, LANES), lambda i: (i, 0, 0)),
                   pl.BlockSpec((1, LANES, n_sub), lambda i: (i, 0, 0))],
        out_shape=[jax.ShapeDtypeStruct((n, n_sub, LANES), BF16), jax.ShapeDtypeStruct((n, LANES, n_sub), BF16)],
        compiler_params=_cparams("parallel"),
        name="compress",
    )(xk, xv, *ws)


_PS_PAD = 16


def _importance(ps_scr, n_sel_pad):
    imp = ps_scr[pl.ds(_PS_PAD - 1, n_sel_pad, stride=4), :]
    for j in range(1, 5):
        imp = imp + ps_scr[pl.ds(_PS_PAD - 1 + j, n_sel_pad, stride=4), :]
    return imp


def _select_blocks(imp, t, n_top):
    s_idx = lax.broadcasted_iota(jnp.int32, imp.shape, 0)
    cur = t >> int(math.log2(SEL_BLOCK))
    valid = s_idx * SEL_BLOCK <= t
    forced = (s_idx == 0) | (s_idx == cur) | (s_idx == cur - 1)
    score = jnp.where(valid, imp + jnp.where(forced, FORCE_BONUS, 0.0), NEG_INF)
    s_f = s_idx.astype(F32)
    sel = jnp.zeros(imp.shape, F32)
    for _ in range(n_top):
        mx = jnp.max(score, axis=0, keepdims=True)
        first = jnp.min(jnp.where(score == mx, s_f, 1e9), axis=0, keepdims=True)
        hit = s_f == first
        sel = jnp.where(hit & (mx > 0.5 * NEG_INF), 1.0, sel)
        score = jnp.where(hit, KNOCKED_OUT, score)
    return sel


def _cmpsel_prompt_body(q_ref, kc_ref, vct_ref, band_ref, far_ref, oc_ref, sel_ref, s_scr, ps_scr, *, n_sub,
                        n_sel_pad, n_top):
    i = pl.program_id(1)
    t0 = i * LANES
    c_idx = lax.broadcasted_iota(jnp.int32, (n_sub, LANES), 0)
    q_idx = lax.broadcasted_iota(jnp.int32, (n_sub, LANES), 1)
    valid = (t0 + q_idx - CMP_STRIDE * c_idx - (CMP_LEN - 1)) >= 0
    kc = kc_ref[0]
    band_rows = band_ref.shape[1]
    base = pl.multiple_of(SUBLANES * i, SUBLANES)
    s_scr[0:_PS_PAD, :] = jnp.zeros((_PS_PAD, LANES), F32)
    ps_scr[...] = jnp.zeros(ps_scr.shape, F32)
    o_parts = []
    for g in range(NSA_KV_HEADS):
        psum = jnp.zeros((n_sub, LANES), F32)
        for hh in range(HPG):
            h = HPG * g + hh
            st = _dot_nt(kc, q_ref[:, LANES * h:LANES * (h + 1)])
            s_scr[_PS_PAD:_PS_PAD + n_sub, :] = st * ATT_SCALE + far_ref[h:h + 1, :]
            s_scr[pl.ds(base, band_rows), :] = s_scr[pl.ds(base, band_rows), :] + band_ref[h]
            s = jnp.where(valid, s_scr[_PS_PAD:_PS_PAD + n_sub, :], NEG_INF)
            m = jnp.max(s, axis=0, keepdims=True)
            e = jnp.where(valid, jnp.exp(s - m), 0.0)
            l = jnp.sum(e, axis=0, keepdims=True)
            p = e * (1.0 / jnp.where(l > 0.0, l, 1.0))
            psum = psum + p
            o_parts.append(_dot(vct_ref[0, HEAD_DIM * g:HEAD_DIM * (g + 1), :], p.astype(BF16)))
        ps_scr[_PS_PAD:_PS_PAD + n_sub, :] = psum
        t = t0 + lax.broadcasted_iota(jnp.int32, (n_sel_pad, LANES), 1)
        sel = _select_blocks(_importance(ps_scr, n_sel_pad), t, n_top)
        sel_ref[0, g] = sel.T.astype(BF16)
    oc_ref[...] = jnp.concatenate(o_parts, axis=0).T


def _cmpsel_prompt(qpad, kc, vct, band, far_rows, n_batch, seq, n_sel_pad, n_top):
    n_sub = kc.shape[1]
    nblk = seq // LANES
    full = lambda a: pl.BlockSpec(a.shape, lambda n, i: (0,) * a.ndim)
    return pl.pallas_call(
        functools.partial(_cmpsel_prompt_body, n_sub=n_sub, n_sel_pad=n_sel_pad, n_top=n_top),
        grid=(n_batch, nblk),
        in_specs=[pl.BlockSpec((LANES, NSA_HEADS * LANES), lambda n, i: (n * nblk + i, 0)),
                  pl.BlockSpec((1, n_sub, LANES), lambda n, i: (n, 0, 0)),
                  pl.BlockSpec((1, LANES, n_sub), lambda n, i: (n, 0, 0)),
                  full(band), full(far_rows)],
        out_specs=[pl.BlockSpec((LANES, NSA_HEADS * HEAD_DIM), lambda n, i: (n * nblk + i, 0)),
                   pl.BlockSpec((1, NSA_KV_HEADS, LANES, n_sel_pad), lambda n, i: (n, 0, i, 0))],
        out_shape=[jax.ShapeDtypeStruct((n_batch * seq, NSA_HEADS * HEAD_DIM), F32),
                   jax.ShapeDtypeStruct((n_batch, NSA_KV_HEADS, seq, n_sel_pad), BF16)],
        scratch_shapes=[pltpu.VMEM((_PS_PAD + n_sub, LANES), F32),
                        pltpu.VMEM((_PS_PAD + 4 * n_sel_pad, LANES), F32)],
        compiler_params=_cparams("parallel", "parallel"),
        name="cmpsel_prompt",
    )(qpad, kc, vct, band, far_rows)


def _cmpsel_sample_body(q_ref, kc_ref, vct_ref, bias_ref, rmat_ref, oct_ref, sel_ref, ps_scr, *, n_sub, n_sel_pad,
                        n_top, past, dec_seq):
    s = _dot_nt(kc_ref[0], q_ref[0]) * ATT_SCALE + bias_ref[...]
    m = jnp.max(s, axis=0, keepdims=True)
    e = jnp.exp(s - m)
    p = e * (1.0 / jnp.sum(e, axis=0, keepdims=True))
    oct_ref[0] = _dot(vct_ref[0], p.astype(BF16))
    hi = p.astype(BF16)
    lo = (p - hi.astype(F32)).astype(BF16)
    psum = _dot(hi, rmat_ref[...]) + _dot(lo, rmat_ref[...])
    ps_scr[...] = jnp.zeros(ps_scr.shape, F32)
    ps_scr[_PS_PAD:_PS_PAD + n_sub, :] = psum
    col = lax.broadcasted_iota(jnp.int32, (n_sel_pad, LANES), 1)
    t = past + (col & (dec_seq - 1))
    sel = _select_blocks(_importance(ps_scr, n_sel_pad), t, n_top)
    sel_ref[0] = sel.T.astype(BF16)


def _cmpsel_sample(qs, kc, vct, bias_t, rmat, n_sel_pad, n_top, past, dec_seq):
    n, n_sub, _ = kc.shape
    full = lambda a: pl.BlockSpec(a.shape, lambda i: (0,) * a.ndim)
    return pl.pallas_call(
        functools.partial(_cmpsel_sample_body, n_sub=n_sub, n_sel_pad=n_sel_pad, n_top=n_top, past=past,
                          dec_seq=dec_seq),
        grid=(n,),
        in_specs=[pl.BlockSpec((1, LANES, LANES), lambda i: (i, 0, 0)),
                  pl.BlockSpec((1, n_sub, LANES), lambda i: (i, 0, 0)),
                  pl.BlockSpec((1, LANES, n_sub), lambda i: (i, 0, 0)),
                  full(bias_t), full(rmat)],
        out_specs=[pl.BlockSpec((1, LANES, LANES), lambda i: (i, 0, 0)),
                   pl.BlockSpec((1, LANES, n_sel_pad), lambda i: (i, 0, 0))],
        out_shape=[jax.ShapeDtypeStruct((n, LANES, LANES), F32), jax.ShapeDtypeStruct((n, LANES, n_sel_pad), BF16)],
        scratch_shapes=[pltpu.VMEM((_PS_PAD + 4 * n_sel_pad, LANES), F32)],
        compiler_params=_cparams("parallel"),
        name="cmpsel_sample",
    )(qs, kc, vct, bias_t, rmat)


def _flash_update(q, k, v, bias, mask, m_ref, l_ref, acc_ref):
    s = _dot_nt(q, k) * ATT_SCALE + bias
    if mask is not None:
        s = jnp.where(mask, s, NEG_INF)
    m_prev = m_ref[...]
    m_new = jnp.maximum(m_prev, jnp.max(s, axis=1, keepdims=True))
    alpha = jnp.exp(m_prev - m_new)
    p = jnp.exp(s - m_new)
    if mask is not None:
        p = jnp.where(mask, p, 0.0)
    l_ref[...] = alpha * l_ref[...] + jnp.sum(p, axis=1, keepdims=True)
    acc_ref[...] = alpha * acc_ref[...] + _dot(p.astype(BF16), v)
    m_ref[...] = m_new


def _flash_init(m_ref, l_ref, acc_ref):
    m_ref[...] = jnp.full(m_ref.shape, NEG_INF, F32)
    l_ref[...] = jnp.zeros(l_ref.shape, F32)
    acc_ref[...] = jnp.zeros(acc_ref.shape, F32)


def _flash_prompt_body(far_ref, q_ref, ks_ref, vs_ref, kw_ref, vw_ref, sel_ref, e_ref, dprev_ref, ddiag_ref,
                       osel_ref, owin_ref, ms, ls, accs, mw, lw, accw, *, tq):
    qi = pl.program_id(1)
    for refs in ((ms, ls, accs), (mw, lw, accw)):
        _flash_init(*refs)
    ri = lax.broadcasted_iota(jnp.int32, (tq, tq), 0)
    ci = lax.broadcasted_iota(jnp.int32, (tq, tq), 1)
    causal = ci <= ri

    def heads_update(k, v, masks, delta_ref, m_r, l_r, acc_r):
        for h in range(NSA_HEADS):
            bias = far_ref[h] if delta_ref is None else far_ref[h] + delta_ref[h]
            _flash_update(q_ref[:, LANES * h:LANES * (h + 1)], k, v, bias, masks[h // HPG], m_r.at[h], l_r.at[h],
                          acc_r.at[h])

    def sel_tile(kt, delta_ref, extra_mask):
        k0 = pl.multiple_of(kt * tq, tq)
        masks = []
        for g in range(NSA_KV_HEADS):
            mk = _dot(sel_ref[0, g], e_ref[:, pl.ds(k0, tq)]) > 0.5
            masks.append(mk if extra_mask is None else mk & extra_mask)
        heads_update(ks_ref[pl.ds(k0, tq), :], vs_ref[pl.ds(k0, tq), :], masks, delta_ref, ms, ls, accs)

    def win_tile(kt, delta_ref, mask):
        k0 = pl.multiple_of(kt * tq, tq)
        heads_update(kw_ref[pl.ds(k0, tq), :], vw_ref[pl.ds(k0, tq), :], [mask] * NSA_KV_HEADS, delta_ref, mw, lw,
                     accw)

    def far_body(kt, carry):
        sel_tile(kt, None, None)
        return carry

    lax.fori_loop(0, qi - 1, far_body, 0)

    @pl.when(qi >= 1)
    def _():
        sel_tile(qi - 1, dprev_ref, None)
        win_tile(qi - 1, dprev_ref, None)

    sel_tile(qi, ddiag_ref, causal)

    @pl.when(qi >= WINDOW // tq)
    def _():
        win_tile(qi - WINDOW // tq, None, ci > ri)

    win_tile(qi, ddiag_ref, causal)

    lane = lax.broadcasted_iota(jnp.int32, (tq, LANES), 1)
    for out_ref, l_r, acc_r in ((osel_ref, ls, accs), (owin_ref, lw, accw)):
        for j in range(NSA_HEADS // 2):
            a0 = acc_r[2 * j] * (1.0 / l_r[2 * j])
            a1 = acc_r[2 * j + 1] * (1.0 / l_r[2 * j + 1])
            if (2 * j) // HPG == 0:
                a1 = pltpu.roll(a1, HEAD_DIM, 1)
            else:
                a0 = pltpu.roll(a0, HEAD_DIM, 1)
            out_ref[:, LANES * j:LANES * (j + 1)] = jnp.where(lane < HEAD_DIM, a0, a1)


def _flash_prompt(far, qpad, kvb, selmask, e_all, dprev, ddiag, n_batch, seq, tq):
    nq = seq // tq
    assert WINDOW % tq == 0 and WINDOW // tq == 2, "window tiles are hard-wired to two key tiles"
    n_sel_pad = selmask.shape[-1]
    kv_spec = lambda col: pl.BlockSpec((seq, LANES), lambda n, i, far_r: (n, col))
    full = lambda a: pl.BlockSpec(a.shape, lambda n, i, far_r: (0,) * a.ndim)
    out_spec = pl.BlockSpec((tq, NSA_HEADS * HEAD_DIM), lambda n, i, far_r: (n * nq + i, 0))
    state = [pltpu.VMEM((NSA_HEADS, tq, 1), F32), pltpu.VMEM((NSA_HEADS, tq, 1), F32),
             pltpu.VMEM((NSA_HEADS, tq, LANES), F32)]
    grid_spec = pltpu.PrefetchScalarGridSpec(
        num_scalar_prefetch=1,
        grid=(n_batch, nq),
        in_specs=[pl.BlockSpec((tq, NSA_HEADS * LANES), lambda n, i, far_r: (n * nq + i, 0)),
                  kv_spec(2), kv_spec(3), kv_spec(4), kv_spec(5),
                  pl.BlockSpec((1, NSA_KV_HEADS, tq, n_sel_pad), lambda n, i, far_r: (n, 0, i, 0)),
                  full(e_all), full(dprev), full(ddiag)],
        out_specs=[out_spec, out_spec],
        scratch_shapes=state + state,
    )
    out = jax.ShapeDtypeStruct((n_batch * seq, NSA_HEADS * HEAD_DIM), F32)
    return pl.pallas_call(
        functools.partial(_flash_prompt_body, tq=tq),
        grid_spec=grid_spec,
        out_shape=[out, out],
        compiler_params=_cparams("parallel", "parallel"),
        name="flash_prompt",
    )(far, qpad, kvb, kvb, kvb, kvb, selmask, e_all, dprev, ddiag)


def _flash_sample_body(q_ref, ks_ref, vs_ref, ksn_ref, vsn_ref, sel_ref, e_ref, far_ref, tailb_ref, newb_ref, kw_ref,
                       vw_ref, winb_ref, osel_ref, owin_ref, m_r, l_r, acc_r, *, tk):
    q = q_ref[0]
    past = ks_ref.shape[1]
    nt = past // tk

    def sel_tile(kt, tail):
        k0 = pl.multiple_of(kt * tk, tk)
        mask = _dot(sel_ref[0], e_ref[:, pl.ds(k0, tk)]) > 0.5
        bias = far_ref[...] + tailb_ref[...] if tail else far_ref[...]
        _flash_update(q, ks_ref[0, pl.ds(k0, tk), :].astype(BF16), vs_ref[0, pl.ds(k0, tk), :].astype(BF16), bias,
                      mask, m_r, l_r, acc_r)

    def far_body(kt, carry):
        sel_tile(kt, False)
        return carry

    _flash_init(m_r, l_r, acc_r)
    lax.fori_loop(0, nt - 1, far_body, 0)
    sel_tile(nt - 1, True)
    _flash_update(q, ksn_ref[0], vsn_ref[0], newb_ref[...], None, m_r, l_r, acc_r)
    osel_ref[0] = acc_r[...] * (1.0 / l_r[...])

    _flash_init(m_r, l_r, acc_r)
    _flash_update(q, kw_ref[0].astype(BF16), vw_ref[0].astype(BF16), winb_ref[...], None, m_r, l_r, acc_r)
    _flash_update(q, ksn_ref[1], vsn_ref[1], newb_ref[...], None, m_r, l_r, acc_r)
    owin_ref[0] = acc_r[...] * (1.0 / l_r[...])


def _flash_sample(qs, ks, vs, kvn, selrows, e_all, far_rows, tailb, newb, kw, vw, winb, tk):
    n, past, _ = ks.shape
    win_buf = kw.shape[1]
    n_sel_pad = selrows.shape[-1]
    per_n = lambda a: pl.BlockSpec((1,) + a.shape[1:], lambda i: (i,) + (0,) * (a.ndim - 1))
    full = lambda a: pl.BlockSpec(a.shape, lambda i: (0,) * a.ndim)
    out = jax.ShapeDtypeStruct((n, LANES, LANES), F32)
    return pl.pallas_call(
        functools.partial(_flash_sample_body, tk=tk),
        grid=(n,),
        in_specs=[per_n(qs), per_n(ks), per_n(vs),
                  pl.BlockSpec((2, LANES, LANES), lambda i: (i, 0, 0)),
                  pl.BlockSpec((2, LANES, LANES), lambda i: (i, 0, 0)),
                  per_n(selrows), full(e_all), full(far_rows), full(tailb), full(newb), per_n(kw), per_n(vw),
                  full(winb)],
        out_specs=[per_n(out), per_n(out)],
        out_shape=[out, out],
        scratch_shapes=[pltpu.VMEM((LANES, 1), F32), pltpu.VMEM((LANES, 1), F32), pltpu.VMEM((LANES, LANES), F32)],
        compiler_params=_cparams("parallel"),
        name="flash_sample",
    )(qs, ks, vs, kvn[0], kvn[1], selrows, e_all, far_rows, tailb, newb, kw, vw, winb)


def _gather_body(pt_ref, *refs, n_pools, n_pages, page):
    pools, outs, sem = refs[:n_pools], refs[n_pools:2 * n_pools], refs[2 * n_pools]
    n = pl.program_id(0)

    def copies():
        for a in range(n_pools):
            for p in range(n_pages):
                yield pltpu.make_async_copy(pools[a].at[pt_ref[n, p]], outs[a].at[n, pl.ds(p * page, page)],
                                            sem.at[a])

    for c in copies():
        c.start()
    for c in copies():
        c.wait()


def _gather_pages(page_table, pools):
    n, n_pages = page_table.shape
    page, width = pools[0].shape[1:]
    any_spec = pl.BlockSpec(memory_space=pl.ANY)
    grid_spec = pltpu.PrefetchScalarGridSpec(
        num_scalar_prefetch=1,
        grid=(n,),
        in_specs=[any_spec] * len(pools),
        out_specs=[any_spec] * len(pools),
        scratch_shapes=[pltpu.SemaphoreType.DMA((len(pools),))],
    )
    return pl.pallas_call(
        functools.partial(_gather_body, n_pools=len(pools), n_pages=n_pages, page=page),
        grid_spec=grid_spec,
        out_shape=[jax.ShapeDtypeStruct((n, n_pages * page, width), F32)] * len(pools),
        compiler_params=pltpu.CompilerParams(dimension_semantics=("arbitrary",), has_side_effects=True),
        name="gather_pages",
    )(page_table, *pools)


def _mixout_body(an_ref, oc_ref, os_ref, ow_ref, gate_ref, x_ref, gb_ref, woa_ref, wob_ref, y_ref):
    gates = gate_ref[...]
    tm = gates.shape[0]
    lane = lax.broadcasted_iota(jnp.int32, (tm, LANES), 1)
    tiles = []
    for j in range(NSA_HEADS // 2):
        cols = slice(LANES * j, LANES * (j + 1))
        acc = None
        for br, o_ref in enumerate((oc_ref, os_ref, ow_ref)):
            c0 = 3 * (2 * j) + br
            c1 = 3 * (2 * j + 1) + br
            gcol = jnp.where(lane < HEAD_DIM, gates[:, c0:c0 + 1], gates[:, c1:c1 + 1])
            term = gcol * o_ref[:, cols]
            acc = term if acc is None else acc + term
        tiles.append(acc)
    bn = _rms(jnp.concatenate(tiles, axis=1), gb_ref[...]).astype(BF16)
    y_ref[...] = x_ref[...] + _dot(an_ref[...].astype(BF16), woa_ref[...]) + _dot(bn, wob_ref[...])


def _mixout(an, oc, osel, owin, gates, x, g_b, wo_a, wo_b, tm):
    m, d = x.shape
    row = lambda a: pl.BlockSpec((tm, a.shape[1]), lambda i: (i, 0))
    full = lambda a: pl.BlockSpec(a.shape, lambda i: (0,) * a.ndim)
    return pl.pallas_call(
        _mixout_body,
        grid=(m // tm,),
        in_specs=[row(an), row(oc), row(osel), row(owin), row(gates), row(x), full(g_b), full(wo_a), full(wo_b)],
        out_specs=row(x),
        out_shape=jax.ShapeDtypeStruct((m, d), F32),
        compiler_params=_cparams("parallel"),
        name="mix_out",
    )(an, oc, osel, owin, gates, x, g_b, wo_a, wo_b)


def _normmm_body(x_ref, g_ref, w_ref, o_ref):
    o_ref[...] = _dot(_rms(x_ref[...], g_ref[...]).astype(BF16), w_ref[...]).astype(o_ref.dtype)


def _norm_matmul(x, g, w, out_dtype, tm):
    m, d = x.shape
    n = w.shape[1]
    return pl.pallas_call(
        _normmm_body,
        grid=(m // tm,),
        in_specs=[pl.BlockSpec((tm, d), lambda i: (i, 0)), pl.BlockSpec(g.shape, lambda i: (0, 0)),
                  pl.BlockSpec(w.shape, lambda i: (0, 0))],
        out_specs=pl.BlockSpec((tm, n), lambda i: (i, 0)),
        out_shape=jax.ShapeDtypeStruct((m, n), out_dtype),
        compiler_params=_cparams("parallel"),
        name="norm_matmul",
    )(x, g, w)


def _mmres_body(a_ref, w_ref, r_ref, o_ref):
    o_ref[...] = r_ref[...] + _dot(a_ref[...].astype(BF16), w_ref[...])


def _matmul_residual(a, w, res, tm):
    m, k = a.shape
    n = w.shape[1]
    return pl.pallas_call(
        _mmres_body,
        grid=(m // tm,),
        in_specs=[pl.BlockSpec((tm, k), lambda i: (i, 0)), pl.BlockSpec(w.shape, lambda i: (0, 0)),
                  pl.BlockSpec((tm, n), lambda i: (i, 0))],
        out_specs=pl.BlockSpec((tm, n), lambda i: (i, 0)),
        out_shape=jax.ShapeDtypeStruct((m, n), F32),
        compiler_params=_cparams("parallel"),
        name="matmul_residual",
    )(a, w, res)


def _xattn_body(q_ref, mk_ref, mv_ref, o_ref, *, n_heads):
    hd = q_ref.shape[2] // n_heads
    scale = hd ** -0.5
    for h in range(n_heads):
        cols = slice(hd * h, hd * (h + 1))
        s = _dot_nt(q_ref[0, :, cols], mk_ref[0, :, cols].astype(BF16)) * scale
        e = jnp.exp(s - jnp.max(s, axis=1, keepdims=True))
        p = e * (1.0 / jnp.sum(e, axis=1, keepdims=True))
        o_ref[0, :, cols] = _dot(p.astype(BF16), mv_ref[0, :, cols].astype(BF16))


def _xattn_core(q, mk, mv, tm):
    n, t, d = q.shape
    mem = mk.shape[1]
    return pl.pallas_call(
        functools.partial(_xattn_body, n_heads=XA_HEADS),
        grid=(n, t // tm),
        in_specs=[pl.BlockSpec((1, tm, d), lambda b, i: (b, i, 0)), pl.BlockSpec((1, mem, d), lambda b, i: (b, 0, 0)),
                  pl.BlockSpec((1, mem, d), lambda b, i: (b, 0, 0))],
        out_specs=pl.BlockSpec((1, tm, d), lambda b, i: (b, i, 0)),
        out_shape=jax.ShapeDtypeStruct((n, t, d), F32),
        compiler_params=_cparams("parallel", "parallel"),
        name="xattn_core",
    )(q, mk, mv)


def _ffn_body(x_ref, g_ref, wu_ref, wg_ref, cw_ref, cb_ref, wd_ref, gf_ref, s1_ref, s2_ref, y_ref, a_ref, h_scr,
              acc_scr, carry_scr, *, tf, dec_seq):
    i = pl.program_id(1)
    j = pl.program_id(2)
    tm = x_ref.shape[0]

    @pl.when(j == 0)
    def _():
        h_scr[...] = _rms(x_ref[...], g_ref[...]).astype(BF16)
        acc_scr[...] = jnp.zeros(acc_scr.shape, F32)

    a = _dot(h_scr[...], wu_ref[...])
    gt = _dot(h_scr[...], wg_ref[...])
    row = lax.broadcasted_iota(jnp.int32, (tm, tf), 0)
    r1 = pltpu.roll(a, 1, 0)
    r2 = pltpu.roll(a, 2, 0)
    if dec_seq is None:
        cols = pl.ds(pl.multiple_of(j * tf, LANES), tf)

        @pl.when(i == 0)
        def _():
            carry_scr[:, cols] = jnp.zeros((SUBLANES, tf), F32)

        p0 = carry_scr[SUBLANES - 2:SUBLANES - 1, cols]
        p1 = carry_scr[SUBLANES - 1:SUBLANES, cols]
        a1 = jnp.where(row == 0, p1, r1)
        a2 = jnp.where(row == 0, p0, jnp.where(row == 1, p1, r2))
        carry_scr[:, cols] = a[tm - SUBLANES:tm, :]
        a_ref[0] = a[tm - SUBLANES:tm, :]
    else:
        t = row & (dec_seq - 1)
        a1 = jnp.where(t == 0, s1_ref[...], r1)
        a2 = jnp.where(t < 2, s2_ref[...], r2)
        a_ref[...] = a
    c = cb_ref[...] + a2 * cw_ref[0:1, :] + a1 * cw_ref[1:2, :] + a * cw_ref[2:3, :]
    acc_scr[...] += _dot((_gelu(c) * gt).astype(BF16), wd_ref[...])

    @pl.when(j == pl.num_programs(2) - 1)
    def _():
        y_ref[...] = _rms(x_ref[...] + acc_scr[...], gf_ref[...])


def _ffn(x, g, wu, wg, cw, cb, wd, g_final, s1, s2, n_seq, tm, tf, dec_seq):
    m, d = x.shape
    f = wu.shape[1]
    nt = m // n_seq // tm
    nf = f // tf
    row = pl.BlockSpec((tm, d), lambda n, i, j: (n * nt + i, 0))
    vec = lambda a: pl.BlockSpec(a.shape, lambda n, i, j: (0,) * a.ndim)
    fcol = lambda a: pl.BlockSpec((a.shape[0], tf), lambda n, i, j: (0, j))
    if dec_seq is None:
        st_spec = pl.BlockSpec((SUBLANES, tf), lambda n, i, j: (0, j))
        a_spec = pl.BlockSpec((1, SUBLANES, tf), lambda n, i, j: (n * nt + i, 0, j))
        a_shape = jax.ShapeDtypeStruct((n_seq * nt, SUBLANES, f), F32)
    else:
        st_spec = pl.BlockSpec((tm, tf), lambda n, i, j: (n * nt + i, j))
        a_spec = st_spec
        a_shape = jax.ShapeDtypeStruct((m, f), F32)
    return pl.pallas_call(
        functools.partial(_ffn_body, tf=tf, dec_seq=dec_seq),
        grid=(n_seq, nt, nf),
        in_specs=[row, vec(g), fcol(wu), fcol(wg), fcol(cw), fcol(cb),
                  pl.BlockSpec((tf, d), lambda n, i, j: (j, 0)), vec(g_final), st_spec, st_spec],
        out_specs=[row, a_spec],
        out_shape=[jax.ShapeDtypeStruct((m, d), F32), a_shape],
        scratch_shapes=[pltpu.VMEM((tm, d), BF16), pltpu.VMEM((tm, d), F32), pltpu.VMEM((SUBLANES, f), F32)],
        compiler_params=_cparams("parallel", "arbitrary", "arbitrary"),
        name="conv_ffn",
    )(x, g, wu, wg, cw, cb, wd, g_final, s1, s2)


def _t5_bucket_np(dist):
    n = np.maximum(dist, 0)
    max_exact = REL_BUCKETS // 2
    nf = np.maximum(n, 1).astype(np.float64)
    large = max_exact + (np.log(nf / max_exact) / math.log(REL_MAX_DIST / max_exact)
                         * (REL_BUCKETS - max_exact)).astype(np.int32)
    large = np.minimum(large, REL_BUCKETS - 1)
    return np.where(n < max_exact, n, large).astype(np.int32)


def _bias_tables_prompt(rel_bias, tq):
    far = rel_bias[REL_BUCKETS - 1]
    dmax = 2 * tq
    dvec = rel_bias[_t5_bucket_np(np.arange(dmax))] - far[None, :]
    i = np.arange(tq)[:, None]
    j = np.arange(tq)[None, :]
    ddiag = jnp.where((i >= j)[..., None], dvec[np.clip(i - j, 0, dmax - 1)], 0.0)
    dprev = dvec[np.clip(tq + i - j, 0, dmax - 1)]
    r = np.arange(3 * SUBLANES)[:, None]
    q = np.arange(LANES)[None, :]
    d = q - CMP_STRIDE * (r - 2 * SUBLANES) - (CMP_LEN - 1)
    band = jnp.where(((d >= 0) & (d < dmax))[..., None], dvec[np.clip(d, 0, dmax - 1)], 0.0)
    to_h = lambda a: jnp.moveaxis(a, -1, 0)
    far_rows = jnp.broadcast_to(far[:, None], (NSA_HEADS, LANES))
    return far, far_rows, to_h(dprev), to_h(ddiag), to_h(band)


def _row_heads(dec_seq):
    rows = np.arange(LANES)
    live = rows < NSA_HEADS * dec_seq
    return np.where(live, rows // dec_seq, 0), rows % dec_seq, live


def _bias_tables_sample(rel_bias, past, dec_seq, n_cmp, n_sub, tk, win_buf):
    h_of, t_of, live = _row_heads(dec_seq)
    qpos = past + t_of

    def table(dist, ok):
        b = rel_bias[_t5_bucket_np(dist), h_of[:, None]]
        return jnp.where(live[:, None], jnp.where(ok, b, NEG_INF), 0.0).astype(F32)

    c = np.arange(n_sub)[None, :]
    d_cmp = qpos[:, None] - (c * CMP_STRIDE + CMP_LEN - 1)
    cmp_t = table(d_cmp, (d_cmp >= 0) & (c < n_cmp)).T
    far_rows = jnp.where(live, rel_bias[REL_BUCKETS - 1][h_of], 0.0)[:, None].astype(F32)
    kpos = past - tk + np.arange(tk)[None, :]
    d_tail = qpos[:, None] - kpos
    tail = jnp.where(live[:, None], rel_bias[_t5_bucket_np(d_tail), h_of[:, None]] - far_rows, 0.0).astype(F32)
    jn = np.arange(LANES)[None, :]
    d_new = t_of[:, None] - jn
    newb = table(d_new, (d_new >= 0) & (jn < dec_seq))
    jw = np.arange(win_buf)[None, :]
    d_win = qpos[:, None] - (past - win_buf + jw)
    winb = table(d_win, (d_win >= 0) & (d_win < WINDOW))
    return cmp_t, far_rows, tail, newb, winb


def _prep_in_proj(w_in, d_a):
    d_b = NSA_HEADS * HEAD_DIM
    o_q = 2 * d_a
    o_kv = o_q + d_b
    o_g = o_kv + 6 * NSA_KV_HEADS * HEAD_DIM
    wq = w_in[:, o_q:o_kv].reshape(-1, NSA_HEADS, HEAD_DIM)
    slot = jnp.zeros((w_in.shape[0], NSA_HEADS, NSA_KV_HEADS, HEAD_DIM), w_in.dtype)
    for h in range(NSA_HEADS):
        slot = slot.at[:, h, h // HPG].set(wq[:, h])
    wg = jnp.pad(w_in[:, o_g:], ((0, 0), (0, LANES - (w_in.shape[1] - o_g))))
    return jnp.concatenate([w_in[:, :o_q], slot.reshape(w_in.shape[0], -1), w_in[:, o_kv:o_g], wg],
                           axis=1).astype(BF16)


def _prep_compress(w1, b1, w2, b2, pe):
    eye = jnp.eye(NSA_KV_HEADS, dtype=w1.dtype)
    hid = w1.shape[-1]

    def big1(w):
        return jnp.einsum("pdh,gk->pgdkh", w, eye).reshape(CMP_STRIDE * NSA_KV_HEADS * HEAD_DIM, NSA_KV_HEADS * hid)

    w1big = jnp.concatenate([big1(w1[:CMP_STRIDE]), big1(w1[CMP_STRIDE:])], axis=1)
    pe_rows = jnp.broadcast_to(pe.reshape(2, CMP_STRIDE, 1, HEAD_DIM), (2, CMP_STRIDE, NSA_KV_HEADS, HEAD_DIM))
    pe_rows = jnp.pad(pe_rows.reshape(2, -1), ((0, 2 * SUBLANES - 2), (0, 0)))
    w2big = jnp.einsum("hd,gk->ghkd", w2, eye).reshape(NSA_KV_HEADS * hid, NSA_KV_HEADS * HEAD_DIM)
    return (w1big.astype(BF16), pe_rows.astype(BF16), jnp.tile(b1, NSA_KV_HEADS)[None, :], w2big.astype(BF16),
            jnp.tile(b2, NSA_KV_HEADS)[None, :])


def _prep_compress_kv(p):
    w1k, pek, b1k, w2k, b2k = _prep_compress(p["w_cmp1_k"], p["b_cmp1_k"], p["w_cmp2_k"], p["b_cmp2_k"], p["pe_cmp_k"])
    w1v, pev, b1v, w2v, b2v = _prep_compress(p["w_cmp1_v"], p["b_cmp1_v"], p["w_cmp2_v"], p["b_cmp2_v"], p["pe_cmp_v"])
    return dict(w1k=w1k, w1v=w1v, pek=pek, pev=pev, b1k=b1k, b1v=b1v, w2k=w2k, w2vt=w2v.T, b2k=b2k, b2vt=b2v.T)


def _block_expand(n_sel_pad, n_keys):
    s = np.arange(n_sel_pad)[:, None]
    k = np.arange(n_keys)[None, :]
    return jnp.asarray((k // SEL_BLOCK == s), dtype=BF16)


def _round_up(x, m):
    return -(-x // m) * m


def _shared_front(x2d, p, w_in_pad, period, mix_w, mix_b, tm):
    d_a = p["g_a"].shape[0]
    u, v, qpad, kc, vc, ks, vs, kw, vw, kvb, gates = _in_proj(x2d, p["g_mix"][None, :], w_in_pad, d_a, tm)
    an, v_rows = _gmlp(u, v, p["ln_v_g"][None, :], p["ln_v_b"][None, :], mix_w, mix_b, p["g_a"][None, :], period, tm)
    return an, v_rows, qpad, (kc, vc, ks, vs, kw, vw), kvb, gates


def _shared_back(x2d, an, oc, osel, owin, gates, p, wts, mk, mv, n_seq, s1, s2, g_final, dec_seq, tm):
    d = x2d.shape[1]
    x1 = _mixout(an, oc, osel, owin, gates, x2d, p["g_b"][None, :], wts["wo_a"], wts["wo_b"], tm)
    qx = _norm_matmul(x1, p["g_xa"][None, :], wts["w_xq"], BF16, tm)
    t = x1.shape[0] // n_seq
    if dec_seq is None:
        ox = _xattn_core(qx.reshape(n_seq, t, d), mk, mv, tm).reshape(-1, d)
    else:
        qx3 = jnp.pad(qx.reshape(n_seq, t, d), ((0, 0), (0, SUBLANES - t), (0, 0)))
        ox = _xattn_core(qx3, mk, mv, SUBLANES)[:, :t].reshape(-1, d)
    x2 = _matmul_residual(ox, wts["w_xo"], x1, tm)
    n_ffn_seq = n_seq if dec_seq is None else 1
    return _ffn(x2, p["g_ffn"][None, :], wts["w_up"], wts["w_gate"], wts["conv_w"], p["conv_b"][None, :],
                wts["w_down"], g_final[None, :], s1, s2, n_ffn_seq, tm, wts["tf"], dec_seq)


def _heads_to_rows(qpad, n, t):
    q = qpad.reshape(n, t, NSA_HEADS, LANES).transpose(0, 2, 1, 3).reshape(n, NSA_HEADS * t, LANES)
    return jnp.pad(q, ((0, 0), (0, LANES - NSA_HEADS * t), (0, 0)))


def _rows_to_tokens(o, n, t):
    o = o[:, :NSA_HEADS * t].reshape(n, NSA_KV_HEADS, HPG, t, NSA_KV_HEADS, HEAD_DIM)
    o = jnp.stack([o[:, g, :, :, g] for g in range(NSA_KV_HEADS)], axis=1)
    return o.transpose(0, 3, 1, 2, 4).reshape(n * t, NSA_HEADS * HEAD_DIM)


def kernel(x_prompt, x_sample, mem_prompt, cache_cmp_k, cache_cmp_v, cache_sel_k, cache_sel_v, cache_win_k,
           cache_win_v, cache_mem_k, cache_mem_v, state_conv, page_table, g_mix, w_in, w_s, b_s, ln_v_g, ln_v_b,
           w_cmp1_k, b_cmp1_k, w_cmp2_k, b_cmp2_k, pe_cmp_k, w_cmp1_v, b_cmp1_v, w_cmp2_v, b_cmp2_v, pe_cmp_v,
           rel_bias, g_a, g_b, w_o, g_xa, g_mem, w_xq, w_mk, w_mv, w_xo, g_ffn, w_up, w_gate, conv_w, conv_b,
           w_down, g_final):
    depth = w_in.shape[0]
    assert depth == 1, "the layer loop is written for a single layer"
    bsz, seq, d = x_prompt.shape
    nd, dec_seq, _ = x_sample.shape
    mem_len = mem_prompt.shape[1]
    d_a = g_a.shape[1]
    d_ff = w_up.shape[2]
    n_pages = page_table.shape[1]
    page = cache_cmp_k.shape[2]
    past = n_pages * page
    kvw = NSA_KV_HEADS * HEAD_DIM
    win_buf = cache_win_k.shape[2]
    assert kvw == LANES and seq % 256 == 0 and past % 512 == 0 and dec_seq == 4 and nd * dec_seq % LANES == 0
    tm = 512
    tq = 256
    tk_s = 512
    tf = d_ff // 2 if (d_ff // 2) % LANES == 0 else d_ff

    l = 0
    p = dict(g_mix=g_mix[l], ln_v_g=ln_v_g[l], ln_v_b=ln_v_b[l], w_cmp1_k=w_cmp1_k[l], b_cmp1_k=b_cmp1_k[l],
             w_cmp2_k=w_cmp2_k[l], b_cmp2_k=b_cmp2_k[l], pe_cmp_k=pe_cmp_k[l], w_cmp1_v=w_cmp1_v[l],
             b_cmp1_v=b_cmp1_v[l], w_cmp2_v=w_cmp2_v[l], b_cmp2_v=b_cmp2_v[l], pe_cmp_v=pe_cmp_v[l], g_a=g_a[l],
             g_b=g_b[l], g_xa=g_xa[l], g_ffn=g_ffn[l], conv_b=conv_b[l])
    w_in_pad = _prep_in_proj(w_in[l], d_a)
    cw = _prep_compress_kv(p)
    wts = dict(wo_a=w_o[l, :d_a].astype(BF16), wo_b=w_o[l, d_a:].astype(BF16), w_xq=w_xq[l].astype(BF16),
               w_xo=w_xo[l].astype(BF16), w_up=w_up[l].astype(BF16), w_gate=w_gate[l].astype(BF16),
               w_down=w_down[l].astype(BF16), conv_w=jnp.pad(conv_w[l], ((0, SUBLANES - CONV_W), (0, 0))), tf=tf)
    bias_lanes = jnp.repeat(b_s[l].T, d_a // A_GROUPS, axis=1)

    xp = x_prompt.reshape(bsz * seq, d)
    an, _, qpad, kv6, kvb, gates = _shared_front(xp, p, w_in_pad, CHUNK, w_s[l], bias_lanes, tm)
    n_sub = seq // CMP_STRIDE
    kc, vct = _compress(kv6[0].reshape(bsz, n_sub, CMP_STRIDE * kvw), kv6[1].reshape(bsz, n_sub, CMP_STRIDE * kvw),
                        cw)
    n_sel = seq // SEL_BLOCK
    n_sel_pad = _round_up(n_sel, LANES)
    far, far_rows, dprev, ddiag, band = _bias_tables_prompt(rel_bias, tq)
    oc, selmask = _cmpsel_prompt(qpad, kc, vct, band, far_rows, bsz, seq, n_sel_pad, min(SEL_TOP, n_sel))
    osel, owin = _flash_prompt(far, qpad, kvb, selmask, _block_expand(n_sel_pad, seq), dprev, ddiag, bsz, seq, tq)
    mem_kv = _norm_matmul(mem_prompt.reshape(bsz * mem_len, d), g_mem[l][None, :],
                          jnp.concatenate([w_mk[l], w_mv[l]], axis=1).astype(BF16), F32, min(tm, bsz * mem_len))
    mk_p = mem_kv[:, :d].reshape(bsz, mem_len, d)
    mv_p = mem_kv[:, d:].reshape(bsz, mem_len, d)
    zero_state = jnp.zeros((SUBLANES, d_ff), F32)
    yp, a_tail = _shared_back(xp, an, oc, osel, owin, gates, p, wts, mk_p, mv_p, bsz, zero_state, zero_state,
                              g_final, None, tm)
    keep = min(WINDOW, seq)
    shp = lambda a: a.reshape(1, bsz, seq, NSA_KV_HEADS, HEAD_DIM)
    prompt_state = (shp(kv6[0]), shp(kv6[1]), shp(kv6[2]), shp(kv6[3]), shp(kv6[4])[:, :, -keep:],
                    shp(kv6[5])[:, :, -keep:], mk_p.reshape(1, bsz, mem_len, XA_HEADS, d // XA_HEADS),
                    mv_p.reshape(1, bsz, mem_len, XA_HEADS, d // XA_HEADS), a_tail.reshape(bsz, -1, SUBLANES, d_ff)[None, :, -1, SUBLANES - (CONV_W - 1):])

    xs = x_sample.reshape(nd * dec_seq, d)
    reps = CHUNK // dec_seq
    mix_w_s = jnp.tile(w_s[l][:, :dec_seq, :dec_seq], (1, reps, reps))
    bias_s = jnp.repeat(jnp.tile(b_s[l][:, :dec_seq].T, (reps, 1)), d_a // A_GROUPS, axis=1)
    tms = min(tm, nd * dec_seq)
    an_s, v_rows, qpad_s, kv6s, kvb_s, gates_s = _shared_front(xs, p, w_in_pad, dec_seq, mix_w_s, bias_s, tms)
    pools = [c[l].reshape(c.shape[1], page, kvw) for c in (cache_cmp_k, cache_cmp_v, cache_sel_k, cache_sel_v)]
    gck, gcv, gsk, gsv = _gather_pages(page_table, pools)
    n_sub_s = past // CMP_STRIDE
    kc_s, vct_s = _compress(gck.reshape(nd, n_sub_s, CMP_STRIDE * kvw), gcv.reshape(nd, n_sub_s, CMP_STRIDE * kvw), cw)
    n_cmp_s = (past + dec_seq) // CMP_STRIDE - 1
    n_sel_s = -(-(past + dec_seq) // SEL_BLOCK)
    n_sel_pad_s = _round_up(n_sel_s, LANES)
    cmp_t, far_rows_s, tail, newb, winb = _bias_tables_sample(rel_bias, past, dec_seq, n_cmp_s, n_sub_s, tk_s, win_buf)
    h_of, t_of, live = _row_heads(dec_seq)
    same = (live[:, None] & live[None, :] & ((h_of[:, None] // HPG) == (h_of[None, :] // HPG))
            & (t_of[:, None] == t_of[None, :]))
    qs = _heads_to_rows(qpad_s, nd, dec_seq)
    oct_s, selrows = _cmpsel_sample(qs, kc_s, vct_s, cmp_t, jnp.asarray(same, dtype=BF16), n_sel_pad_s,
                                    min(SEL_TOP, n_sel_s), past, dec_seq)
    newrows = lambda a: jnp.pad(a.reshape(nd, dec_seq, LANES), ((0, 0), (0, LANES - dec_seq), (0, 0)))
    kvn = (jnp.stack([newrows(kvb_s[:, 2 * LANES:3 * LANES]), newrows(kvb_s[:, 4 * LANES:5 * LANES])], axis=1)
           .reshape(2 * nd, LANES, LANES),
           jnp.stack([newrows(kvb_s[:, 3 * LANES:4 * LANES]), newrows(kvb_s[:, 5 * LANES:6 * LANES])], axis=1)
           .reshape(2 * nd, LANES, LANES))
    wk = cache_win_k[l].reshape(nd, win_buf, kvw)
    wv = cache_win_v[l].reshape(nd, win_buf, kvw)
    osel_r, owin_r = _flash_sample(qs, gsk, gsv, kvn, selrows, _block_expand(n_sel_pad_s, past), far_rows_s, tail,
                                   newb, wk, wv, winb, tk_s)
    oc_s = _rows_to_tokens(oct_s.transpose(0, 2, 1), nd, dec_seq)
    osel_s = _rows_to_tokens(osel_r, nd, dec_seq)
    owin_s = _rows_to_tokens(owin_r, nd, dec_seq)
    st = state_conv[l]
    zrow = jnp.zeros_like(st[:, :1])
    s1 = jnp.concatenate([st[:, 1:2]] + [zrow] * (dec_seq - 1), axis=1).reshape(nd * dec_seq, d_ff)
    s2 = jnp.concatenate([st[:, 0:1], st[:, 1:2]] + [zrow] * (dec_seq - 2), axis=1).reshape(nd * dec_seq, d_ff)
    mk_s = cache_mem_k[l].reshape(nd, mem_len, d)
    mv_s = cache_mem_v[l].reshape(nd, mem_len, d)
    ys, a_full = _shared_back(xs, an_s, oc_s, osel_s, owin_s, gates_s, p, wts, mk_s, mv_s, nd, s1, s2, g_final,
                              dec_seq, tms)
    keep_s = min(WINDOW, past + dec_seq)
    shs = lambda a: a.reshape(1, nd, dec_seq, NSA_KV_HEADS, HEAD_DIM)
    win_new = lambda cache, new: jnp.concatenate([cache[l], shs(new)[0]], axis=1)[None, :, -keep_s:]
    sample_state = (shs(kv6s[0]), shs(kv6s[1]), shs(kv6s[2]), shs(kv6s[3]), win_new(cache_win_k, kv6s[4]),
                    win_new(cache_win_v, kv6s[5]), v_rows.reshape(1, nd, dec_seq, d_a),
                    a_full.reshape(1, nd, dec_seq, d_ff)[:, :, dec_seq - (CONV_W - 1):])

    return (yp.reshape(bsz, seq, d), ys.reshape(nd, dec_seq, d)) + prompt_state + sample_state
```

```python
import functools
import math

import numpy as np
import jax
import jax.numpy as jnp
from jax import lax
from jax.experimental import pallas as pl
from jax.experimental.pallas import tpu as pltpu

F32 = jnp.float32
BF16 = jnp.bfloat16

LANES = 128
SUBLANES = 8
VMEM_LIMIT_BYTES = 56 * 1024 * 1024

NORM_EPS = 1e-6
NEG_INF = -1e30
M_INIT = -1e29
KNOCKED_OUT = -3e38
FORCE_BONUS = 1e4

A_GROUPS = 8
CHUNK = 128
NSA_HEADS = 8
HEAD_DIM = 64
NSA_KV_HEADS = 2
HPG = NSA_HEADS // NSA_KV_HEADS
CMP_LEN = 32
CMP_STRIDE = 16
SEL_BLOCK = 64
SEL_TOP = 16
WINDOW = 512
REL_BUCKETS = 32
REL_MAX_DIST = 128
XA_HEADS = 4
CONV_W = 3
ATT_SCALE = HEAD_DIM ** -0.5

_NT = (((1,), (1,)), ((), ()))


def _cparams(*sem):
    return pltpu.CompilerParams(dimension_semantics=sem, vmem_limit_bytes=VMEM_LIMIT_BYTES)


def _gelu(x):
    return 0.5 * x * (1.0 + jnp.tanh(0.7978845608028654 * (x + 0.044715 * (x * x * x))))


def _rms(x, g):
    return x * lax.rsqrt(jnp.mean(x * x, axis=-1, keepdims=True) + NORM_EPS) * g


def _dot(a, b):
    return jnp.dot(a, b, preferred_element_type=F32)


def _dot_nt(a, b):
    return lax.dot_general(a, b, _NT, preferred_element_type=F32)


def _inproj_body(x_ref, g_ref, w_ref, wvt_ref, u_ref, v_ref, q_ref, kc_ref, vc_ref, ks_ref, vs_ref, kw_ref,
                 vw_ref, kvb_ref, vt_ref, gate_ref, *, d_a):
    h = _rms(x_ref[...], g_ref[...]).astype(BF16)

    def mm(lo, hi):
        return _dot(h, w_ref[:, lo:hi])

    u_ref[...] = mm(0, d_a)
    v_ref[...] = mm(d_a, 2 * d_a)
    o = 2 * d_a
    q_ref[...] = mm(o, o + NSA_HEADS * LANES).astype(BF16)
    o += NSA_HEADS * LANES
    for i, r in enumerate((kc_ref, vc_ref, ks_ref, vs_ref, kw_ref, vw_ref)):
        z = mm(o + LANES * i, o + LANES * (i + 1))
        r[...] = z
        kvb_ref[:, LANES * i:LANES * (i + 1)] = z.astype(BF16)
    o += 6 * LANES
    gate_ref[...] = 1.0 / (1.0 + jnp.exp(-mm(o, o + LANES)))
    vt_ref[...] = _dot_nt(wvt_ref[...], h).astype(BF16)


def _in_proj(x, g, w_pad, w_vt, d_a, tm):
    m, d = x.shape
    row = lambda n: pl.BlockSpec((tm, n), lambda i: (i, 0))
    full = lambda a: pl.BlockSpec(a.shape, lambda i: (0,) * a.ndim)
    kv = jax.ShapeDtypeStruct((m, LANES), F32)
    return pl.pallas_call(
        functools.partial(_inproj_body, d_a=d_a),
        grid=(m // tm,),
        in_specs=[row(d), full(g), full(w_pad), full(w_vt)],
        out_specs=[row(d_a), row(d_a), row(NSA_HEADS * LANES)] + [row(LANES)] * 6
                  + [row(6 * LANES), pl.BlockSpec((w_vt.shape[0], tm), lambda i: (0, i)), row(LANES)],
        out_shape=[jax.ShapeDtypeStruct((m, d_a), F32), jax.ShapeDtypeStruct((m, d_a), F32),
                   jax.ShapeDtypeStruct((m, NSA_HEADS * LANES), BF16)] + [kv] * 6
                  + [jax.ShapeDtypeStruct((m, 6 * LANES), BF16), jax.ShapeDtypeStruct((w_vt.shape[0], m), BF16),
                     jax.ShapeDtypeStruct((m, LANES), F32)],
        compiler_params=_cparams("parallel"),
        name="in_proj",
    )(x, g, w_pad, w_vt)


def _gmlp_body(u_ref, v_ref, lng_ref, lnb_ref, w_ref, bias_ref, ga_ref, a_ref, vr_ref, *, period_log2, rows):
    ri = lax.broadcasted_iota(jnp.int32, (CHUNK, CHUNK), 0)
    ci = lax.broadcasted_iota(jnp.int32, (CHUNK, CHUNK), 1)
    mask = (ci <= ri) & ((ri >> period_log2) == (ci >> period_log2))
    wm = [jnp.where(mask, w_ref[g], 0.0).astype(BF16) for g in range(A_GROUPS)]
    lane = lax.broadcasted_iota(jnp.int32, (CHUNK, LANES), 1)
    for c in range(rows // CHUNK):
        rs = slice(CHUNK * c, CHUNK * (c + 1))
        gv = _gelu(v_ref[rs, :])
        mu = jnp.mean(gv, axis=-1, keepdims=True)
        var = jnp.mean(jnp.square(gv - mu), axis=-1, keepdims=True)
        vn = (gv - mu) * lax.rsqrt(var + NORM_EPS) * lng_ref[...] + lnb_ref[...]
        vr_ref[rs, :] = vn
        vb = vn.astype(BF16)
        tiles = []
        for j in range(A_GROUPS // 2):
            vj = vb[:, LANES * j:LANES * (j + 1)]
            tiles.append(jnp.where(lane < LANES // 2, _dot(wm[2 * j], vj), _dot(wm[2 * j + 1], vj)))
        mixed = jnp.concatenate(tiles, axis=1) + bias_ref[...]
        a_ref[rs, :] = _rms(_gelu(u_ref[rs, :]) * mixed, ga_ref[...])


def _gmlp(u, v, ln_g, ln_b, w_mix, bias_full, g_a, period, rows):
    m, d_a = u.shape
    row = pl.BlockSpec((rows, d_a), lambda i: (i, 0))
    full = lambda a: pl.BlockSpec(a.shape, lambda i: (0,) * a.ndim)
    return pl.pallas_call(
        functools.partial(_gmlp_body, period_log2=int(math.log2(period)), rows=rows),
        grid=(m // rows,),
        in_specs=[row, row, full(ln_g), full(ln_b), full(w_mix), full(bias_full), full(g_a)],
        out_specs=[row, row],
        out_shape=[jax.ShapeDtypeStruct((m, d_a), F32)] * 2,
        compiler_params=_cparams("parallel"),
        name="gmlp",
    )(u, v, ln_g, ln_b, w_mix, bias_full, g_a)


_CW_KEYS = ("w1k", "w1v", "pek", "pev", "b1k", "b1v", "w2k", "w2vt", "b2k", "b2vt")


def _compress_rows(xk_ref, xv_ref, w, kc_ref, vct_ref):
    n_sub = kc_ref.shape[1]
    hid_w = w["b1k"].shape[1]

    def hidden(x_ref, w1_ref, pe_ref, b1_ref):
        fs = jnp.zeros((n_sub, 2 * hid_w), F32)
        pc = jnp.zeros((2 * SUBLANES, 2 * hid_w), F32)
        for p in range(CMP_STRIDE):
            xp = x_ref[pl.ds(p, n_sub, stride=CMP_STRIDE), :].astype(BF16)
            fs = fs + _dot(xp, w1_ref[p])
            pc = pc + _dot(pe_ref[p], w1_ref[p])
        const = pc[0:1, :hid_w] + pc[1:2, hid_w:] + b1_ref[...]
        hid = fs[:, :hid_w] + pltpu.roll(fs[:, hid_w:], n_sub - 1, 0) + const
        return _gelu(hid).astype(BF16)

    kc_ref[0] = (_dot(hidden(xk_ref, w["w1k"], w["pek"], w["b1k"]), w["w2k"][...]) + w["b2k"][...]).astype(BF16)
    vct_ref[0] = (_dot_nt(w["w2vt"][...], hidden(xv_ref, w["w1v"], w["pev"], w["b1v"]))
                  + w["b2vt"][...]).astype(BF16)


def _compress_prompt_body(xk_ref, xv_ref, *refs):
    w = dict(zip(_CW_KEYS, refs[:len(_CW_KEYS)]))
    _compress_rows(xk_ref, xv_ref, w, *refs[len(_CW_KEYS):])


def _compress_prompt(kc_rows, vc_rows, cw, n_batch, seq):
    n_sub = seq // CMP_STRIDE
    blk = pl.BlockSpec((seq, LANES), lambda i: (i, 0))
    full = lambda a: pl.BlockSpec(a.shape, lambda i: (0,) * a.ndim)
    ws = [cw[k] for k in _CW_KEYS]
    return pl.pallas_call(
        _compress_prompt_body,
        grid=(n_batch,),
        in_specs=[blk, blk] + [full(a) for a in ws],
        out_specs=[pl.BlockSpec((1, n_sub, LANES), lambda i: (i, 0, 0)),
                   pl.BlockSpec((1, LANES, n_sub), lambda i: (i, 0, 0))],
        out_shape=[jax.ShapeDtypeStruct((n_batch, n_sub, LANES), BF16),
                   jax.ShapeDtypeStruct((n_batch, LANES, n_sub), BF16)],
        compiler_params=_cparams("parallel"),
        name="compress_prompt",
    )(kc_rows, vc_rows, *ws)


def _page_copies(pt_ref, n, pools, bufs, sem, slot, n_pages, page):
    for a, (pool, buf) in enumerate(zip(pools, bufs)):
        for p in range(n_pages):
            yield pltpu.make_async_copy(pool.at[pt_ref[n, p]], buf.at[slot, pl.ds(p * page, page)], sem.at[a, slot])


def _fetch_pages(pt_ref, pools, bufs, sem, n_pages, page):
    n = pl.program_id(0)
    slot = n % 2
    copies = functools.partial(_page_copies, pt_ref, pools=pools, bufs=bufs, sem=sem, n_pages=n_pages, page=page)

    @pl.when(n == 0)
    def _():
        for c in copies(n=n, slot=slot):
            c.start()

    @pl.when(n + 1 < pl.num_programs(0))
    def _():
        for c in copies(n=n + 1, slot=1 - slot):
            c.start()

    for c in copies(n=n, slot=slot):
        c.wait()
    return slot


def _compress_sample_body(pt_ref, kpool_ref, vpool_ref, *refs, n_pages, page):
    nw = len(_CW_KEYS)
    w = dict(zip(_CW_KEYS, refs[:nw]))
    kc_ref, vct_ref, kbuf, vbuf, sem = refs[nw:]
    slot = _fetch_pages(pt_ref, (kpool_ref, vpool_ref), (kbuf, vbuf), sem, n_pages, page)
    _compress_rows(kbuf.at[slot], vbuf.at[slot], w, kc_ref, vct_ref)


def _compress_sample(page_table, kpool, vpool, cw):
    n, n_pages = page_table.shape
    page = kpool.shape[1]
    past = n_pages * page
    n_sub = past // CMP_STRIDE
    any_spec = pl.BlockSpec(memory_space=pl.ANY)
    full = lambda a: pl.BlockSpec(a.shape, lambda i, pt: (0,) * a.ndim)
    ws = [cw[k] for k in _CW_KEYS]
    grid_spec = pltpu.PrefetchScalarGridSpec(
        num_scalar_prefetch=1,
        grid=(n,),
        in_specs=[any_spec, any_spec] + [full(a) for a in ws],
        out_specs=[pl.BlockSpec((1, n_sub, LANES), lambda i, pt: (i, 0, 0)),
                   pl.BlockSpec((1, LANES, n_sub), lambda i, pt: (i, 0, 0))],
        scratch_shapes=[pltpu.VMEM((2, past, LANES), F32), pltpu.VMEM((2, past, LANES), F32),
                        pltpu.SemaphoreType.DMA((2, 2))],
    )
    return pl.pallas_call(
        functools.partial(_compress_sample_body, n_pages=n_pages, page=page),
        grid_spec=grid_spec,
        out_shape=[jax.ShapeDtypeStruct((n, n_sub, LANES), BF16), jax.ShapeDtypeStruct((n, LANES, n_sub), BF16)],
        compiler_params=_cparams("arbitrary"),
        name="compress_sample",
    )(page_table, kpool, vpool, *ws)


_PS_PAD = 16


def _importance(ps_scr, n_sel_pad):
    imp = ps_scr[pl.ds(_PS_PAD - 1, n_sel_pad, stride=4), :]
    for j in range(1, 5):
        imp = imp + ps_scr[pl.ds(_PS_PAD - 1 + j, n_sel_pad, stride=4), :]
    return imp


def _select_blocks(imp, t, n_top):
    s_idx = lax.broadcasted_iota(jnp.int32, imp.shape, 0)
    cur = t >> int(math.log2(SEL_BLOCK))
    valid = s_idx * SEL_BLOCK <= t
    forced = (s_idx == 0) | (s_idx == cur) | (s_idx == cur - 1)
    score = jnp.where(valid, imp + jnp.where(forced, FORCE_BONUS, 0.0), NEG_INF)
    s_f = s_idx.astype(F32)
    sel = jnp.zeros(imp.shape, F32)
    for _ in range(n_top):
        mx = jnp.max(score, axis=0, keepdims=True)
        first = jnp.min(jnp.where(score == mx, s_f, 1e9), axis=0, keepdims=True)
        hit = s_f == first
        sel = jnp.where(hit & (mx > 0.5 * NEG_INF), 1.0, sel)
        score = jnp.where(hit, KNOCKED_OUT, score)
    return sel


def _cmpsel_prompt_body(qt_ref, kc_ref, vct_ref, band_ref, oc_ref, sel_ref, s_scr, ps_scr, *, n_sub, n_sel_pad,
                        n_top):
    i = pl.program_id(1)
    t0 = i * LANES
    c_idx = lax.broadcasted_iota(jnp.int32, (n_sub, LANES), 0)
    q_idx = lax.broadcasted_iota(jnp.int32, (n_sub, LANES), 1)
    valid = (t0 + q_idx - CMP_STRIDE * c_idx - (CMP_LEN - 1)) >= 0
    kc = kc_ref[0]
    band_rows = band_ref.shape[1]
    base = pl.multiple_of(SUBLANES * i, SUBLANES)
    s_scr[:, 0:_PS_PAD, :] = jnp.zeros((NSA_HEADS, _PS_PAD, LANES), F32)
    ps_scr[...] = jnp.zeros(ps_scr.shape, F32)
    o_parts = []
    for g in range(NSA_KV_HEADS):
        psum = jnp.zeros((n_sub, LANES), F32)
        for hh in range(HPG):
            h = HPG * g + hh
            s_scr[h, _PS_PAD:_PS_PAD + n_sub, :] = _dot(kc, qt_ref[LANES * h:LANES * (h + 1), :])
            s_scr[h, pl.ds(base, band_rows), :] = s_scr[h, pl.ds(base, band_rows), :] + band_ref[h]
            s = jnp.where(valid, s_scr[h, _PS_PAD:_PS_PAD + n_sub, :], NEG_INF)
            m = jnp.max(s, axis=0, keepdims=True)
            e = jnp.where(valid, jnp.exp(s - m), 0.0)
            l = jnp.sum(e, axis=0, keepdims=True)
            p = e * (1.0 / jnp.where(l > 0.0, l, 1.0))
            psum = psum + p
            o_parts.append(_dot(vct_ref[0, HEAD_DIM * g:HEAD_DIM * (g + 1), :], p.astype(BF16)))
        ps_scr[g, _PS_PAD:_PS_PAD + n_sub, :] = psum
        t = t0 + lax.broadcasted_iota(jnp.int32, (n_sel_pad, LANES), 1)
        sel_ref[0, g] = _select_blocks(_importance(ps_scr.at[g], n_sel_pad), t, n_top)
    oc_ref[...] = jnp.concatenate(o_parts, axis=0).T


def _cmpsel_prompt(qvt, kc, vct, band, n_batch, seq, n_sel_pad, n_top):
    n_sub = kc.shape[1]
    nblk = seq // LANES
    full = lambda a: pl.BlockSpec(a.shape, lambda n, i: (0,) * a.ndim)
    return pl.pallas_call(
        functools.partial(_cmpsel_prompt_body, n_sub=n_sub, n_sel_pad=n_sel_pad, n_top=n_top),
        grid=(n_batch, nblk),
        in_specs=[pl.BlockSpec((NSA_HEADS * LANES, LANES), lambda n, i: (0, n * nblk + i)),
                  pl.BlockSpec((1, n_sub, LANES), lambda n, i: (n, 0, 0)),
                  pl.BlockSpec((1, LANES, n_sub), lambda n, i: (n, 0, 0)),
                  full(band)],
        out_specs=[pl.BlockSpec((LANES, NSA_HEADS * HEAD_DIM), lambda n, i: (n * nblk + i, 0)),
                   pl.BlockSpec((1, NSA_KV_HEADS, n_sel_pad, LANES), lambda n, i: (n, 0, 0, i))],
        out_shape=[jax.ShapeDtypeStruct((n_batch * seq, NSA_HEADS * HEAD_DIM), F32),
                   jax.ShapeDtypeStruct((n_batch, NSA_KV_HEADS, n_sel_pad, seq), F32)],
        scratch_shapes=[pltpu.VMEM((NSA_HEADS, _PS_PAD + n_sub, LANES), F32),
                        pltpu.VMEM((NSA_KV_HEADS, _PS_PAD + 4 * n_sel_pad, LANES), F32)],
        compiler_params=_cparams("parallel", "parallel"),
        name="cmpsel_prompt",
    )(qvt, kc, vct, band)


def _cmpsel_sample_body(q_ref, kc_ref, vct_ref, bias_ref, rmat_ref, oct_ref, sel_ref, ps_scr, *, n_sub, n_sel_pad,
                        n_top, past, dec_seq):
    s = _dot_nt(kc_ref[0], q_ref[0]) + bias_ref[...]
    m = jnp.max(s, axis=0, keepdims=True)
    e = jnp.exp(s - m)
    p = e * (1.0 / jnp.sum(e, axis=0, keepdims=True))
    oct_ref[0] = _dot(vct_ref[0], p.astype(BF16))
    hi = p.astype(BF16)
    lo = (p - hi.astype(F32)).astype(BF16)
    psum = _dot(hi, rmat_ref[...]) + _dot(lo, rmat_ref[...])
    ps_scr[...] = jnp.zeros(ps_scr.shape, F32)
    ps_scr[_PS_PAD:_PS_PAD + n_sub, :] = psum
    col = lax.broadcasted_iota(jnp.int32, (n_sel_pad, LANES), 1)
    t = past + (col & (dec_seq - 1))
    sel = _select_blocks(_importance(ps_scr, n_sel_pad), t, n_top)
    sel_ref[0] = sel.T.astype(BF16)


def _cmpsel_sample(qs, kc, vct, bias_t, rmat, n_sel_pad, n_top, past, dec_seq):
    n, n_sub, _ = kc.shape
    full = lambda a: pl.BlockSpec(a.shape, lambda i: (0,) * a.ndim)
    return pl.pallas_call(
        functools.partial(_cmpsel_sample_body, n_sub=n_sub, n_sel_pad=n_sel_pad, n_top=n_top, past=past,
                          dec_seq=dec_seq),
        grid=(n,),
        in_specs=[pl.BlockSpec((1, LANES, LANES), lambda i: (i, 0, 0)),
                  pl.BlockSpec((1, n_sub, LANES), lambda i: (i, 0, 0)),
                  pl.BlockSpec((1, LANES, n_sub), lambda i: (i, 0, 0)),
                  full(bias_t), full(rmat)],
        out_specs=[pl.BlockSpec((1, LANES, LANES), lambda i: (i, 0, 0)),
                   pl.BlockSpec((1, LANES, n_sel_pad), lambda i: (i, 0, 0))],
        out_shape=[jax.ShapeDtypeStruct((n, LANES, LANES), F32), jax.ShapeDtypeStruct((n, LANES, n_sel_pad), BF16)],
        scratch_shapes=[pltpu.VMEM((_PS_PAD + 4 * n_sel_pad, LANES), F32)],
        compiler_params=_cparams("parallel"),
        name="cmpsel_sample",
    )(qs, kc, vct, bias_t, rmat)


def _flash_t_update(h, k, vt, qt, bias_ref, mask_rows, m_ref, l_ref, acc_ref):
    s = _dot(k, qt)
    m = m_ref[h]
    l = l_ref[h]
    acc = acc_ref[h]
    for c in range(k.shape[0] // SEL_BLOCK):
        rows = slice(SEL_BLOCK * c, SEL_BLOCK * (c + 1))
        sc = s[rows]
        if bias_ref is not None:
            sc = sc + bias_ref[rows, :]
        if mask_rows is not None:
            sc = jnp.where(mask_rows[c:c + 1, :] > 0.5, sc, NEG_INF)
        m_new = jnp.maximum(m, jnp.max(sc, axis=0, keepdims=True))
        alpha = jnp.exp(m - m_new)
        p = jnp.exp(sc - m_new)
        l = alpha * l + jnp.sum(p, axis=0, keepdims=True)
        acc = alpha * acc + _dot(vt[:, rows], p.astype(BF16))
        m = m_new
    m_ref[h] = m
    l_ref[h] = l
    acc_ref[h] = acc


def _flash_prompt_body(qt_ref, ks_ref, vst_ref, kw_ref, vwt_ref, sel_ref, near_ref, winfar_ref, osel_ref, owin_ref,
                       ms, ls, accs, mw, lw, accw, *, tq):
    qi = pl.program_id(1)
    for m_r, l_r, acc_r in ((ms, ls, accs), (mw, lw, accw)):
        m_r[...] = jnp.full(m_r.shape, M_INIT, F32)
        l_r[...] = jnp.zeros(l_r.shape, F32)
        acc_r[...] = jnp.zeros(acc_r.shape, F32)
    blocks = tq // SEL_BLOCK

    def sel_tile(kt, near):
        k0 = pl.multiple_of(kt * tq, tq)
        k = ks_ref[pl.ds(k0, tq), :]
        r0 = pl.multiple_of((kt // 2) * SUBLANES, SUBLANES)
        odd = (kt & 1) == 1
        for g in range(NSA_KV_HEADS):
            r8 = sel_ref[0, g, pl.ds(r0, SUBLANES), :]
            r4 = jnp.where(odd, r8[blocks:2 * blocks], r8[0:blocks])
            vt = vst_ref[HEAD_DIM * g:HEAD_DIM * (g + 1), pl.ds(k0, tq)]
            for hh in range(HPG):
                h = HPG * g + hh
                bias = None if near is None else near_ref.at[near, h]
                _flash_t_update(h, k, vt, qt_ref[LANES * h:LANES * (h + 1), :], bias, r4, ms, ls, accs)

    def win_tile(kt, near):
        k0 = pl.multiple_of(kt * tq, tq)
        k = kw_ref[pl.ds(k0, tq), :]
        for g in range(NSA_KV_HEADS):
            vt = vwt_ref[HEAD_DIM * g:HEAD_DIM * (g + 1), pl.ds(k0, tq)]
            for hh in range(HPG):
                h = HPG * g + hh
                bias = winfar_ref if near is None else near_ref.at[near, h]
                _flash_t_update(h, k, vt, qt_ref[LANES * h:LANES * (h + 1), :], bias, None, mw, lw, accw)

    def far_body(kt, carry):
        sel_tile(kt, None)
        return carry

    lax.fori_loop(0, qi - 1, far_body, 0)

    @pl.when(qi >= 1)
    def _():
        sel_tile(qi - 1, 0)
        win_tile(qi - 1, 0)

    sel_tile(qi, 1)

    @pl.when(qi >= 2)
    def _():
        win_tile(qi - 2, None)

    win_tile(qi, 1)

    for out_ref, l_r, acc_r in ((osel_ref, ls, accs), (owin_ref, lw, accw)):
        o_t = jnp.concatenate([acc_r[h] * (1.0 / l_r[h]) for h in range(NSA_HEADS)], axis=0)
        out_ref[...] = o_t.T


def _flash_prompt(kvb, qvt, sel_t, near, winfar, n_batch, seq, tq):
    nq = seq // tq
    assert WINDOW == 2 * tq and tq == 4 * SEL_BLOCK, "window = two key tiles; a key tile = 4 selection blocks"
    n_sel_pad = sel_t.shape[2]
    k_spec = lambda col: pl.BlockSpec((seq, LANES), lambda n, i: (n, col))
    vt_spec = lambda row: pl.BlockSpec((LANES, seq), lambda n, i: (row, n))
    full = lambda a: pl.BlockSpec(a.shape, lambda n, i: (0,) * a.ndim)
    out_spec = pl.BlockSpec((tq, NSA_HEADS * HEAD_DIM), lambda n, i: (n * nq + i, 0))
    state = [pltpu.VMEM((NSA_HEADS, 1, tq), F32), pltpu.VMEM((NSA_HEADS, 1, tq), F32),
             pltpu.VMEM((NSA_HEADS, HEAD_DIM, tq), F32)]
    out = jax.ShapeDtypeStruct((n_batch * seq, NSA_HEADS * HEAD_DIM), F32)
    return pl.pallas_call(
        functools.partial(_flash_prompt_body, tq=tq),
        grid=(n_batch, nq),
        in_specs=[pl.BlockSpec((NSA_HEADS * LANES, tq), lambda n, i: (0, n * nq + i)),
                  k_spec(2), vt_spec(NSA_HEADS), k_spec(4), vt_spec(NSA_HEADS + 1),
                  pl.BlockSpec((1, NSA_KV_HEADS, n_sel_pad, tq), lambda n, i: (n, 0, 0, i)),
                  full(near), full(winfar)],
        out_specs=[out_spec, out_spec],
        out_shape=[out, out],
        scratch_shapes=state + state,
        compiler_params=_cparams("parallel", "parallel"),
        name="flash_prompt",
    )(qvt, kvb, qvt, kvb, qvt, sel_t, near, winfar)


def _flash_update(q, k, v, bias, mask, m_ref, l_ref, acc_ref):
    s = _dot_nt(q, k)
    if bias is not None:
        s = s + bias
    if mask is not None:
        s = jnp.where(mask, s, NEG_INF)
    m_prev = m_ref[...]
    m_new = jnp.maximum(m_prev, jnp.max(s, axis=1, keepdims=True))
    alpha = jnp.exp(m_prev - m_new)
    p = jnp.exp(s - m_new)
    l_ref[...] = alpha * l_ref[...] + jnp.sum(p, axis=1, keepdims=True)
    acc_ref[...] = alpha * acc_ref[...] + _dot(p.astype(BF16), v)
    m_ref[...] = m_new


def _flash_init(m_ref, l_ref, acc_ref):
    m_ref[...] = jnp.full(m_ref.shape, M_INIT, F32)
    l_ref[...] = jnp.zeros(l_ref.shape, F32)
    acc_ref[...] = jnp.zeros(acc_ref.shape, F32)


def _flash_sample_body(pt_ref, q_ref, kpool_ref, vpool_ref, ksn_ref, vsn_ref, sel_ref, e_ref, tailb_ref, newb_ref,
                       kw_ref, vw_ref, winb_ref, osel_ref, owin_ref, kbuf, vbuf, sem, m_r, l_r, acc_r, *, tk,
                       n_pages, page):
    slot = _fetch_pages(pt_ref, (kpool_ref, vpool_ref), (kbuf, vbuf), sem, n_pages, page)
    q = q_ref[0]
    nt = n_pages * page // tk

    def sel_tile(kt, tail):
        k0 = pl.multiple_of(kt * tk, tk)
        mask = _dot(sel_ref[0], e_ref[:, pl.ds(k0, tk)]) > 0.5
        _flash_update(q, kbuf[slot, pl.ds(k0, tk), :].astype(BF16), vbuf[slot, pl.ds(k0, tk), :].astype(BF16),
                      tailb_ref[...] if tail else None, mask, m_r, l_r, acc_r)

    def far_body(kt, carry):
        sel_tile(kt, False)
        return carry

    _flash_init(m_r, l_r, acc_r)
    lax.fori_loop(0, nt - 1, far_body, 0)
    sel_tile(nt - 1, True)
    _flash_update(q, ksn_ref[0], vsn_ref[0], newb_ref[...], None, m_r, l_r, acc_r)
    osel_ref[0] = acc_r[...] * (1.0 / l_r[...])

    _flash_init(m_r, l_r, acc_r)
    _flash_update(q, kw_ref[0].astype(BF16), vw_ref[0].astype(BF16), winb_ref[...], None, m_r, l_r, acc_r)
    _flash_update(q, ksn_ref[1], vsn_ref[1], newb_ref[...], None, m_r, l_r, acc_r)
    owin_ref[0] = acc_r[...] * (1.0 / l_r[...])


def _flash_sample(page_table, qs, kpool, vpool, kvn, selrows, e_all, tailb, newb, kw, vw, winb, tk):
    n, n_pages = page_table.shape
    page = kpool.shape[1]
    past = n_pages * page
    per_n = lambda a: pl.BlockSpec((1,) + a.shape[1:], lambda i, pt: (i,) + (0,) * (a.ndim - 1))
    full = lambda a: pl.BlockSpec(a.shape, lambda i, pt: (0,) * a.ndim)
    any_spec = pl.BlockSpec(memory_space=pl.ANY)
    new_spec = pl.BlockSpec((2, LANES, LANES), lambda i, pt: (i, 0, 0))
    out = jax.ShapeDtypeStruct((n, LANES, LANES), F32)
    grid_spec = pltpu.PrefetchScalarGridSpec(
        num_scalar_prefetch=1,
        grid=(n,),
        in_specs=[per_n(qs), any_spec, any_spec, new_spec, new_spec, per_n(selrows), full(e_all), full(tailb),
                  full(newb), per_n(kw), per_n(vw), full(winb)],
        out_specs=[per_n(out), per_n(out)],
        scratch_shapes=[pltpu.VMEM((2, past, LANES), F32), pltpu.VMEM((2, past, LANES), F32),
                        pltpu.SemaphoreType.DMA((2, 2)),
                        pltpu.VMEM((LANES, 1), F32), pltpu.VMEM((LANES, 1), F32), pltpu.VMEM((LANES, LANES), F32)],
    )
    return pl.pallas_call(
        functools.partial(_flash_sample_body, tk=tk, n_pages=n_pages, page=page),
        grid_spec=grid_spec,
        out_shape=[out, out],
        compiler_params=_cparams("arbitrary"),
        name="flash_sample",
    )(page_table, qs, kpool, vpool, kvn[0], kvn[1], selrows, e_all, tailb, newb, kw, vw, winb)


def _mixout_body(an_ref, oc_ref, os_ref, ow_ref, gate_ref, x_ref, gb_ref, woa_ref, wob_ref, y_ref):
    gates = gate_ref[...]
    tm = gates.shape[0]
    lane = lax.broadcasted_iota(jnp.int32, (tm, LANES), 1)
    tiles = []
    for j in range(NSA_HEADS // 2):
        cols = slice(LANES * j, LANES * (j + 1))
        acc = None
        for br, o_ref in enumerate((oc_ref, os_ref, ow_ref)):
            c0 = 3 * (2 * j) + br
            c1 = 3 * (2 * j + 1) + br
            gcol = jnp.where(lane < HEAD_DIM, gates[:, c0:c0 + 1], gates[:, c1:c1 + 1])
            term = gcol * o_ref[:, cols]
            acc = term if acc is None else acc + term
        tiles.append(acc)
    bn = _rms(jnp.concatenate(tiles, axis=1), gb_ref[...]).astype(BF16)
    y_ref[...] = x_ref[...] + _dot(an_ref[...].astype(BF16), woa_ref[...]) + _dot(bn, wob_ref[...])


def _mixout(an, oc, osel, owin, gates, x, g_b, wo_a, wo_b, tm):
    m, d = x.shape
    row = lambda a: pl.BlockSpec((tm, a.shape[1]), lambda i: (i, 0))
    full = lambda a: pl.BlockSpec(a.shape, lambda i: (0,) * a.ndim)
    return pl.pallas_call(
        _mixout_body,
        grid=(m // tm,),
        in_specs=[row(an), row(oc), row(osel), row(owin), row(gates), row(x), full(g_b), full(wo_a), full(wo_b)],
        out_specs=row(x),
        out_shape=jax.ShapeDtypeStruct((m, d), F32),
        compiler_params=_cparams("parallel"),
        name="mix_out",
    )(an, oc, osel, owin, gates, x, g_b, wo_a, wo_b)


def _normmm_body(x_ref, g_ref, w_ref, o_ref):
    o_ref[...] = _dot(_rms(x_ref[...], g_ref[...]).astype(BF16), w_ref[...]).astype(o_ref.dtype)


def _norm_matmul(x, g, w, out_dtype, tm):
    m, d = x.shape
    n = w.shape[1]
    return pl.pallas_call(
        _normmm_body,
        grid=(m // tm,),
        in_specs=[pl.BlockSpec((tm, d), lambda i: (i, 0)), pl.BlockSpec(g.shape, lambda i: (0, 0)),
                  pl.BlockSpec(w.shape, lambda i: (0, 0))],
        out_specs=pl.BlockSpec((tm, n), lambda i: (i, 0)),
        out_shape=jax.ShapeDtypeStruct((m, n), out_dtype),
        compiler_params=_cparams("parallel"),
        name="norm_matmul",
    )(x, g, w)


def _mmres_body(a_ref, w_ref, r_ref, o_ref):
    o_ref[...] = r_ref[...] + _dot(a_ref[...].astype(BF16), w_ref[...])


def _matmul_residual(a, w, res, tm):
    m, k = a.shape
    n = w.shape[1]
    return pl.pallas_call(
        _mmres_body,
        grid=(m // tm,),
        in_specs=[pl.BlockSpec((tm, k), lambda i: (i, 0)), pl.BlockSpec(w.shape, lambda i: (0, 0)),
                  pl.BlockSpec((tm, n), lambda i: (i, 0))],
        out_specs=pl.BlockSpec((tm, n), lambda i: (i, 0)),
        out_shape=jax.ShapeDtypeStruct((m, n), F32),
        compiler_params=_cparams("parallel"),
        name="matmul_residual",
    )(a, w, res)


def _xattn_body(q_ref, mk_ref, mv_ref, o_ref, *, n_heads):
    hd = q_ref.shape[2] // n_heads
    scale = hd ** -0.5
    for h in range(n_heads):
        cols = slice(hd * h, hd * (h + 1))
        s = _dot_nt(q_ref[0, :, cols], mk_ref[0, :, cols].astype(BF16)) * scale
        e = jnp.exp(s - jnp.max(s, axis=1, keepdims=True))
        p = e * (1.0 / jnp.sum(e, axis=1, keepdims=True))
        o_ref[0, :, cols] = _dot(p.astype(BF16), mv_ref[0, :, cols].astype(BF16))


def _xattn_core(q, mk, mv, tm):
    n, t, d = q.shape
    mem = mk.shape[1]
    return pl.pallas_call(
        functools.partial(_xattn_body, n_heads=XA_HEADS),
        grid=(n, t // tm),
        in_specs=[pl.BlockSpec((1, tm, d), lambda b, i: (b, i, 0)), pl.BlockSpec((1, mem, d), lambda b, i: (b, 0, 0)),
                  pl.BlockSpec((1, mem, d), lambda b, i: (b, 0, 0))],
        out_specs=pl.BlockSpec((1, tm, d), lambda b, i: (b, i, 0)),
        out_shape=jax.ShapeDtypeStruct((n, t, d), F32),
        compiler_params=_cparams("parallel", "parallel"),
        name="xattn_core",
    )(q, mk, mv)


def _ffn_body(x_ref, g_ref, wu_ref, wg_ref, cw_ref, cb_ref, wd_ref, gf_ref, s1_ref, s2_ref, y_ref, a_ref, h_scr,
              acc_scr, carry_scr, *, tf, dec_seq):
    i = pl.program_id(1)
    j = pl.program_id(2)
    tm = x_ref.shape[0]

    @pl.when(j == 0)
    def _():
        h_scr[...] = _rms(x_ref[...], g_ref[...]).astype(BF16)
        acc_scr[...] = jnp.zeros(acc_scr.shape, F32)

    a = _dot(h_scr[...], wu_ref[...])
    gt = _dot(h_scr[...], wg_ref[...])
    row = lax.broadcasted_iota(jnp.int32, (tm, tf), 0)
    r1 = pltpu.roll(a, 1, 0)
    r2 = pltpu.roll(a, 2, 0)
    if dec_seq is None:
        cols = pl.ds(pl.multiple_of(j * tf, LANES), tf)

        @pl.when(i == 0)
        def _():
            carry_scr[:, cols] = jnp.zeros((SUBLANES, tf), F32)

        p0 = carry_scr[SUBLANES - 2:SUBLANES - 1, cols]
        p1 = carry_scr[SUBLANES - 1:SUBLANES, cols]
        a1 = jnp.where(row == 0, p1, r1)
        a2 = jnp.where(row == 0, p0, jnp.where(row == 1, p1, r2))
        carry_scr[:, cols] = a[tm - SUBLANES:tm, :]
        a_ref[0] = a[tm - SUBLANES:tm, :]
    else:
        t = row & (dec_seq - 1)
        a1 = jnp.where(t == 0, s1_ref[...], r1)
        a2 = jnp.where(t < 2, s2_ref[...], r2)
        a_ref[...] = a
    c = cb_ref[...] + a2 * cw_ref[0:1, :] + a1 * cw_ref[1:2, :] + a * cw_ref[2:3, :]
    acc_scr[...] += _dot((_gelu(c) * gt).astype(BF16), wd_ref[...])

    @pl.when(j == pl.num_programs(2) - 1)
    def _():
        y_ref[...] = _rms(x_ref[...] + acc_scr[...], gf_ref[...])


def _ffn(x, g, wu, wg, cw, cb, wd, g_final, s1, s2, n_seq, tm, tf, dec_seq):
    m, d = x.shape
    f = wu.shape[1]
    nt = m // n_seq // tm
    nf = f // tf
    row = pl.BlockSpec((tm, d), lambda n, i, j: (n * nt + i, 0))
    vec = lambda a: pl.BlockSpec(a.shape, lambda n, i, j: (0,) * a.ndim)
    fcol = lambda a: pl.BlockSpec((a.shape[0], tf), lambda n, i, j: (0, j))
    if dec_seq is None:
        st_spec = pl.BlockSpec((SUBLANES, tf), lambda n, i, j: (0, j))
        a_spec = pl.BlockSpec((1, SUBLANES, tf), lambda n, i, j: (n * nt + i, 0, j))
        a_shape = jax.ShapeDtypeStruct((n_seq * nt, SUBLANES, f), F32)
    else:
        st_spec = pl.BlockSpec((tm, tf), lambda n, i, j: (n * nt + i, j))
        a_spec = st_spec
        a_shape = jax.ShapeDtypeStruct((m, f), F32)
    return pl.pallas_call(
        functools.partial(_ffn_body, tf=tf, dec_seq=dec_seq),
        grid=(n_seq, nt, nf),
        in_specs=[row, vec(g), fcol(wu), fcol(wg), fcol(cw), fcol(cb),
                  pl.BlockSpec((tf, d), lambda n, i, j: (j, 0)), vec(g_final), st_spec, st_spec],
        out_specs=[row, a_spec],
        out_shape=[jax.ShapeDtypeStruct((m, d), F32), a_shape],
        scratch_shapes=[pltpu.VMEM((tm, d), BF16), pltpu.VMEM((tm, d), F32), pltpu.VMEM((SUBLANES, f), F32)],
        compiler_params=_cparams("parallel", "arbitrary", "arbitrary"),
        name="conv_ffn",
    )(x, g, wu, wg, cw, cb, wd, g_final, s1, s2)


def _t5_bucket_np(dist):
    n = np.maximum(dist, 0)
    max_exact = REL_BUCKETS // 2
    nf = np.maximum(n, 1).astype(np.float64)
    large = max_exact + (np.log(nf / max_exact) / math.log(REL_MAX_DIST / max_exact)
                         * (REL_BUCKETS - max_exact)).astype(np.int32)
    large = np.minimum(large, REL_BUCKETS - 1)
    return np.where(n < max_exact, n, large).astype(np.int32)


def _bias_vectors(rel_bias, dist, live, masked):
    dist, live, masked = np.broadcast_arrays(dist, live, masked)
    onehot = np.zeros(dist.shape + (REL_BUCKETS,), np.float32)
    np.put_along_axis(onehot, _t5_bucket_np(dist)[..., None], 1.0, axis=-1)
    onehot[..., REL_BUCKETS - 1] -= 1.0
    onehot *= live[..., None]
    add = np.where(masked, NEG_INF, 0.0).astype(np.float32)
    table = jnp.einsum("...b,bh->h...", jnp.asarray(onehot), rel_bias, precision=lax.Precision.HIGHEST)
    return table + add


def _toeplitz(v, rows, cols, stride):
    length = v.shape[-1]
    w = length - stride
    assert cols <= w
    t = jnp.tile(v, (1,) * (v.ndim - 1) + (rows,))[..., :rows * w]
    return t.reshape(v.shape[:-1] + (rows, w))[..., :cols]


def _bias_tables_prompt(rel_bias, tq):
    k = np.arange(2 * tq)
    upper = k < tq
    diag = _bias_vectors(rel_bias, k, upper, ~upper)
    prev = _bias_vectors(rel_bias, np.where(upper, k + tq, k - tq), True, False)
    near = jnp.stack([_toeplitz(prev, tq, tq, 1), _toeplitz(diag, tq, tq, 1)])
    winfar = _toeplitz(jnp.asarray(np.where(upper, NEG_INF, 0.0), F32), tq, tq, 1)
    length = 4 * LANES
    off = 2 * SUBLANES * CMP_STRIDE - (CMP_LEN - 1)
    kk = np.arange(length)
    d = np.where(kk < LANES, kk + off, kk - length + off)
    band = _toeplitz(_bias_vectors(rel_bias, d, (d >= 0) & ((kk < LANES) | (kk >= LANES + CMP_STRIDE)), False),
                     3 * SUBLANES, LANES, CMP_STRIDE)
    return near, winfar, band


def _bias_tables_sample(rel_bias, past, dec_seq, n_cmp, n_sub, tk, win_buf):
    t = np.arange(dec_seq)[:, None]

    def rows(dist, live, masked):
        tab = _bias_vectors(rel_bias, dist, live, masked)
        tab = tab.reshape(NSA_HEADS * dec_seq, -1)
        return jnp.pad(tab, ((0, LANES - NSA_HEADS * dec_seq), (0, 0)))

    c = np.arange(n_sub)[None, :]
    d_cmp = past + t - (c * CMP_STRIDE + CMP_LEN - 1)
    ok = (d_cmp >= 0) & (c < n_cmp)
    cmp_t = rows(d_cmp, ok, ~ok).T
    tail = rows(tk + t - np.arange(tk)[None, :], True, False)
    jn = np.arange(LANES)[None, :]
    ok = (jn <= t) & (jn < dec_seq)
    newb = rows(t - jn, ok, ~ok)
    d_win = win_buf + t - np.arange(win_buf)[None, :]
    ok = d_win < WINDOW
    winb = rows(d_win, ok, ~ok)
    return cmp_t, tail, newb, winb


def _row_heads(dec_seq):
    rows = np.arange(LANES)
    live = rows < NSA_HEADS * dec_seq
    return np.where(live, rows // dec_seq, 0), rows % dec_seq, live


def _prep_in_proj(w_in, d_a):
    d_b = NSA_HEADS * HEAD_DIM
    o_q = 2 * d_a
    o_kv = o_q + d_b
    o_g = o_kv + 6 * NSA_KV_HEADS * HEAD_DIM
    wq = w_in[:, o_q:o_kv].reshape(-1, NSA_HEADS, HEAD_DIM) * ATT_SCALE
    slot = jnp.zeros((w_in.shape[0], NSA_HEADS, NSA_KV_HEADS, HEAD_DIM), w_in.dtype)
    for h in range(NSA_HEADS):
        slot = slot.at[:, h, h // HPG].set(wq[:, h])
    wg = jnp.pad(w_in[:, o_g:], ((0, 0), (0, LANES - (w_in.shape[1] - o_g))))
    w_pad = jnp.concatenate([w_in[:, :o_q], slot.reshape(w_in.shape[0], -1), w_in[:, o_kv:o_g], wg], axis=1)
    kvw = NSA_KV_HEADS * HEAD_DIM
    w_vt = jnp.concatenate([slot.reshape(w_in.shape[0], -1), w_in[:, o_kv + 3 * kvw:o_kv + 4 * kvw],
                            w_in[:, o_kv + 5 * kvw:o_kv + 6 * kvw]], axis=1).T
    return w_pad.astype(BF16), w_vt.astype(BF16)


def _prep_compress(w1, b1, w2, b2, pe):
    eye = jnp.eye(NSA_KV_HEADS, dtype=w1.dtype)
    hid = w1.shape[-1]
    kvw = NSA_KV_HEADS * HEAD_DIM
    w1p = jnp.einsum("apdh,gk->pgdakh", w1.reshape(2, CMP_STRIDE, HEAD_DIM, hid), eye)
    w1p = w1p.reshape(CMP_STRIDE, kvw, 2 * NSA_KV_HEADS * hid)
    pe_rows = jnp.broadcast_to(pe.reshape(2, CMP_STRIDE, 1, HEAD_DIM), (2, CMP_STRIDE, NSA_KV_HEADS, HEAD_DIM))
    pe_rows = jnp.pad(pe_rows.reshape(2, CMP_STRIDE, kvw).transpose(1, 0, 2), ((0, 0), (0, 2 * SUBLANES - 2), (0, 0)))
    w2big = jnp.einsum("hd,gk->ghkd", w2, eye).reshape(NSA_KV_HEADS * hid, kvw)
    return (w1p.astype(BF16), pe_rows.astype(BF16), jnp.tile(b1, NSA_KV_HEADS)[None, :], w2big.astype(BF16),
            jnp.tile(b2, NSA_KV_HEADS)[None, :])


def _prep_compress_kv(p):
    w1k, pek, b1k, w2k, b2k = _prep_compress(p["w_cmp1_k"], p["b_cmp1_k"], p["w_cmp2_k"], p["b_cmp2_k"], p["pe_cmp_k"])
    w1v, pev, b1v, w2v, b2v = _prep_compress(p["w_cmp1_v"], p["b_cmp1_v"], p["w_cmp2_v"], p["b_cmp2_v"], p["pe_cmp_v"])
    return dict(w1k=w1k, w1v=w1v, pek=pek, pev=pev, b1k=b1k, b1v=b1v, w2k=w2k, w2vt=w2v.T, b2k=b2k, b2vt=b2v.T)


def _block_expand(n_sel_pad, n_keys):
    s = np.arange(n_sel_pad)[:, None]
    k = np.arange(n_keys)[None, :]
    return jnp.asarray((k // SEL_BLOCK == s), dtype=BF16)


def _round_up(x, m):
    return -(-x // m) * m


def _shared_front(x2d, p, w_in_pad, w_vt, period, mix_w, mix_b, tm):
    d_a = p["g_a"].shape[0]
    u, v, qpad, kc, vc, ks, vs, kw, vw, kvb, vt, gates = _in_proj(x2d, p["g_mix"][None, :], w_in_pad, w_vt, d_a, tm)
    an, v_rows = _gmlp(u, v, p["ln_v_g"][None, :], p["ln_v_b"][None, :], mix_w, mix_b, p["g_a"][None, :], period, tm)
    return an, v_rows, qpad, (kc, vc, ks, vs, kw, vw), kvb, vt, gates


def _shared_back(x2d, an, oc, osel, owin, gates, p, wts, mk, mv, n_seq, s1, s2, g_final, dec_seq, tm):
    d = x2d.shape[1]
    x1 = _mixout(an, oc, osel, owin, gates, x2d, p["g_b"][None, :], wts["wo_a"], wts["wo_b"], tm)
    qx = _norm_matmul(x1, p["g_xa"][None, :], wts["w_xq"], BF16, tm)
    t = x1.shape[0] // n_seq
    if dec_seq is None:
        ox = _xattn_core(qx.reshape(n_seq, t, d), mk, mv, tm).reshape(-1, d)
    else:
        qx3 = jnp.pad(qx.reshape(n_seq, t, d), ((0, 0), (0, SUBLANES - t), (0, 0)))
        ox = _xattn_core(qx3, mk, mv, SUBLANES)[:, :t].reshape(-1, d)
    x2 = _matmul_residual(ox, wts["w_xo"], x1, tm)
    n_ffn_seq = n_seq if dec_seq is None else 1
    return _ffn(x2, p["g_ffn"][None, :], wts["w_up"], wts["w_gate"], wts["conv_w"], p["conv_b"][None, :],
                wts["w_down"], g_final[None, :], s1, s2, n_ffn_seq, tm, wts["tf"], dec_seq)


def _heads_to_rows(qpad, n, t):
    q = qpad.reshape(n, t, NSA_HEADS, LANES).transpose(0, 2, 1, 3).reshape(n, NSA_HEADS * t, LANES)
    return jnp.pad(q, ((0, 0), (0, LANES - NSA_HEADS * t), (0, 0)))


def _rows_to_tokens(o, n, t):
    o = o[:, :NSA_HEADS * t].reshape(n, NSA_KV_HEADS, HPG, t, NSA_KV_HEADS, HEAD_DIM)
    o = jnp.stack([o[:, g, :, :, g] for g in range(NSA_KV_HEADS)], axis=1)
    return o.transpose(0, 3, 1, 2, 4).reshape(n * t, NSA_HEADS * HEAD_DIM)


def kernel(x_prompt, x_sample, mem_prompt, cache_cmp_k, cache_cmp_v, cache_sel_k, cache_sel_v, cache_win_k,
           cache_win_v, cache_mem_k, cache_mem_v, state_conv, page_table, g_mix, w_in, w_s, b_s, ln_v_g, ln_v_b,
           w_cmp1_k, b_cmp1_k, w_cmp2_k, b_cmp2_k, pe_cmp_k, w_cmp1_v, b_cmp1_v, w_cmp2_v, b_cmp2_v, pe_cmp_v,
           rel_bias, g_a, g_b, w_o, g_xa, g_mem, w_xq, w_mk, w_mv, w_xo, g_ffn, w_up, w_gate, conv_w, conv_b,
           w_down, g_final):
    depth = w_in.shape[0]
    assert depth == 1, "the layer loop is written for a single layer"
    bsz, seq, d = x_prompt.shape
    nd, dec_seq, _ = x_sample.shape
    mem_len = mem_prompt.shape[1]
    d_a = g_a.shape[1]
    d_ff = w_up.shape[2]
    n_pages = page_table.shape[1]
    page = cache_cmp_k.shape[2]
    past = n_pages * page
    kvw = NSA_KV_HEADS * HEAD_DIM
    win_buf = cache_win_k.shape[2]
    assert kvw == LANES and seq % 256 == 0 and past % 512 == 0 and dec_seq == 4 and nd * dec_seq % LANES == 0
    tm = 512
    tq = 256
    tk_s = 512
    tf = d_ff // 2 if (d_ff // 2) % LANES == 0 else d_ff

    l = 0
    p = dict(g_mix=g_mix[l], ln_v_g=ln_v_g[l], ln_v_b=ln_v_b[l], w_cmp1_k=w_cmp1_k[l], b_cmp1_k=b_cmp1_k[l],
             w_cmp2_k=w_cmp2_k[l], b_cmp2_k=b_cmp2_k[l], pe_cmp_k=pe_cmp_k[l], w_cmp1_v=w_cmp1_v[l],
             b_cmp1_v=b_cmp1_v[l], w_cmp2_v=w_cmp2_v[l], b_cmp2_v=b_cmp2_v[l], pe_cmp_v=pe_cmp_v[l], g_a=g_a[l],
             g_b=g_b[l], g_xa=g_xa[l], g_ffn=g_ffn[l], conv_b=conv_b[l])
    w_in_pad, w_vt = _prep_in_proj(w_in[l], d_a)
    cw = _prep_compress_kv(p)
    wts = dict(wo_a=w_o[l, :d_a].astype(BF16), wo_b=w_o[l, d_a:].astype(BF16), w_xq=w_xq[l].astype(BF16),
               w_xo=w_xo[l].astype(BF16), w_up=w_up[l].astype(BF16), w_gate=w_gate[l].astype(BF16),
               w_down=w_down[l].astype(BF16), conv_w=jnp.pad(conv_w[l], ((0, SUBLANES - CONV_W), (0, 0))), tf=tf)
    bias_lanes = jnp.repeat(b_s[l].T, d_a // A_GROUPS, axis=1)

    xp = x_prompt.reshape(bsz * seq, d)
    an, _, _, kv6, kvb, qvt, gates = _shared_front(xp, p, w_in_pad, w_vt, CHUNK, w_s[l], bias_lanes, tm)
    kc, vct = _compress_prompt(kv6[0], kv6[1], cw, bsz, seq)
    n_sel = seq // SEL_BLOCK
    n_sel_pad = _round_up(n_sel, LANES)
    near, winfar, band = _bias_tables_prompt(rel_bias, tq)
    oc, sel_t = _cmpsel_prompt(qvt, kc, vct, band, bsz, seq, n_sel_pad, min(SEL_TOP, n_sel))
    osel, owin = _flash_prompt(kvb, qvt, sel_t, near, winfar, bsz, seq, tq)
    mem_kv = _norm_matmul(mem_prompt.reshape(bsz * mem_len, d), g_mem[l][None, :],
                          jnp.concatenate([w_mk[l], w_mv[l]], axis=1).astype(BF16), F32, min(tm, bsz * mem_len))
    mk_p = mem_kv[:, :d].reshape(bsz, mem_len, d)
    mv_p = mem_kv[:, d:].reshape(bsz, mem_len, d)
    zero_state = jnp.zeros((SUBLANES, d_ff), F32)
    yp, a_tail = _shared_back(xp, an, oc, osel, owin, gates, p, wts, mk_p, mv_p, bsz, zero_state, zero_state,
                              g_final, None, tm)
    keep = min(WINDOW, seq)
    shp = lambda a: a.reshape(1, bsz, seq, NSA_KV_HEADS, HEAD_DIM)
    prompt_state = (shp(kv6[0]), shp(kv6[1]), shp(kv6[2]), shp(kv6[3]), shp(kv6[4])[:, :, -keep:],
                    shp(kv6[5])[:, :, -keep:], mk_p.reshape(1, bsz, mem_len, XA_HEADS, d // XA_HEADS),
                    mv_p.reshape(1, bsz, mem_len, XA_HEADS, d // XA_HEADS),
                    a_tail.reshape(bsz, -1, SUBLANES, d_ff)[None, :, -1, SUBLANES - (CONV_W - 1):])

    xs = x_sample.reshape(nd * dec_seq, d)
    reps = CHUNK // dec_seq
    mix_w_s = jnp.tile(w_s[l][:, :dec_seq, :dec_seq], (1, reps, reps))
    bias_s = jnp.repeat(jnp.tile(b_s[l][:, :dec_seq].T, (reps, 1)), d_a // A_GROUPS, axis=1)
    tms = min(tm, nd * dec_seq)
    an_s, v_rows, qpad_s, kv6s, kvb_s, _, gates_s = _shared_front(xs, p, w_in_pad, w_vt, dec_seq, mix_w_s, bias_s,
                                                                 tms)
    pools = [c[l].reshape(c.shape[1], page, kvw) for c in (cache_cmp_k, cache_cmp_v, cache_sel_k, cache_sel_v)]
    kc_s, vct_s = _compress_sample(page_table, pools[0], pools[1], cw)
    n_sub_s = past // CMP_STRIDE
    n_cmp_s = (past + dec_seq) // CMP_STRIDE - 1
    n_sel_s = -(-(past + dec_seq) // SEL_BLOCK)
    n_sel_pad_s = _round_up(n_sel_s, LANES)
    cmp_t, tail, newb, winb = _bias_tables_sample(rel_bias, past, dec_seq, n_cmp_s, n_sub_s, tk_s, win_buf)
    h_of, t_of, live = _row_heads(dec_seq)
    same = (live[:, None] & live[None, :] & ((h_of[:, None] // HPG) == (h_of[None, :] // HPG))
            & (t_of[:, None] == t_of[None, :]))
    qs = _heads_to_rows(qpad_s, nd, dec_seq)
    oct_s, selrows = _cmpsel_sample(qs, kc_s, vct_s, cmp_t, jnp.asarray(same, dtype=BF16), n_sel_pad_s,
                                    min(SEL_TOP, n_sel_s), past, dec_seq)
    newrows = lambda a: jnp.pad(a.reshape(nd, dec_seq, LANES), ((0, 0), (0, LANES - dec_seq), (0, 0)))
    kvn = (jnp.stack([newrows(kvb_s[:, 2 * LANES:3 * LANES]), newrows(kvb_s[:, 4 * LANES:5 * LANES])], axis=1)
           .reshape(2 * nd, LANES, LANES),
           jnp.stack([newrows(kvb_s[:, 3 * LANES:4 * LANES]), newrows(kvb_s[:, 5 * LANES:6 * LANES])], axis=1)
           .reshape(2 * nd, LANES, LANES))
    wk = cache_win_k[l].reshape(nd, win_buf, kvw)
    wv = cache_win_v[l].reshape(nd, win_buf, kvw)
    osel_r, owin_r = _flash_sample(page_table, qs, pools[2], pools[3], kvn, selrows,
                                   _block_expand(n_sel_pad_s, past), tail, newb, wk, wv, winb, tk_s)
    oc_s = _rows_to_tokens(oct_s.transpose(0, 2, 1), nd, dec_seq)
    osel_s = _rows_to_tokens(osel_r, nd, dec_seq)
    owin_s = _rows_to_tokens(owin_r, nd, dec_seq)
    st = state_conv[l]
    zrow = jnp.zeros_like(st[:, :1])
    s1 = jnp.concatenate([st[:, 1:2]] + [zrow] * (dec_seq - 1), axis=1).reshape(nd * dec_seq, d_ff)
    s2 = jnp.concatenate([st[:, 0:1], st[:, 1:2]] + [zrow] * (dec_seq - 2), axis=1).reshape(nd * dec_seq, d_ff)
    mk_s = cache_mem_k[l].reshape(nd, mem_len, d)
    mv_s = cache_mem_v[l].reshape(nd, mem_len, d)
    ys, a_full = _shared_back(xs, an_s, oc_s, osel_s, owin_s, gates_s, p, wts, mk_s, mv_s, nd, s1, s2, g_final,
                              dec_seq, tms)
    keep_s = min(WINDOW, past + dec_seq)
    shs = lambda a: a.reshape(1, nd, dec_seq, NSA_KV_HEADS, HEAD_DIM)
    win_new = lambda cache, new: jnp.concatenate([cache[l], shs(new)[0]], axis=1)[None, :, -keep_s:]
    sample_state = (shs(kv6s[0]), shs(kv6s[1]), shs(kv6s[2]), shs(kv6s[3]), win_new(cache_win_k, kv6s[4]),
                    win_new(cache_win_v, kv6s[5]), v_rows.reshape(1, nd, dec_seq, d_a),
                    a_full.reshape(1, nd, dec_seq, d_ff)[:, :, dec_seq - (CONV_W - 1):])

    return (yp.reshape(bsz, seq, d), ys.reshape(nd, dec_seq, d)) + prompt_state + sample_state
```

```python
import functools
import math

import numpy as np
import jax
import jax.numpy as jnp
from jax import lax
from jax.experimental import pallas as pl
from jax.experimental.pallas import tpu as pltpu

F32 = jnp.float32
BF16 = jnp.bfloat16

LANES = 128
SUBLANES = 8
VMEM_LIMIT_BYTES = 56 * 1024 * 1024

NORM_EPS = 1e-6
NEG_INF = -1e30
M_INIT = -1e29
KNOCKED_OUT = -3e38
FORCE_BONUS = 1e4

A_GROUPS = 8
CHUNK = 128
NSA_HEADS = 8
HEAD_DIM = 64
NSA_KV_HEADS = 2
HPG = NSA_HEADS // NSA_KV_HEADS
CMP_LEN = 32
CMP_STRIDE = 16
SEL_BLOCK = 64
SEL_TOP = 16
WINDOW = 512
REL_BUCKETS = 32
REL_MAX_DIST = 128
XA_HEADS = 4
CONV_W = 3
ATT_SCALE = HEAD_DIM ** -0.5

_NT = (((1,), (1,)), ((), ()))


def _cparams(*sem):
    return pltpu.CompilerParams(dimension_semantics=sem, vmem_limit_bytes=VMEM_LIMIT_BYTES)


def _gelu(x):
    return 0.5 * x * (1.0 + jnp.tanh(0.7978845608028654 * (x + 0.044715 * (x * x * x))))


def _rms(x, g):
    return x * lax.rsqrt(jnp.mean(x * x, axis=-1, keepdims=True) + NORM_EPS) * g


def _dot(a, b):
    return jnp.dot(a, b, preferred_element_type=F32)


def _dot_nt(a, b):
    return lax.dot_general(a, b, _NT, preferred_element_type=F32)


def _inproj_body(x_ref, g_ref, w_ref, wvt_ref, u_ref, v_ref, q_ref, kc_ref, vc_ref, ks_ref, vs_ref, kw_ref,
                 vw_ref, kvb_ref, vt_ref, gate_ref, *, d_a):
    h = _rms(x_ref[...], g_ref[...]).astype(BF16)

    def mm(lo, hi):
        return _dot(h, w_ref[:, lo:hi])

    u_ref[...] = mm(0, d_a)
    v_ref[...] = mm(d_a, 2 * d_a)
    o = 2 * d_a
    q_ref[...] = mm(o, o + NSA_HEADS * LANES).astype(BF16)
    o += NSA_HEADS * LANES
    for i, r in enumerate((kc_ref, vc_ref, ks_ref, vs_ref, kw_ref, vw_ref)):
        z = mm(o + LANES * i, o + LANES * (i + 1))
        r[...] = z
        kvb_ref[:, LANES * i:LANES * (i + 1)] = z.astype(BF16)
    o += 6 * LANES
    gate_ref[...] = 1.0 / (1.0 + jnp.exp(-mm(o, o + LANES)))
    vt_ref[...] = _dot_nt(wvt_ref[...], h).astype(BF16)


def _in_proj(x, g, w_pad, w_vt, d_a, tm):
    m, d = x.shape
    row = lambda n: pl.BlockSpec((tm, n), lambda i: (i, 0))
    full = lambda a: pl.BlockSpec(a.shape, lambda i: (0,) * a.ndim)
    kv = jax.ShapeDtypeStruct((m, LANES), F32)
    return pl.pallas_call(
        functools.partial(_inproj_body, d_a=d_a),
        grid=(m // tm,),
        in_specs=[row(d), full(g), full(w_pad), full(w_vt)],
        out_specs=[row(d_a), row(d_a), row(NSA_HEADS * LANES)] + [row(LANES)] * 6
                  + [row(6 * LANES), pl.BlockSpec((w_vt.shape[0], tm), lambda i: (0, i)), row(LANES)],
        out_shape=[jax.ShapeDtypeStruct((m, d_a), F32), jax.ShapeDtypeStruct((m, d_a), F32),
                   jax.ShapeDtypeStruct((m, NSA_HEADS * LANES), BF16)] + [kv] * 6
                  + [jax.ShapeDtypeStruct((m, 6 * LANES), BF16), jax.ShapeDtypeStruct((w_vt.shape[0], m), BF16),
                     jax.ShapeDtypeStruct((m, LANES), F32)],
        compiler_params=_cparams("parallel"),
        name="in_proj",
    )(x, g, w_pad, w_vt)


def _gmlp_body(u_ref, v_ref, lng_ref, lnb_ref, w_ref, bias_ref, ga_ref, a_ref, vr_ref, *, period_log2, rows):
    ri = lax.broadcasted_iota(jnp.int32, (CHUNK, CHUNK), 0)
    ci = lax.broadcasted_iota(jnp.int32, (CHUNK, CHUNK), 1)
    mask = (ci <= ri) & ((ri >> period_log2) == (ci >> period_log2))
    wm = [jnp.where(mask, w_ref[g], 0.0).astype(BF16) for g in range(A_GROUPS)]
    lane = lax.broadcasted_iota(jnp.int32, (CHUNK, LANES), 1)
    for c in range(rows // CHUNK):
        rs = slice(CHUNK * c, CHUNK * (c + 1))
        gv = _gelu(v_ref[rs, :])
        mu = jnp.mean(gv, axis=-1, keepdims=True)
        var = jnp.mean(jnp.square(gv - mu), axis=-1, keepdims=True)
        vn = (gv - mu) * lax.rsqrt(var + NORM_EPS) * lng_ref[...] + lnb_ref[...]
        vr_ref[rs, :] = vn
        vb = vn.astype(BF16)
        tiles = []
        for j in range(A_GROUPS // 2):
            vj = vb[:, LANES * j:LANES * (j + 1)]
            tiles.append(jnp.where(lane < LANES // 2, _dot(wm[2 * j], vj), _dot(wm[2 * j + 1], vj)))
        mixed = jnp.concatenate(tiles, axis=1) + bias_ref[...]
        a_ref[rs, :] = _rms(_gelu(u_ref[rs, :]) * mixed, ga_ref[...])


def _gmlp(u, v, ln_g, ln_b, w_mix, bias_full, g_a, period, rows):
    m, d_a = u.shape
    row = pl.BlockSpec((rows, d_a), lambda i: (i, 0))
    full = lambda a: pl.BlockSpec(a.shape, lambda i: (0,) * a.ndim)
    return pl.pallas_call(
        functools.partial(_gmlp_body, period_log2=int(math.log2(period)), rows=rows),
        grid=(m // rows,),
        in_specs=[row, row, full(ln_g), full(ln_b), full(w_mix), full(bias_full), full(g_a)],
        out_specs=[row, row],
        out_shape=[jax.ShapeDtypeStruct((m, d_a), F32)] * 2,
        compiler_params=_cparams("parallel"),
        name="gmlp",
    )(u, v, ln_g, ln_b, w_mix, bias_full, g_a)


_CW_KEYS = ("w1k", "w1v", "pek", "pev", "b1k", "b1v", "w2k", "w2vt", "b2k", "b2vt")


def _compress_rows(xk_ref, xv_ref, w, kc_ref, vct_ref):
    n_sub = kc_ref.shape[1]
    hid_w = w["b1k"].shape[1]

    def hidden(x_ref, w1_ref, pe_ref, b1_ref):
        fs = jnp.zeros((n_sub, 2 * hid_w), F32)
        pc = jnp.zeros((2 * SUBLANES, 2 * hid_w), F32)
        for pp in range(CMP_STRIDE // 2):
            xp = jnp.concatenate([x_ref[pl.ds(2 * pp + j, n_sub, stride=CMP_STRIDE), :] for j in range(2)],
                                 axis=1).astype(BF16)
            fs = fs + _dot(xp, w1_ref[pp])
            pc = pc + _dot(pe_ref[pp], w1_ref[pp])
        const = pc[0:1, :hid_w] + pc[1:2, hid_w:] + b1_ref[...]
        hid = fs[:, :hid_w] + pltpu.roll(fs[:, hid_w:], n_sub - 1, 0) + const
        return _gelu(hid).astype(BF16)

    kc_ref[0] = (_dot(hidden(xk_ref, w["w1k"], w["pek"], w["b1k"]), w["w2k"][...]) + w["b2k"][...]).astype(BF16)
    vct_ref[0] = (_dot_nt(w["w2vt"][...], hidden(xv_ref, w["w1v"], w["pev"], w["b1v"]))
                  + w["b2vt"][...]).astype(BF16)


def _compress_prompt_body(xk_ref, xv_ref, *refs):
    w = dict(zip(_CW_KEYS, refs[:len(_CW_KEYS)]))
    _compress_rows(xk_ref, xv_ref, w, *refs[len(_CW_KEYS):])


def _compress_prompt(kc_rows, vc_rows, cw, n_batch, seq):
    n_sub = seq // CMP_STRIDE
    blk = pl.BlockSpec((seq, LANES), lambda i: (i, 0))
    full = lambda a: pl.BlockSpec(a.shape, lambda i: (0,) * a.ndim)
    ws = [cw[k] for k in _CW_KEYS]
    return pl.pallas_call(
        _compress_prompt_body,
        grid=(n_batch,),
        in_specs=[blk, blk] + [full(a) for a in ws],
        out_specs=[pl.BlockSpec((1, n_sub, LANES), lambda i: (i, 0, 0)),
                   pl.BlockSpec((1, LANES, n_sub), lambda i: (i, 0, 0))],
        out_shape=[jax.ShapeDtypeStruct((n_batch, n_sub, LANES), BF16),
                   jax.ShapeDtypeStruct((n_batch, LANES, n_sub), BF16)],
        compiler_params=_cparams("parallel"),
        name="compress_prompt",
    )(kc_rows, vc_rows, *ws)


def _page_copies(pt_ref, n, pools, bufs, sem, slot, n_pages, page):
    for a, (pool, buf) in enumerate(zip(pools, bufs)):
        for p in range(n_pages):
            yield pltpu.make_async_copy(pool.at[pt_ref[n, p]], buf.at[slot, :, pl.ds(p * page, page)],
                                        sem.at[a, slot])


def _fetch_pages(pt_ref, pools, bufs, sem, n_pages, page):
    n = pl.program_id(0)
    slot = n % 2
    copies = functools.partial(_page_copies, pt_ref, pools=pools, bufs=bufs, sem=sem, n_pages=n_pages, page=page)

    @pl.when(n == 0)
    def _():
        for c in copies(n=n, slot=slot):
            c.start()

    @pl.when(n + 1 < pl.num_programs(0))
    def _():
        for c in copies(n=n + 1, slot=1 - slot):
            c.start()

    for c in copies(n=n, slot=slot):
        c.wait()
    return slot


def _compress_sample_body(pt_ref, kpool_ref, vpool_ref, *refs, n_pages, page):
    nw = len(_CW_KEYS)
    w = dict(zip(_CW_KEYS, refs[:nw]))
    kc_ref, vct_ref, kbuf, vbuf, sem, xk_scr, xv_scr = refs[nw:]
    slot = _fetch_pages(pt_ref, (kpool_ref, vpool_ref), (kbuf, vbuf), sem, n_pages, page)
    for buf, x_scr in ((kbuf, xk_scr), (vbuf, xv_scr)):
        for p in range(n_pages):
            x_scr[page * p:page * (p + 1), :] = buf[slot, :, page * p:page * (p + 1)].T
    _compress_rows(xk_scr, xv_scr, w, kc_ref, vct_ref)


def _compress_sample(page_table, kpool, vpool, cw):
    n, n_pages = page_table.shape
    page = kpool.shape[2]
    past = n_pages * page
    n_sub = past // CMP_STRIDE
    any_spec = pl.BlockSpec(memory_space=pl.ANY)
    full = lambda a: pl.BlockSpec(a.shape, lambda i, pt: (0,) * a.ndim)
    ws = [cw[k] for k in _CW_KEYS]
    grid_spec = pltpu.PrefetchScalarGridSpec(
        num_scalar_prefetch=1,
        grid=(n,),
        in_specs=[any_spec, any_spec] + [full(a) for a in ws],
        out_specs=[pl.BlockSpec((1, n_sub, LANES), lambda i, pt: (i, 0, 0)),
                   pl.BlockSpec((1, LANES, n_sub), lambda i, pt: (i, 0, 0))],
        scratch_shapes=[pltpu.VMEM((2, LANES, past), F32), pltpu.VMEM((2, LANES, past), F32),
                        pltpu.SemaphoreType.DMA((2, 2)),
                        pltpu.VMEM((past, LANES), F32), pltpu.VMEM((past, LANES), F32)],
    )
    return pl.pallas_call(
        functools.partial(_compress_sample_body, n_pages=n_pages, page=page),
        grid_spec=grid_spec,
        out_shape=[jax.ShapeDtypeStruct((n, n_sub, LANES), BF16), jax.ShapeDtypeStruct((n, LANES, n_sub), BF16)],
        compiler_params=_cparams("arbitrary"),
        name="compress_sample",
    )(page_table, kpool, vpool, *ws)


_PS_PAD = 16


def _importance(ps_scr, n_sel_pad):
    imp = ps_scr[pl.ds(_PS_PAD - 1, n_sel_pad, stride=4), :]
    for j in range(1, 5):
        imp = imp + ps_scr[pl.ds(_PS_PAD - 1 + j, n_sel_pad, stride=4), :]
    return imp


def _select_blocks(imp, t, n_top):
    s_idx = lax.broadcasted_iota(jnp.int32, imp.shape, 0)
    cur = t >> int(math.log2(SEL_BLOCK))
    valid = s_idx * SEL_BLOCK <= t
    forced = (s_idx == 0) | (s_idx == cur) | (s_idx == cur - 1)
    score = jnp.where(valid, imp + jnp.where(forced, FORCE_BONUS, 0.0), NEG_INF)
    s_f = s_idx.astype(F32)
    sel = jnp.zeros(imp.shape, F32)
    for _ in range(n_top):
        mx = jnp.max(score, axis=0, keepdims=True)
        first = jnp.min(jnp.where(score == mx, s_f, 1e9), axis=0, keepdims=True)
        hit = s_f == first
        sel = jnp.where(hit & (mx > 0.5 * NEG_INF), 1.0, sel)
        score = jnp.where(hit, KNOCKED_OUT, score)
    return sel


def _cmpsel_prompt_body(qt_ref, kc_ref, vct_ref, band_ref, oc_ref, sel_ref, s_scr, ps_scr, o_scr, *, n_sub,
                        n_sel_pad, n_top):
    i = pl.program_id(1)
    t0 = i * LANES
    band_rows = band_ref.shape[1]
    base = pl.multiple_of(SUBLANES * i, SUBLANES)
    s_scr[:, 0:_PS_PAD, :] = jnp.zeros((NSA_HEADS, _PS_PAD, LANES), F32)
    ps_scr[...] = jnp.zeros(ps_scr.shape, F32)

    def attend(rows):
        c_idx = lax.broadcasted_iota(jnp.int32, (rows, LANES), 0)
        q_idx = lax.broadcasted_iota(jnp.int32, (rows, LANES), 1)
        valid = (t0 + q_idx - CMP_STRIDE * c_idx - (CMP_LEN - 1)) >= 0
        kc = kc_ref[0, 0:rows, :]
        for g in range(NSA_KV_HEADS):
            psum = jnp.zeros((rows, LANES), F32)
            for hh in range(HPG):
                h = HPG * g + hh
                s_scr[h, _PS_PAD:_PS_PAD + rows, :] = _dot(kc, qt_ref[LANES * h:LANES * (h + 1), :])
                s_scr[h, pl.ds(base, band_rows), :] = s_scr[h, pl.ds(base, band_rows), :] + band_ref[h]
                s = jnp.where(valid, s_scr[h, _PS_PAD:_PS_PAD + rows, :], NEG_INF)
                m = jnp.max(s, axis=0, keepdims=True)
                e = jnp.where(valid, jnp.exp(s - m), 0.0)
                l = jnp.sum(e, axis=0, keepdims=True)
                p = e * (1.0 / jnp.where(l > 0.0, l, 1.0))
                psum = psum + p
                o_scr[HEAD_DIM * h:HEAD_DIM * (h + 1), :] = _dot(
                    vct_ref[0, HEAD_DIM * g:HEAD_DIM * (g + 1), 0:rows], p.astype(BF16))
            ps_scr[g, _PS_PAD:_PS_PAD + rows, :] = psum

    blocks_per_case = LANES // SUBLANES
    for case in range(n_sub // LANES):
        @pl.when((i >= blocks_per_case * case) & (i < blocks_per_case * (case + 1)))
        def _():
            attend(LANES * (case + 1))

    t = t0 + lax.broadcasted_iota(jnp.int32, (n_sel_pad, LANES), 1)
    for g in range(NSA_KV_HEADS):
        sel_ref[0, g] = _select_blocks(_importance(ps_scr.at[g], n_sel_pad), t, n_top)
    oc_ref[...] = o_scr[...].T


def _cmpsel_prompt(qvt, kc, vct, band, n_batch, seq, n_sel_pad, n_top):
    n_sub = kc.shape[1]
    nblk = seq // LANES
    full = lambda a: pl.BlockSpec(a.shape, lambda n, i: (0,) * a.ndim)
    return pl.pallas_call(
        functools.partial(_cmpsel_prompt_body, n_sub=n_sub, n_sel_pad=n_sel_pad, n_top=n_top),
        grid=(n_batch, nblk),
        in_specs=[pl.BlockSpec((NSA_HEADS * LANES, LANES), lambda n, i: (0, n * nblk + i)),
                  pl.BlockSpec((1, n_sub, LANES), lambda n, i: (n, 0, 0)),
                  pl.BlockSpec((1, LANES, n_sub), lambda n, i: (n, 0, 0)),
                  full(band)],
        out_specs=[pl.BlockSpec((LANES, NSA_HEADS * HEAD_DIM), lambda n, i: (n * nblk + i, 0)),
                   pl.BlockSpec((1, NSA_KV_HEADS, n_sel_pad, LANES), lambda n, i: (n, 0, 0, i))],
        out_shape=[jax.ShapeDtypeStruct((n_batch * seq, NSA_HEADS * HEAD_DIM), F32),
                   jax.ShapeDtypeStruct((n_batch, NSA_KV_HEADS, n_sel_pad, seq), F32)],
        scratch_shapes=[pltpu.VMEM((NSA_HEADS, _PS_PAD + n_sub, LANES), F32),
                        pltpu.VMEM((NSA_KV_HEADS, _PS_PAD + 4 * n_sel_pad, LANES), F32),
                        pltpu.VMEM((NSA_HEADS * HEAD_DIM, LANES), F32)],
        compiler_params=_cparams("parallel", "parallel"),
        name="cmpsel_prompt",
    )(qvt, kc, vct, band)


def _cmpsel_sample_body(q_ref, kc_ref, vct_ref, bias_ref, rmat_ref, oct_ref, sel_ref, ps_scr, *, n_sub, n_sel_pad,
                        n_top, past, dec_seq):
    s = _dot_nt(kc_ref[0], q_ref[0]) + bias_ref[...]
    m = jnp.max(s, axis=0, keepdims=True)
    e = jnp.exp(s - m)
    p = e * (1.0 / jnp.sum(e, axis=0, keepdims=True))
    oct_ref[0] = _dot(vct_ref[0], p.astype(BF16))
    hi = p.astype(BF16)
    lo = (p - hi.astype(F32)).astype(BF16)
    psum = _dot(hi, rmat_ref[...]) + _dot(lo, rmat_ref[...])
    ps_scr[...] = jnp.zeros(ps_scr.shape, F32)
    ps_scr[_PS_PAD:_PS_PAD + n_sub, :] = psum
    col = lax.broadcasted_iota(jnp.int32, (n_sel_pad, LANES), 1)
    t = past + (col & (dec_seq - 1))
    sel = _select_blocks(_importance(ps_scr, n_sel_pad), t, n_top)
    sel_ref[0] = sel.T.astype(BF16)


def _cmpsel_sample(qs, kc, vct, bias_t, rmat, n_sel_pad, n_top, past, dec_seq):
    n, n_sub, _ = kc.shape
    full = lambda a: pl.BlockSpec(a.shape, lambda i: (0,) * a.ndim)
    return pl.pallas_call(
        functools.partial(_cmpsel_sample_body, n_sub=n_sub, n_sel_pad=n_sel_pad, n_top=n_top, past=past,
                          dec_seq=dec_seq),
        grid=(n,),
        in_specs=[pl.BlockSpec((1, LANES, LANES), lambda i: (i, 0, 0)),
                  pl.BlockSpec((1, n_sub, LANES), lambda i: (i, 0, 0)),
                  pl.BlockSpec((1, LANES, n_sub), lambda i: (i, 0, 0)),
                  full(bias_t), full(rmat)],
        out_specs=[pl.BlockSpec((1, LANES, LANES), lambda i: (i, 0, 0)),
                   pl.BlockSpec((1, LANES, n_sel_pad), lambda i: (i, 0, 0))],
        out_shape=[jax.ShapeDtypeStruct((n, LANES, LANES), F32), jax.ShapeDtypeStruct((n, LANES, n_sel_pad), BF16)],
        scratch_shapes=[pltpu.VMEM((_PS_PAD + 4 * n_sel_pad, LANES), F32)],
        compiler_params=_cparams("parallel"),
        name="cmpsel_sample",
    )(qs, kc, vct, bias_t, rmat)


def _flash_t_update(h, k, vt, qt, bias_ref, mask_rows, m_ref, l_ref, acc_ref):
    s = _dot(k, qt)
    m = m_ref[h]
    l = l_ref[h]
    acc = acc_ref[h]
    for c in range(k.shape[0] // SEL_BLOCK):
        rows = slice(SEL_BLOCK * c, SEL_BLOCK * (c + 1))
        sc = s[rows]
        if bias_ref is not None:
            sc = sc + bias_ref[rows, :]
        if mask_rows is not None:
            sc = jnp.where(mask_rows[c:c + 1, :] > 0.5, sc, NEG_INF)
        m_new = jnp.maximum(m, jnp.max(sc, axis=0, keepdims=True))
        alpha = jnp.exp(m - m_new)
        p = jnp.exp(sc - m_new)
        l = alpha * l + jnp.sum(p, axis=0, keepdims=True)
        acc = alpha * acc + _dot(vt[:, rows], p.astype(BF16))
        m = m_new
    m_ref[h] = m
    l_ref[h] = l
    acc_ref[h] = acc


def _flash_prompt_body(qt_ref, ks_ref, vst_ref, kw_ref, vwt_ref, sel_ref, near_ref, winfar_ref, osel_ref, owin_ref,
                       ms, ls, accs, mw, lw, accw, *, tq):
    qi = pl.program_id(1)
    for m_r, l_r, acc_r in ((ms, ls, accs), (mw, lw, accw)):
        m_r[...] = jnp.full(m_r.shape, M_INIT, F32)
        l_r[...] = jnp.zeros(l_r.shape, F32)
        acc_r[...] = jnp.zeros(acc_r.shape, F32)
    blocks = tq // SEL_BLOCK

    def sel_tile(kt, near):
        k0 = pl.multiple_of(kt * tq, tq)
        k = ks_ref[pl.ds(k0, tq), :]
        r0 = pl.multiple_of((kt // 2) * SUBLANES, SUBLANES)
        odd = (kt & 1) == 1
        for g in range(NSA_KV_HEADS):
            r8 = sel_ref[0, g, pl.ds(r0, SUBLANES), :]
            r4 = jnp.where(odd, r8[blocks:2 * blocks], r8[0:blocks])
            vt = vst_ref[HEAD_DIM * g:HEAD_DIM * (g + 1), pl.ds(k0, tq)]
            for hh in range(HPG):
                h = HPG * g + hh
                bias = None if near is None else near_ref.at[near, h]
                _flash_t_update(h, k, vt, qt_ref[LANES * h:LANES * (h + 1), :], bias, r4, ms, ls, accs)

    def win_tile(kt, near):
        k0 = pl.multiple_of(kt * tq, tq)
        k = kw_ref[pl.ds(k0, tq), :]
        for g in range(NSA_KV_HEADS):
            vt = vwt_ref[HEAD_DIM * g:HEAD_DIM * (g + 1), pl.ds(k0, tq)]
            for hh in range(HPG):
                h = HPG * g + hh
                bias = winfar_ref if near is None else near_ref.at[near, h]
                _flash_t_update(h, k, vt, qt_ref[LANES * h:LANES * (h + 1), :], bias, None, mw, lw, accw)

    def far_body(kt, carry):
        sel_tile(kt, None)
        return carry

    lax.fori_loop(0, qi - 1, far_body, 0)

    @pl.when(qi >= 1)
    def _():
        sel_tile(qi - 1, 0)
        win_tile(qi - 1, 0)

    sel_tile(qi, 1)

    @pl.when(qi >= 2)
    def _():
        win_tile(qi - 2, None)

    win_tile(qi, 1)

    for out_ref, l_r, acc_r in ((osel_ref, ls, accs), (owin_ref, lw, accw)):
        o_t = jnp.concatenate([acc_r[h] * (1.0 / l_r[h]) for h in range(NSA_HEADS)], axis=0)
        out_ref[...] = o_t.T


def _flash_prompt(kvb, qvt, sel_t, near, winfar, n_batch, seq, tq):
    nq = seq // tq
    assert WINDOW == 2 * tq and tq == 4 * SEL_BLOCK, "window = two key tiles; a key tile = 4 selection blocks"
    n_sel_pad = sel_t.shape[2]
    k_spec = lambda col: pl.BlockSpec((seq, LANES), lambda n, i: (n, col))
    vt_spec = lambda row: pl.BlockSpec((LANES, seq), lambda n, i: (row, n))
    full = lambda a: pl.BlockSpec(a.shape, lambda n, i: (0,) * a.ndim)
    out_spec = pl.BlockSpec((tq, NSA_HEADS * HEAD_DIM), lambda n, i: (n * nq + i, 0))
    state = [pltpu.VMEM((NSA_HEADS, 1, tq), F32), pltpu.VMEM((NSA_HEADS, 1, tq), F32),
             pltpu.VMEM((NSA_HEADS, HEAD_DIM, tq), F32)]
    out = jax.ShapeDtypeStruct((n_batch * seq, NSA_HEADS * HEAD_DIM), F32)
    return pl.pallas_call(
        functools.partial(_flash_prompt_body, tq=tq),
        grid=(n_batch, nq),
        in_specs=[pl.BlockSpec((NSA_HEADS * LANES, tq), lambda n, i: (0, n * nq + i)),
                  k_spec(2), vt_spec(NSA_HEADS), k_spec(4), vt_spec(NSA_HEADS + 1),
                  pl.BlockSpec((1, NSA_KV_HEADS, n_sel_pad, tq), lambda n, i: (n, 0, 0, i)),
                  full(near), full(winfar)],
        out_specs=[out_spec, out_spec],
        out_shape=[out, out],
        scratch_shapes=state + state,
        compiler_params=_cparams("parallel", "parallel"),
        name="flash_prompt",
    )(qvt, kvb, qvt, kvb, qvt, sel_t, near, winfar)


def _flash_sample_body(pt_ref, q_ref, kpool_ref, vpool_ref, ksn_ref, vsn_ref, sel_ref, e_ref, tailb_ref, newb_ref,
                       kwt_ref, vwt_ref, winb_ref, osel_ref, owin_ref, kbuf, vbuf, sem, s_scr, *, tk, n_pages, page):
    slot = _fetch_pages(pt_ref, (kpool_ref, vpool_ref), (kbuf, vbuf), sem, n_pages, page)
    q = q_ref[0]
    nt = n_pages * page // tk
    s_new = _dot_nt(q, ksn_ref[0]) + newb_ref[...]

    def score_tile(kt, tail):
        k0 = pl.multiple_of(kt * tk, tk)
        mask = _dot(sel_ref[0], e_ref[:, pl.ds(k0, tk)]) > 0.5
        s = _dot(q, kbuf[slot, :, pl.ds(k0, tk)].astype(BF16))
        if tail:
            s = s + tailb_ref[...]
        s = jnp.where(mask, s, NEG_INF)
        s_scr[:, pl.ds(k0, tk)] = s
        return jnp.max(s, axis=1, keepdims=True)

    m = lax.fori_loop(0, nt - 1, lambda kt, m: jnp.maximum(m, score_tile(kt, False)),
                      jnp.max(s_new, axis=1, keepdims=True), unroll=3)
    m = jnp.maximum(m, score_tile(nt - 1, True))

    def pv_tile(kt, carry):
        l, acc = carry
        k0 = pl.multiple_of(kt * tk, tk)
        p = jnp.exp(s_scr[:, pl.ds(k0, tk)] - m)
        return (l + jnp.sum(p, axis=1, keepdims=True),
                acc + _dot_nt(p.astype(BF16), vbuf[slot, :, pl.ds(k0, tk)].astype(BF16)))

    p_new = jnp.exp(s_new - m)
    l, acc = lax.fori_loop(0, nt, pv_tile, (jnp.sum(p_new, axis=1, keepdims=True),
                                            _dot(p_new.astype(BF16), vsn_ref[0])), unroll=4)
    osel_ref[0] = acc * (1.0 / l)

    s_w = _dot(q, kwt_ref[0].astype(BF16)) + winb_ref[...]
    s_new = _dot_nt(q, ksn_ref[1]) + newb_ref[...]
    m = jnp.maximum(jnp.max(s_w, axis=1, keepdims=True), jnp.max(s_new, axis=1, keepdims=True))
    p_w = jnp.exp(s_w - m)
    p_new = jnp.exp(s_new - m)
    l = jnp.sum(p_w, axis=1, keepdims=True) + jnp.sum(p_new, axis=1, keepdims=True)
    acc = _dot_nt(p_w.astype(BF16), vwt_ref[0].astype(BF16)) + _dot(p_new.astype(BF16), vsn_ref[1])
    owin_ref[0] = acc * (1.0 / l)


def _flash_sample(page_table, qs, kpool, vpool, kvn, selrows, e_all, tailb, newb, kw, vw, winb, tk):
    n, n_pages = page_table.shape
    page = kpool.shape[2]
    assert (n_pages * page // tk - 1) % 3 == 0 and (n_pages * page // tk) % 4 == 0, "unroll factors of the key loops"
    past = n_pages * page
    per_n = lambda a: pl.BlockSpec((1,) + a.shape[1:], lambda i, pt: (i,) + (0,) * (a.ndim - 1))
    full = lambda a: pl.BlockSpec(a.shape, lambda i, pt: (0,) * a.ndim)
    any_spec = pl.BlockSpec(memory_space=pl.ANY)
    new_spec = pl.BlockSpec((2, LANES, LANES), lambda i, pt: (i, 0, 0))
    out = jax.ShapeDtypeStruct((n, LANES, LANES), F32)
    grid_spec = pltpu.PrefetchScalarGridSpec(
        num_scalar_prefetch=1,
        grid=(n,),
        in_specs=[per_n(qs), any_spec, any_spec, new_spec, new_spec, per_n(selrows), full(e_all), full(tailb),
                  full(newb), per_n(kw), per_n(vw), full(winb)],
        out_specs=[per_n(out), per_n(out)],
        scratch_shapes=[pltpu.VMEM((2, LANES, past), F32), pltpu.VMEM((2, LANES, past), F32),
                        pltpu.SemaphoreType.DMA((2, 2)), pltpu.VMEM((LANES, past), F32)],
    )
    return pl.pallas_call(
        functools.partial(_flash_sample_body, tk=tk, n_pages=n_pages, page=page),
        grid_spec=grid_spec,
        out_shape=[out, out],
        compiler_params=_cparams("arbitrary"),
        name="flash_sample",
    )(page_table, qs, kpool, vpool, kvn[0], kvn[1], selrows, e_all, tailb, newb, kw, vw, winb)


def _mixout_body(an_ref, oc_ref, os_ref, ow_ref, gate_ref, x_ref, gb_ref, woa_ref, wob_ref, y_ref):
    gates = gate_ref[...]
    tm = gates.shape[0]
    lane = lax.broadcasted_iota(jnp.int32, (tm, LANES), 1)
    tiles = []
    for j in range(NSA_HEADS // 2):
        cols = slice(LANES * j, LANES * (j + 1))
        acc = None
        for br, o_ref in enumerate((oc_ref, os_ref, ow_ref)):
            c0 = 3 * (2 * j) + br
            c1 = 3 * (2 * j + 1) + br
            gcol = jnp.where(lane < HEAD_DIM, gates[:, c0:c0 + 1], gates[:, c1:c1 + 1])
            term = gcol * o_ref[:, cols]
            acc = term if acc is None else acc + term
        tiles.append(acc)
    bn = _rms(jnp.concatenate(tiles, axis=1), gb_ref[...]).astype(BF16)
    y_ref[...] = x_ref[...] + _dot(an_ref[...].astype(BF16), woa_ref[...]) + _dot(bn, wob_ref[...])


def _mixout(an, oc, osel, owin, gates, x, g_b, wo_a, wo_b, tm):
    m, d = x.shape
    row = lambda a: pl.BlockSpec((tm, a.shape[1]), lambda i: (i, 0))
    full = lambda a: pl.BlockSpec(a.shape, lambda i: (0,) * a.ndim)
    return pl.pallas_call(
        _mixout_body,
        grid=(m // tm,),
        in_specs=[row(an), row(oc), row(osel), row(owin), row(gates), row(x), full(g_b), full(wo_a), full(wo_b)],
        out_specs=row(x),
        out_shape=jax.ShapeDtypeStruct((m, d), F32),
        compiler_params=_cparams("parallel"),
        name="mix_out",
    )(an, oc, osel, owin, gates, x, g_b, wo_a, wo_b)


def _normmm_body(x_ref, g_ref, w_ref, o_ref):
    o_ref[...] = _dot(_rms(x_ref[...], g_ref[...]).astype(BF16), w_ref[...]).astype(o_ref.dtype)


def _norm_matmul(x, g, w, out_dtype, tm):
    m, d = x.shape
    n = w.shape[1]
    return pl.pallas_call(
        _normmm_body,
        grid=(m // tm,),
        in_specs=[pl.BlockSpec((tm, d), lambda i: (i, 0)), pl.BlockSpec(g.shape, lambda i: (0, 0)),
                  pl.BlockSpec(w.shape, lambda i: (0, 0))],
        out_specs=pl.BlockSpec((tm, n), lambda i: (i, 0)),
        out_shape=jax.ShapeDtypeStruct((m, n), out_dtype),
        compiler_params=_cparams("parallel"),
        name="norm_matmul",
    )(x, g, w)


def _mmres_body(a_ref, w_ref, r_ref, o_ref):
    o_ref[...] = r_ref[...] + _dot(a_ref[...].astype(BF16), w_ref[...])


def _matmul_residual(a, w, res, tm):
    m, k = a.shape
    n = w.shape[1]
    return pl.pallas_call(
        _mmres_body,
        grid=(m // tm,),
        in_specs=[pl.BlockSpec((tm, k), lambda i: (i, 0)), pl.BlockSpec(w.shape, lambda i: (0, 0)),
                  pl.BlockSpec((tm, n), lambda i: (i, 0))],
        out_specs=pl.BlockSpec((tm, n), lambda i: (i, 0)),
        out_shape=jax.ShapeDtypeStruct((m, n), F32),
        compiler_params=_cparams("parallel"),
        name="matmul_residual",
    )(a, w, res)


def _xattn_body(q_ref, mk_ref, mv_ref, o_ref, *, n_heads):
    hd = q_ref.shape[2] // n_heads
    scale = hd ** -0.5
    for h in range(n_heads):
        cols = slice(hd * h, hd * (h + 1))
        s = _dot_nt(q_ref[0, :, cols], mk_ref[0, :, cols].astype(BF16)) * scale
        e = jnp.exp(s - jnp.max(s, axis=1, keepdims=True))
        p = e * (1.0 / jnp.sum(e, axis=1, keepdims=True))
        o_ref[0, :, cols] = _dot(p.astype(BF16), mv_ref[0, :, cols].astype(BF16))


def _xattn_core(q, mk, mv, tm):
    n, t, d = q.shape
    return pl.pallas_call(
        functools.partial(_xattn_body, n_heads=XA_HEADS),
        grid=(n, t // tm),
        in_specs=[pl.BlockSpec((1, tm, d), lambda b, i: (b, i, 0)),
                  pl.BlockSpec((1,) + mk.shape[1:], lambda b, i: (b, 0, 0)),
                  pl.BlockSpec((1,) + mv.shape[1:], lambda b, i: (b, 0, 0))],
        out_specs=pl.BlockSpec((1, tm, d), lambda b, i: (b, i, 0)),
        out_shape=jax.ShapeDtypeStruct((n, t, d), F32),
        compiler_params=_cparams("parallel", "parallel"),
        name="xattn_core",
    )(q, mk, mv)


def _ffn_body(x_ref, g_ref, wu_ref, wg_ref, cw_ref, cb_ref, wd_ref, gf_ref, s1_ref, s2_ref, y_ref, a_ref, h_scr,
              acc_scr, carry_scr, *, tf, dec_seq):
    i = pl.program_id(1)
    j = pl.program_id(2)
    tm = x_ref.shape[0]

    @pl.when(j == 0)
    def _():
        h_scr[...] = _rms(x_ref[...], g_ref[...]).astype(BF16)
        acc_scr[...] = jnp.zeros(acc_scr.shape, F32)

    a = _dot(h_scr[...], wu_ref[...])
    gt = _dot(h_scr[...], wg_ref[...])
    row = lax.broadcasted_iota(jnp.int32, (tm, tf), 0)
    r1 = pltpu.roll(a, 1, 0)
    r2 = pltpu.roll(a, 2, 0)
    if dec_seq is None:
        cols = pl.ds(pl.multiple_of(j * tf, LANES), tf)

        @pl.when(i == 0)
        def _():
            carry_scr[:, cols] = jnp.zeros((SUBLANES, tf), F32)

        p0 = carry_scr[SUBLANES - 2:SUBLANES - 1, cols]
        p1 = carry_scr[SUBLANES - 1:SUBLANES, cols]
        a1 = jnp.where(row == 0, p1, r1)
        a2 = jnp.where(row == 0, p0, jnp.where(row == 1, p1, r2))
        carry_scr[:, cols] = a[tm - SUBLANES:tm, :]
        a_ref[0] = a[tm - SUBLANES:tm, :]
    else:
        t = row & (dec_seq - 1)
        a1 = jnp.where(t == 0, s1_ref[...], r1)
        a2 = jnp.where(t < 2, s2_ref[...], r2)
        a_ref[...] = a
    c = cb_ref[...] + a2 * cw_ref[0:1, :] + a1 * cw_ref[1:2, :] + a * cw_ref[2:3, :]
    acc_scr[...] += _dot((_gelu(c) * gt).astype(BF16), wd_ref[...])

    @pl.when(j == pl.num_programs(2) - 1)
    def _():
        y_ref[...] = _rms(x_ref[...] + acc_scr[...], gf_ref[...])


def _ffn(x, g, wu, wg, cw, cb, wd, g_final, s1, s2, n_seq, tm, tf, dec_seq):
    m, d = x.shape
    f = wu.shape[1]
    nt = m // n_seq // tm
    nf = f // tf
    row = pl.BlockSpec((tm, d), lambda n, i, j: (n * nt + i, 0))
    vec = lambda a: pl.BlockSpec(a.shape, lambda n, i, j: (0,) * a.ndim)
    fcol = lambda a: pl.BlockSpec((a.shape[0], tf), lambda n, i, j: (0, j))
    if dec_seq is None:
        st_spec = pl.BlockSpec((SUBLANES, tf), lambda n, i, j: (0, j))
        a_spec = pl.BlockSpec((1, SUBLANES, tf), lambda n, i, j: (n * nt + i, 0, j))
        a_shape = jax.ShapeDtypeStruct((n_seq * nt, SUBLANES, f), F32)
    else:
        st_spec = pl.BlockSpec((tm, tf), lambda n, i, j: (n * nt + i, j))
        a_spec = st_spec
        a_shape = jax.ShapeDtypeStruct((m, f), F32)
    return pl.pallas_call(
        functools.partial(_ffn_body, tf=tf, dec_seq=dec_seq),
        grid=(n_seq, nt, nf),
        in_specs=[row, vec(g), fcol(wu), fcol(wg), fcol(cw), fcol(cb),
                  pl.BlockSpec((tf, d), lambda n, i, j: (j, 0)), vec(g_final), st_spec, st_spec],
        out_specs=[row, a_spec],
        out_shape=[jax.ShapeDtypeStruct((m, d), F32), a_shape],
        scratch_shapes=[pltpu.VMEM((tm, d), BF16), pltpu.VMEM((tm, d), F32), pltpu.VMEM((SUBLANES, f), F32)],
        compiler_params=_cparams("parallel", "arbitrary", "arbitrary"),
        name="conv_ffn",
    )(x, g, wu, wg, cw, cb, wd, g_final, s1, s2)


def _t5_bucket_np(dist):
    n = np.maximum(dist, 0)
    max_exact = REL_BUCKETS // 2
    nf = np.maximum(n, 1).astype(np.float64)
    large = max_exact + (np.log(nf / max_exact) / math.log(REL_MAX_DIST / max_exact)
                         * (REL_BUCKETS - max_exact)).astype(np.int32)
    large = np.minimum(large, REL_BUCKETS - 1)
    return np.where(n < max_exact, n, large).astype(np.int32)


def _bias_vectors(rel_bias, dist, live, masked):
    dist, live, masked = np.broadcast_arrays(dist, live, masked)
    onehot = np.zeros(dist.shape + (REL_BUCKETS,), np.float32)
    np.put_along_axis(onehot, _t5_bucket_np(dist)[..., None], 1.0, axis=-1)
    onehot[..., REL_BUCKETS - 1] -= 1.0
    onehot *= live[..., None]
    add = np.where(masked, NEG_INF, 0.0).astype(np.float32)
    table = jnp.einsum("...b,bh->h...", jnp.asarray(onehot), rel_bias, precision=lax.Precision.HIGHEST)
    return table + add


def _toeplitz(v, rows, cols, stride):
    length = v.shape[-1]
    w = length - stride
    assert cols <= w
    t = jnp.tile(v, (1,) * (v.ndim - 1) + (rows,))[..., :rows * w]
    return t.reshape(v.shape[:-1] + (rows, w))[..., :cols]


def _bias_tables_prompt(rel_bias, tq):
    k = np.arange(2 * tq)
    upper = k < tq
    diag = _bias_vectors(rel_bias, k, upper, ~upper)
    prev = _bias_vectors(rel_bias, np.where(upper, k + tq, k - tq), True, False)
    near = jnp.stack([_toeplitz(prev, tq, tq, 1), _toeplitz(diag, tq, tq, 1)])
    winfar = _toeplitz(jnp.asarray(np.where(upper, NEG_INF, 0.0), F32), tq, tq, 1)
    length = 4 * LANES
    off = 2 * SUBLANES * CMP_STRIDE - (CMP_LEN - 1)
    kk = np.arange(length)
    d = np.where(kk < LANES, kk + off, kk - length + off)
    band = _toeplitz(_bias_vectors(rel_bias, d, (d >= 0) & ((kk < LANES) | (kk >= LANES + CMP_STRIDE)), False),
                     3 * SUBLANES, LANES, CMP_STRIDE)
    return near, winfar, band


def _bias_tables_sample(rel_bias, past, dec_seq, n_cmp, n_sub, tk, win_buf):
    t = np.arange(dec_seq)[:, None]

    def rows(dist, live, masked):
        tab = _bias_vectors(rel_bias, dist, live, masked)
        tab = tab.reshape(NSA_HEADS * dec_seq, -1)
        return jnp.pad(tab, ((0, LANES - NSA_HEADS * dec_seq), (0, 0)))

    c = np.arange(n_sub)[None, :]
    d_cmp = past + t - (c * CMP_STRIDE + CMP_LEN - 1)
    ok = (d_cmp >= 0) & (c < n_cmp)
    cmp_t = rows(d_cmp, ok, ~ok).T
    tail = rows(tk + t - np.arange(tk)[None, :], True, False)
    jn = np.arange(LANES)[None, :]
    ok = (jn <= t) & (jn < dec_seq)
    newb = rows(t - jn, ok, ~ok)
    d_win = win_buf + t - np.arange(win_buf)[None, :]
    ok = d_win < WINDOW
    winb = rows(d_win, ok, ~ok)
    return cmp_t, tail, newb, winb


def _row_heads(dec_seq):
    rows = np.arange(LANES)
    live = rows < NSA_HEADS * dec_seq
    return np.where(live, rows // dec_seq, 0), rows % dec_seq, live


def _prep_in_proj(w_in, d_a):
    d_b = NSA_HEADS * HEAD_DIM
    o_q = 2 * d_a
    o_kv = o_q + d_b
    o_g = o_kv + 6 * NSA_KV_HEADS * HEAD_DIM
    wq = w_in[:, o_q:o_kv].reshape(-1, NSA_HEADS, HEAD_DIM) * ATT_SCALE
    slot = jnp.zeros((w_in.shape[0], NSA_HEADS, NSA_KV_HEADS, HEAD_DIM), w_in.dtype)
    for h in range(NSA_HEADS):
        slot = slot.at[:, h, h // HPG].set(wq[:, h])
    wg = jnp.pad(w_in[:, o_g:], ((0, 0), (0, LANES - (w_in.shape[1] - o_g))))
    w_pad = jnp.concatenate([w_in[:, :o_q], slot.reshape(w_in.shape[0], -1), w_in[:, o_kv:o_g], wg], axis=1)
    kvw = NSA_KV_HEADS * HEAD_DIM
    w_vt = jnp.concatenate([slot.reshape(w_in.shape[0], -1), w_in[:, o_kv + 3 * kvw:o_kv + 4 * kvw],
                            w_in[:, o_kv + 5 * kvw:o_kv + 6 * kvw]], axis=1).T
    return w_pad.astype(BF16), w_vt.astype(BF16)


def _prep_compress(w1, b1, w2, b2, pe):
    eye = jnp.eye(NSA_KV_HEADS, dtype=w1.dtype)
    hid = w1.shape[-1]
    kvw = NSA_KV_HEADS * HEAD_DIM
    w1p = jnp.einsum("apdh,gk->pgdakh", w1.reshape(2, CMP_STRIDE, HEAD_DIM, hid), eye)
    w1p = w1p.reshape(CMP_STRIDE // 2, 2 * kvw, 2 * NSA_KV_HEADS * hid)
    pe_rows = jnp.broadcast_to(pe.reshape(2, CMP_STRIDE, 1, HEAD_DIM), (2, CMP_STRIDE, NSA_KV_HEADS, HEAD_DIM))
    pe_rows = pe_rows.reshape(2, CMP_STRIDE // 2, 2 * kvw).transpose(1, 0, 2)
    pe_rows = jnp.pad(pe_rows, ((0, 0), (0, 2 * SUBLANES - 2), (0, 0)))
    w2big = jnp.einsum("hd,gk->ghkd", w2, eye).reshape(NSA_KV_HEADS * hid, kvw)
    return (w1p.astype(BF16), pe_rows.astype(BF16), jnp.tile(b1, NSA_KV_HEADS)[None, :], w2big.astype(BF16),
            jnp.tile(b2, NSA_KV_HEADS)[None, :])


def _prep_compress_kv(p):
    w1k, pek, b1k, w2k, b2k = _prep_compress(p["w_cmp1_k"], p["b_cmp1_k"], p["w_cmp2_k"], p["b_cmp2_k"], p["pe_cmp_k"])
    w1v, pev, b1v, w2v, b2v = _prep_compress(p["w_cmp1_v"], p["b_cmp1_v"], p["w_cmp2_v"], p["b_cmp2_v"], p["pe_cmp_v"])
    return dict(w1k=w1k, w1v=w1v, pek=pek, pev=pev, b1k=b1k, b1v=b1v, w2k=w2k, w2vt=w2v.T, b2k=b2k, b2vt=b2v.T)


def _block_expand(n_sel_pad, n_keys):
    s = np.arange(n_sel_pad)[:, None]
    k = np.arange(n_keys)[None, :]
    return jnp.asarray((k // SEL_BLOCK == s), dtype=BF16)


def _round_up(x, m):
    return -(-x // m) * m


def _shared_front(x2d, p, w_in_pad, w_vt, period, mix_w, mix_b, tm):
    d_a = p["g_a"].shape[0]
    u, v, qpad, kc, vc, ks, vs, kw, vw, kvb, vt, gates = _in_proj(x2d, p["g_mix"][None, :], w_in_pad, w_vt, d_a, tm)
    an, v_rows = _gmlp(u, v, p["ln_v_g"][None, :], p["ln_v_b"][None, :], mix_w, mix_b, p["g_a"][None, :], period, tm)
    return an, v_rows, qpad, (kc, vc, ks, vs, kw, vw), kvb, vt, gates


def _shared_back(x2d, an, oc, osel, owin, gates, p, wts, mk, mv, n_seq, s1, s2, g_final, dec_seq, tm):
    d = x2d.shape[1]
    x1 = _mixout(an, oc, osel, owin, gates, x2d, p["g_b"][None, :], wts["wo_a"], wts["wo_b"], tm)
    qx = _norm_matmul(x1, p["g_xa"][None, :], wts["w_xq"], BF16, tm)
    t = x1.shape[0] // n_seq
    if dec_seq is None:
        ox = _xattn_core(qx.reshape(n_seq, t, d), mk, mv, tm).reshape(-1, d)
    else:
        qx3 = jnp.pad(qx.reshape(n_seq, t, d), ((0, 0), (0, SUBLANES - t), (0, 0)))
        ox = _xattn_core(qx3, mk, mv, SUBLANES)[:, :t].reshape(-1, d)
    x2 = _matmul_residual(ox, wts["w_xo"], x1, tm)
    n_ffn_seq = n_seq if dec_seq is None else 1
    return _ffn(x2, p["g_ffn"][None, :], wts["w_up"], wts["w_gate"], wts["conv_w"], p["conv_b"][None, :],
                wts["w_down"], g_final[None, :], s1, s2, n_ffn_seq, tm, wts["tf"], dec_seq)


def _heads_to_rows(qpad, n, t):
    q = qpad.reshape(n, t, NSA_HEADS, LANES).transpose(0, 2, 1, 3).reshape(n, NSA_HEADS * t, LANES)
    return jnp.pad(q, ((0, 0), (0, LANES - NSA_HEADS * t), (0, 0)))


def _rows_to_tokens(o, n, t):
    o = o[:, :NSA_HEADS * t].reshape(n, NSA_KV_HEADS, HPG, t, NSA_KV_HEADS, HEAD_DIM)
    o = jnp.stack([o[:, g, :, :, g] for g in range(NSA_KV_HEADS)], axis=1)
    return o.transpose(0, 3, 1, 2, 4).reshape(n * t, NSA_HEADS * HEAD_DIM)


def kernel(x_prompt, x_sample, mem_prompt, cache_cmp_k, cache_cmp_v, cache_sel_k, cache_sel_v, cache_win_k,
           cache_win_v, cache_mem_k, cache_mem_v, state_conv, page_table, g_mix, w_in, w_s, b_s, ln_v_g, ln_v_b,
           w_cmp1_k, b_cmp1_k, w_cmp2_k, b_cmp2_k, pe_cmp_k, w_cmp1_v, b_cmp1_v, w_cmp2_v, b_cmp2_v, pe_cmp_v,
           rel_bias, g_a, g_b, w_o, g_xa, g_mem, w_xq, w_mk, w_mv, w_xo, g_ffn, w_up, w_gate, conv_w, conv_b,
           w_down, g_final):
    depth = w_in.shape[0]
    assert depth == 1, "the layer loop is written for a single layer"
    bsz, seq, d = x_prompt.shape
    nd, dec_seq, _ = x_sample.shape
    mem_len = mem_prompt.shape[1]
    d_a = g_a.shape[1]
    d_ff = w_up.shape[2]
    n_pages = page_table.shape[1]
    page = cache_cmp_k.shape[2]
    past = n_pages * page
    kvw = NSA_KV_HEADS * HEAD_DIM
    win_buf = cache_win_k.shape[2]
    assert kvw == LANES and seq % 256 == 0 and past % 512 == 0 and dec_seq == 4 and nd * dec_seq % LANES == 0
    tm = 512
    tq = 256
    tk_s = 512
    tf = d_ff // 2 if (d_ff // 2) % LANES == 0 else d_ff

    l = 0
    p = dict(g_mix=g_mix[l], ln_v_g=ln_v_g[l], ln_v_b=ln_v_b[l], w_cmp1_k=w_cmp1_k[l], b_cmp1_k=b_cmp1_k[l],
             w_cmp2_k=w_cmp2_k[l], b_cmp2_k=b_cmp2_k[l], pe_cmp_k=pe_cmp_k[l], w_cmp1_v=w_cmp1_v[l],
             b_cmp1_v=b_cmp1_v[l], w_cmp2_v=w_cmp2_v[l], b_cmp2_v=b_cmp2_v[l], pe_cmp_v=pe_cmp_v[l], g_a=g_a[l],
             g_b=g_b[l], g_xa=g_xa[l], g_ffn=g_ffn[l], conv_b=conv_b[l])
    w_in_pad, w_vt = _prep_in_proj(w_in[l], d_a)
    cw = _prep_compress_kv(p)
    wts = dict(wo_a=w_o[l, :d_a].astype(BF16), wo_b=w_o[l, d_a:].astype(BF16), w_xq=w_xq[l].astype(BF16),
               w_xo=w_xo[l].astype(BF16), w_up=w_up[l].astype(BF16), w_gate=w_gate[l].astype(BF16),
               w_down=w_down[l].astype(BF16), conv_w=jnp.pad(conv_w[l], ((0, SUBLANES - CONV_W), (0, 0))), tf=tf)
    bias_lanes = jnp.repeat(b_s[l].T, d_a // A_GROUPS, axis=1)

    xp = x_prompt.reshape(bsz * seq, d)
    an, _, _, kv6, kvb, qvt, gates = _shared_front(xp, p, w_in_pad, w_vt, CHUNK, w_s[l], bias_lanes, tm)
    kc, vct = _compress_prompt(kv6[0], kv6[1], cw, bsz, seq)
    n_sel = seq // SEL_BLOCK
    n_sel_pad = _round_up(n_sel, LANES)
    near, winfar, band = _bias_tables_prompt(rel_bias, tq)
    oc, sel_t = _cmpsel_prompt(qvt, kc, vct, band, bsz, seq, n_sel_pad, min(SEL_TOP, n_sel))
    osel, owin = _flash_prompt(kvb, qvt, sel_t, near, winfar, bsz, seq, tq)
    mem_kv = _norm_matmul(mem_prompt.reshape(bsz * mem_len, d), g_mem[l][None, :],
                          jnp.concatenate([w_mk[l], w_mv[l]], axis=1).astype(BF16), F32, min(tm, bsz * mem_len))
    mk_p = mem_kv[:, :d].reshape(bsz, mem_len, d)
    mv_p = mem_kv[:, d:].reshape(bsz, mem_len, d)
    zero_state = jnp.zeros((SUBLANES, d_ff), F32)
    yp, a_tail = _shared_back(xp, an, oc, osel, owin, gates, p, wts, mk_p, mv_p, bsz, zero_state, zero_state,
                              g_final, None, tm)
    keep = min(WINDOW, seq)
    shp = lambda a: a.reshape(1, bsz, seq, NSA_KV_HEADS, HEAD_DIM)
    prompt_state = (shp(kv6[0]), shp(kv6[1]), shp(kv6[2]), shp(kv6[3]), shp(kv6[4])[:, :, -keep:],
                    shp(kv6[5])[:, :, -keep:], mk_p.reshape(1, bsz, mem_len, XA_HEADS, d // XA_HEADS),
                    mv_p.reshape(1, bsz, mem_len, XA_HEADS, d // XA_HEADS),
                    a_tail.reshape(bsz, -1, SUBLANES, d_ff)[None, :, -1, SUBLANES - (CONV_W - 1):])

    xs = x_sample.reshape(nd * dec_seq, d)
    reps = CHUNK // dec_seq
    mix_w_s = jnp.tile(w_s[l][:, :dec_seq, :dec_seq], (1, reps, reps))
    bias_s = jnp.repeat(jnp.tile(b_s[l][:, :dec_seq].T, (reps, 1)), d_a // A_GROUPS, axis=1)
    tms = min(tm, nd * dec_seq)
    an_s, v_rows, qpad_s, kv6s, kvb_s, _, gates_s = _shared_front(xs, p, w_in_pad, w_vt, dec_seq, mix_w_s, bias_s,
                                                                 tms)
    pools = [c[l].transpose(0, 2, 3, 1).reshape(c.shape[1], kvw, page)
             for c in (cache_cmp_k, cache_cmp_v, cache_sel_k, cache_sel_v)]
    kc_s, vct_s = _compress_sample(page_table, pools[0], pools[1], cw)
    n_sub_s = past // CMP_STRIDE
    n_cmp_s = (past + dec_seq) // CMP_STRIDE - 1
    n_sel_s = -(-(past + dec_seq) // SEL_BLOCK)
    n_sel_pad_s = _round_up(n_sel_s, LANES)
    cmp_t, tail, newb, winb = _bias_tables_sample(rel_bias, past, dec_seq, n_cmp_s, n_sub_s, tk_s, win_buf)
    h_of, t_of, live = _row_heads(dec_seq)
    same = (live[:, None] & live[None, :] & ((h_of[:, None] // HPG) == (h_of[None, :] // HPG))
            & (t_of[:, None] == t_of[None, :]))
    qs = _heads_to_rows(qpad_s, nd, dec_seq)
    oct_s, selrows = _cmpsel_sample(qs, kc_s, vct_s, cmp_t, jnp.asarray(same, dtype=BF16), n_sel_pad_s,
                                    min(SEL_TOP, n_sel_s), past, dec_seq)
    newrows = lambda a: jnp.pad(a.reshape(nd, dec_seq, LANES), ((0, 0), (0, LANES - dec_seq), (0, 0)))
    kvn = (jnp.stack([newrows(kvb_s[:, 2 * LANES:3 * LANES]), newrows(kvb_s[:, 4 * LANES:5 * LANES])], axis=1)
           .reshape(2 * nd, LANES, LANES),
           jnp.stack([newrows(kvb_s[:, 3 * LANES:4 * LANES]), newrows(kvb_s[:, 5 * LANES:6 * LANES])], axis=1)
           .reshape(2 * nd, LANES, LANES))
    wk = cache_win_k[l].transpose(0, 2, 3, 1).reshape(nd, kvw, win_buf)
    wv = cache_win_v[l].transpose(0, 2, 3, 1).reshape(nd, kvw, win_buf)
    osel_r, owin_r = _flash_sample(page_table, qs, pools[2], pools[3], kvn, selrows,
                                   _block_expand(n_sel_pad_s, past), tail, newb, wk, wv, winb, tk_s)
    oc_s = _rows_to_tokens(oct_s.transpose(0, 2, 1), nd, dec_seq)
    osel_s = _rows_to_tokens(osel_r, nd, dec_seq)
    owin_s = _rows_to_tokens(owin_r, nd, dec_seq)
    st = state_conv[l]
    zrow = jnp.zeros_like(st[:, :1])
    s1 = jnp.concatenate([st[:, 1:2]] + [zrow] * (dec_seq - 1), axis=1).reshape(nd * dec_seq, d_ff)
    s2 = jnp.concatenate([st[:, 0:1], st[:, 1:2]] + [zrow] * (dec_seq - 2), axis=1).reshape(nd * dec_seq, d_ff)
    mk_s = cache_mem_k[l].reshape(nd, mem_len, d)
    mv_s = cache_mem_v[l].reshape(nd, mem_len, d)
    ys, a_full = _shared_back(xs, an_s, oc_s, osel_s, owin_s, gates_s, p, wts, mk_s, mv_s, nd, s1, s2, g_final,
                              dec_seq, tms)
    keep_s = min(WINDOW, past + dec_seq)
    shs = lambda a: a.reshape(1, nd, dec_seq, NSA_KV_HEADS, HEAD_DIM)
    win_new = lambda cache, new: jnp.concatenate([cache[l], shs(new)[0]], axis=1)[None, :, -keep_s:]
    sample_state = (shs(kv6s[0]), shs(kv6s[1]), shs(kv6s[2]), shs(kv6s[3]), win_new(cache_win_k, kv6s[4]),
                    win_new(cache_win_v, kv6s[5]), v_rows.reshape(1, nd, dec_seq, d_a),
                    a_full.reshape(1, nd, dec_seq, d_ff)[:, :, dec_seq - (CONV_W - 1):])

    return (yp.reshape(bsz, seq, d), ys.reshape(nd, dec_seq, d)) + prompt_state + sample_state
```

```python
import functools
import math

import numpy as np
import jax
import jax.numpy as jnp
from jax import lax
from jax.experimental import pallas as pl
from jax.experimental.pallas import tpu as pltpu

F32 = jnp.float32
BF16 = jnp.bfloat16

LANES = 128
SUBLANES = 8
VMEM_LIMIT_BYTES = 56 * 1024 * 1024

NORM_EPS = 1e-6
NEG_INF = -1e30
M_INIT = -1e29
KNOCKED_OUT = -3e38
FORCE_BONUS = 1e4

A_GROUPS = 8
CHUNK = 128
NSA_HEADS = 8
HEAD_DIM = 64
NSA_KV_HEADS = 2
HPG = NSA_HEADS // NSA_KV_HEADS
CMP_LEN = 32
CMP_STRIDE = 16
SEL_BLOCK = 64
SEL_TOP = 16
WINDOW = 512
REL_BUCKETS = 32
REL_MAX_DIST = 128
XA_HEADS = 4
CONV_W = 3
ATT_SCALE = HEAD_DIM ** -0.5
LOG2E = math.log2(math.e)

_NT = (((1,), (1,)), ((), ()))


def _cparams(*sem):
    return pltpu.CompilerParams(dimension_semantics=sem, vmem_limit_bytes=VMEM_LIMIT_BYTES)


def _gelu(x):
    return 0.5 * x * (1.0 + jnp.tanh(0.7978845608028654 * (x + 0.044715 * (x * x * x))))


def _rms(x, g):
    return x * lax.rsqrt(jnp.mean(x * x, axis=-1, keepdims=True) + NORM_EPS) * g


def _dot(a, b):
    return jnp.dot(a, b, preferred_element_type=F32)


def _dot_nt(a, b):
    return lax.dot_general(a, b, _NT, preferred_element_type=F32)


def _inproj_body(x_ref, g_ref, w_ref, wvt_ref, u_ref, v_ref, q_ref, kc_ref, vc_ref, ks_ref, vs_ref, kw_ref,
                 vw_ref, kvb_ref, vt_ref, gate_ref, *, d_a):
    h = _rms(x_ref[...], g_ref[...]).astype(BF16)

    def mm(lo, hi):
        return _dot(h, w_ref[:, lo:hi])

    u_ref[...] = mm(0, d_a)
    v_ref[...] = mm(d_a, 2 * d_a)
    o = 2 * d_a
    q_ref[...] = mm(o, o + NSA_HEADS * LANES).astype(BF16)
    o += NSA_HEADS * LANES
    for i, r in enumerate((kc_ref, vc_ref, ks_ref, vs_ref, kw_ref, vw_ref)):
        z = mm(o + LANES * i, o + LANES * (i + 1))
        r[...] = z
        kvb_ref[:, LANES * i:LANES * (i + 1)] = z.astype(BF16)
    o += 6 * LANES
    gate_ref[...] = 1.0 / (1.0 + jnp.exp(-mm(o, o + LANES)))
    vt_ref[...] = _dot_nt(wvt_ref[...], h).astype(BF16)


def _in_proj(x, g, w_pad, w_vt, d_a, tm):
    m, d = x.shape
    row = lambda n: pl.BlockSpec((tm, n), lambda i: (i, 0))
    full = lambda a: pl.BlockSpec(a.shape, lambda i: (0,) * a.ndim)
    kv = jax.ShapeDtypeStruct((m, LANES), F32)
    return pl.pallas_call(
        functools.partial(_inproj_body, d_a=d_a),
        grid=(m // tm,),
        in_specs=[row(d), full(g), full(w_pad), full(w_vt)],
        out_specs=[row(d_a), row(d_a), row(NSA_HEADS * LANES)] + [row(LANES)] * 6
                  + [row(6 * LANES), pl.BlockSpec((w_vt.shape[0], tm), lambda i: (0, i)), row(LANES)],
        out_shape=[jax.ShapeDtypeStruct((m, d_a), F32), jax.ShapeDtypeStruct((m, d_a), F32),
                   jax.ShapeDtypeStruct((m, NSA_HEADS * LANES), BF16)] + [kv] * 6
                  + [jax.ShapeDtypeStruct((m, 6 * LANES), BF16), jax.ShapeDtypeStruct((w_vt.shape[0], m), BF16),
                     jax.ShapeDtypeStruct((m, LANES), F32)],
        compiler_params=_cparams("parallel"),
        name="in_proj",
    )(x, g, w_pad, w_vt)


def _gmlp_body(u_ref, v_ref, lng_ref, lnb_ref, w_ref, bias_ref, ga_ref, a_ref, vr_ref, *, period_log2, rows):
    ri = lax.broadcasted_iota(jnp.int32, (CHUNK, CHUNK), 0)
    ci = lax.broadcasted_iota(jnp.int32, (CHUNK, CHUNK), 1)
    mask = (ci <= ri) & ((ri >> period_log2) == (ci >> period_log2))
    wm = [jnp.where(mask, w_ref[g], 0.0).astype(BF16) for g in range(A_GROUPS)]
    lane = lax.broadcasted_iota(jnp.int32, (CHUNK, LANES), 1)
    for c in range(rows // CHUNK):
        rs = slice(CHUNK * c, CHUNK * (c + 1))
        gv = _gelu(v_ref[rs, :])
        mu = jnp.mean(gv, axis=-1, keepdims=True)
        var = jnp.mean(jnp.square(gv - mu), axis=-1, keepdims=True)
        vn = (gv - mu) * lax.rsqrt(var + NORM_EPS) * lng_ref[...] + lnb_ref[...]
        vr_ref[rs, :] = vn
        vb = vn.astype(BF16)
        tiles = []
        for j in range(A_GROUPS // 2):
            vj = vb[:, LANES * j:LANES * (j + 1)]
            tiles.append(jnp.where(lane < LANES // 2, _dot(wm[2 * j], vj), _dot(wm[2 * j + 1], vj)))
        mixed = jnp.concatenate(tiles, axis=1) + bias_ref[...]
        a_ref[rs, :] = _rms(_gelu(u_ref[rs, :]) * mixed, ga_ref[...])


def _gmlp(u, v, ln_g, ln_b, w_mix, bias_full, g_a, period, rows):
    m, d_a = u.shape
    row = pl.BlockSpec((rows, d_a), lambda i: (i, 0))
    full = lambda a: pl.BlockSpec(a.shape, lambda i: (0,) * a.ndim)
    return pl.pallas_call(
        functools.partial(_gmlp_body, period_log2=int(math.log2(period)), rows=rows),
        grid=(m // rows,),
        in_specs=[row, row, full(ln_g), full(ln_b), full(w_mix), full(bias_full), full(g_a)],
        out_specs=[row, row],
        out_shape=[jax.ShapeDtypeStruct((m, d_a), F32)] * 2,
        compiler_params=_cparams("parallel"),
        name="gmlp",
    )(u, v, ln_g, ln_b, w_mix, bias_full, g_a)


_CW_KEYS = ("w1k", "w1v", "pek", "pev", "b1k", "b1v", "w2k", "w2vt", "b2k", "b2vt")


def _compress_rows(xk_ref, xv_ref, w, kc_ref, vct_ref):
    n_sub = kc_ref.shape[1]
    hid_w = w["b1k"].shape[1]

    def hidden(x_ref, w1_ref, pe_ref, b1_ref):
        fs = jnp.zeros((n_sub, 2 * hid_w), F32)
        pc = jnp.zeros((2 * SUBLANES, 2 * hid_w), F32)
        for pp in range(CMP_STRIDE // 2):
            xp = jnp.concatenate([x_ref[pl.ds(2 * pp + j, n_sub, stride=CMP_STRIDE), :] for j in range(2)],
                                 axis=1).astype(BF16)
            fs = fs + _dot(xp, w1_ref[pp])
            pc = pc + _dot(pe_ref[pp], w1_ref[pp])
        const = pc[0:1, :hid_w] + pc[1:2, hid_w:] + b1_ref[...]
        hid = fs[:, :hid_w] + pltpu.roll(fs[:, hid_w:], n_sub - 1, 0) + const
        return _gelu(hid).astype(BF16)

    kc_ref[0] = (_dot(hidden(xk_ref, w["w1k"], w["pek"], w["b1k"]), w["w2k"][...]) + w["b2k"][...]).astype(BF16)
    vct_ref[0] = (_dot_nt(w["w2vt"][...], hidden(xv_ref, w["w1v"], w["pev"], w["b1v"]))
                  + w["b2vt"][...]).astype(BF16)


def _compress_prompt_body(xk_ref, xv_ref, *refs):
    w = dict(zip(_CW_KEYS, refs[:len(_CW_KEYS)]))
    _compress_rows(xk_ref, xv_ref, w, *refs[len(_CW_KEYS):])


def _compress_prompt(kc_rows, vc_rows, cw, n_batch, seq):
    n_sub = seq // CMP_STRIDE
    blk = pl.BlockSpec((seq, LANES), lambda i: (i, 0))
    full = lambda a: pl.BlockSpec(a.shape, lambda i: (0,) * a.ndim)
    ws = [cw[k] for k in _CW_KEYS]
    return pl.pallas_call(
        _compress_prompt_body,
        grid=(n_batch,),
        in_specs=[blk, blk] + [full(a) for a in ws],
        out_specs=[pl.BlockSpec((1, n_sub, LANES), lambda i: (i, 0, 0)),
                   pl.BlockSpec((1, LANES, n_sub), lambda i: (i, 0, 0))],
        out_shape=[jax.ShapeDtypeStruct((n_batch, n_sub, LANES), BF16),
                   jax.ShapeDtypeStruct((n_batch, LANES, n_sub), BF16)],
        compiler_params=_cparams("parallel"),
        name="compress_prompt",
    )(kc_rows, vc_rows, *ws)


def _page_copies(pt_ref, n, pools, bufs, sem, slot, n_pages, page):
    for a, (pool, buf) in enumerate(zip(pools, bufs)):
        for p in range(n_pages):
            yield pltpu.make_async_copy(pool.at[pt_ref[n, p]], buf.at[slot, :, pl.ds(p * page, page)],
                                        sem.at[a, slot])


def _fetch_pages(pt_ref, pools, bufs, sem, n_pages, page):
    n = pl.program_id(0)
    slot = n % 2
    copies = functools.partial(_page_copies, pt_ref, pools=pools, bufs=bufs, sem=sem, n_pages=n_pages, page=page)

    @pl.when(n == 0)
    def _():
        for c in copies(n=n, slot=slot):
            c.start()

    @pl.when(n + 1 < pl.num_programs(0))
    def _():
        for c in copies(n=n + 1, slot=1 - slot):
            c.start()

    for c in copies(n=n, slot=slot):
        c.wait()
    return slot


def _compress_sample_body(pt_ref, kpool_ref, vpool_ref, *refs, n_pages, page):
    nw = len(_CW_KEYS)
    w = dict(zip(_CW_KEYS, refs[:nw]))
    kc_ref, vct_ref, kbuf, vbuf, sem, xk_scr, xv_scr = refs[nw:]
    slot = _fetch_pages(pt_ref, (kpool_ref, vpool_ref), (kbuf, vbuf), sem, n_pages, page)
    for buf, x_scr in ((kbuf, xk_scr), (vbuf, xv_scr)):
        for p in range(n_pages):
            x_scr[page * p:page * (p + 1), :] = buf[slot, :, page * p:page * (p + 1)].T
    _compress_rows(xk_scr, xv_scr, w, kc_ref, vct_ref)


def _compress_sample(page_table, kpool, vpool, cw):
    n, n_pages = page_table.shape
    page = kpool.shape[2]
    past = n_pages * page
    n_sub = past // CMP_STRIDE
    any_spec = pl.BlockSpec(memory_space=pl.ANY)
    full = lambda a: pl.BlockSpec(a.shape, lambda i, pt: (0,) * a.ndim)
    ws = [cw[k] for k in _CW_KEYS]
    grid_spec = pltpu.PrefetchScalarGridSpec(
        num_scalar_prefetch=1,
        grid=(n,),
        in_specs=[any_spec, any_spec] + [full(a) for a in ws],
        out_specs=[pl.BlockSpec((1, n_sub, LANES), lambda i, pt: (i, 0, 0)),
                   pl.BlockSpec((1, LANES, n_sub), lambda i, pt: (i, 0, 0))],
        scratch_shapes=[pltpu.VMEM((2, LANES, past), F32), pltpu.VMEM((2, LANES, past), F32),
                        pltpu.SemaphoreType.DMA((2, 2)),
                        pltpu.VMEM((past, LANES), F32), pltpu.VMEM((past, LANES), F32)],
    )
    return pl.pallas_call(
        functools.partial(_compress_sample_body, n_pages=n_pages, page=page),
        grid_spec=grid_spec,
        out_shape=[jax.ShapeDtypeStruct((n, n_sub, LANES), BF16), jax.ShapeDtypeStruct((n, LANES, n_sub), BF16)],
        compiler_params=_cparams("arbitrary"),
        name="compress_sample",
    )(page_table, kpool, vpool, *ws)


_PS_PAD = 16


def _importance(ps_scr, n_sel_pad):
    imp = ps_scr[pl.ds(_PS_PAD - 1, n_sel_pad, stride=4), :]
    for j in range(1, 5):
        imp = imp + ps_scr[pl.ds(_PS_PAD - 1 + j, n_sel_pad, stride=4), :]
    return imp


def _select_blocks(imp, t, n_top):
    s_idx = lax.broadcasted_iota(jnp.int32, imp.shape, 0)
    cur = t >> int(math.log2(SEL_BLOCK))
    valid = s_idx * SEL_BLOCK <= t
    forced = (s_idx == 0) | (s_idx == cur) | (s_idx == cur - 1)
    score = jnp.where(valid, imp + jnp.where(forced, FORCE_BONUS, 0.0), NEG_INF)
    s_f = s_idx.astype(F32)
    sel = jnp.zeros(imp.shape, F32)
    for _ in range(n_top):
        mx = jnp.max(score, axis=0, keepdims=True)
        first = jnp.min(jnp.where(score == mx, s_f, 1e9), axis=0, keepdims=True)
        hit = s_f == first
        sel = jnp.where(hit & (mx > 0.5 * NEG_INF), 1.0, sel)
        score = jnp.where(hit, KNOCKED_OUT, score)
    return sel


def _cmpsel_prompt_body(qt_ref, kc_ref, vct_ref, band_ref, oc_ref, sel_ref, s_scr, ps_scr, o_scr, *, n_sub,
                        n_sel_pad, n_top):
    i = pl.program_id(1)
    t0 = i * LANES
    band_rows = band_ref.shape[1]
    base = pl.multiple_of(SUBLANES * i, SUBLANES)
    s_scr[:, 0:_PS_PAD, :] = jnp.zeros((NSA_HEADS, _PS_PAD, LANES), F32)
    ps_scr[...] = jnp.zeros(ps_scr.shape, F32)

    def attend(rows):
        c_idx = lax.broadcasted_iota(jnp.int32, (rows, LANES), 0)
        q_idx = lax.broadcasted_iota(jnp.int32, (rows, LANES), 1)
        valid = (t0 + q_idx - CMP_STRIDE * c_idx - (CMP_LEN - 1)) >= 0
        kc = kc_ref[0, 0:rows, :]
        for g in range(NSA_KV_HEADS):
            psum = jnp.zeros((rows, LANES), F32)
            for hh in range(HPG):
                h = HPG * g + hh
                s_scr[h, _PS_PAD:_PS_PAD + rows, :] = _dot(kc, qt_ref[LANES * h:LANES * (h + 1), :])
                s_scr[h, pl.ds(base, band_rows), :] = s_scr[h, pl.ds(base, band_rows), :] + band_ref[h]
                s = jnp.where(valid, s_scr[h, _PS_PAD:_PS_PAD + rows, :], NEG_INF)
                m = jnp.max(s, axis=0, keepdims=True)
                e = jnp.where(valid, jnp.exp2(s - m), 0.0)
                l = jnp.sum(e, axis=0, keepdims=True)
                p = e * (1.0 / jnp.where(l > 0.0, l, 1.0))
                psum = psum + p
                o_scr[HEAD_DIM * h:HEAD_DIM * (h + 1), :] = _dot(
                    vct_ref[0, HEAD_DIM * g:HEAD_DIM * (g + 1), 0:rows], p.astype(BF16))
            ps_scr[g, _PS_PAD:_PS_PAD + rows, :] = psum

    blocks_per_case = LANES // SUBLANES
    for case in range(n_sub // LANES):
        @pl.when((i >= blocks_per_case * case) & (i < blocks_per_case * (case + 1)))
        def _():
            attend(LANES * (case + 1))

    t = t0 + lax.broadcasted_iota(jnp.int32, (n_sel_pad, LANES), 1)
    for g in range(NSA_KV_HEADS):
        sel_ref[0, g] = _select_blocks(_importance(ps_scr.at[g], n_sel_pad), t, n_top)
    oc_ref[...] = o_scr[...].T


def _cmpsel_prompt(qvt, kc, vct, band, n_batch, seq, n_sel_pad, n_top):
    n_sub = kc.shape[1]
    nblk = seq // LANES
    full = lambda a: pl.BlockSpec(a.shape, lambda n, i: (0,) * a.ndim)
    return pl.pallas_call(
        functools.partial(_cmpsel_prompt_body, n_sub=n_sub, n_sel_pad=n_sel_pad, n_top=n_top),
        grid=(n_batch, nblk),
        in_specs=[pl.BlockSpec((NSA_HEADS * LANES, LANES), lambda n, i: (0, n * nblk + i)),
                  pl.BlockSpec((1, n_sub, LANES), lambda n, i: (n, 0, 0)),
                  pl.BlockSpec((1, LANES, n_sub), lambda n, i: (n, 0, 0)),
                  full(band)],
        out_specs=[pl.BlockSpec((LANES, NSA_HEADS * HEAD_DIM), lambda n, i: (n * nblk + i, 0)),
                   pl.BlockSpec((1, NSA_KV_HEADS, n_sel_pad, LANES), lambda n, i: (n, 0, 0, i))],
        out_shape=[jax.ShapeDtypeStruct((n_batch * seq, NSA_HEADS * HEAD_DIM), F32),
                   jax.ShapeDtypeStruct((n_batch, NSA_KV_HEADS, n_sel_pad, seq), F32)],
        scratch_shapes=[pltpu.VMEM((NSA_HEADS, _PS_PAD + n_sub, LANES), F32),
                        pltpu.VMEM((NSA_KV_HEADS, _PS_PAD + 4 * n_sel_pad, LANES), F32),
                        pltpu.VMEM((NSA_HEADS * HEAD_DIM, LANES), F32)],
        compiler_params=_cparams("parallel", "parallel"),
        name="cmpsel_prompt",
    )(qvt, kc, vct, band)


def _cmpsel_sample_body(q_ref, kc_ref, vct_ref, bias_ref, rmat_ref, oct_ref, sel_ref, ps_scr, *, n_sub, n_sel_pad,
                        n_top, past, dec_seq):
    s = _dot_nt(kc_ref[0], q_ref[0]) + bias_ref[...]
    m = jnp.max(s, axis=0, keepdims=True)
    e = jnp.exp2(s - m)
    p = e * (1.0 / jnp.sum(e, axis=0, keepdims=True))
    oct_ref[0] = _dot(vct_ref[0], p.astype(BF16))
    hi = p.astype(BF16)
    lo = (p - hi.astype(F32)).astype(BF16)
    psum = _dot(hi, rmat_ref[...]) + _dot(lo, rmat_ref[...])
    ps_scr[...] = jnp.zeros(ps_scr.shape, F32)
    ps_scr[_PS_PAD:_PS_PAD + n_sub, :] = psum
    col = lax.broadcasted_iota(jnp.int32, (n_sel_pad, LANES), 1)
    t = past + (col & (dec_seq - 1))
    sel = _select_blocks(_importance(ps_scr, n_sel_pad), t, n_top)
    sel_ref[0] = sel.T.astype(BF16)


def _cmpsel_sample(qs, kc, vct, bias_t, rmat, n_sel_pad, n_top, past, dec_seq):
    n, n_sub, _ = kc.shape
    full = lambda a: pl.BlockSpec(a.shape, lambda i: (0,) * a.ndim)
    return pl.pallas_call(
        functools.partial(_cmpsel_sample_body, n_sub=n_sub, n_sel_pad=n_sel_pad, n_top=n_top, past=past,
                          dec_seq=dec_seq),
        grid=(n,),
        in_specs=[pl.BlockSpec((1, LANES, LANES), lambda i: (i, 0, 0)),
                  pl.BlockSpec((1, n_sub, LANES), lambda i: (i, 0, 0)),
                  pl.BlockSpec((1, LANES, n_sub), lambda i: (i, 0, 0)),
                  full(bias_t), full(rmat)],
        out_specs=[pl.BlockSpec((1, LANES, LANES), lambda i: (i, 0, 0)),
                   pl.BlockSpec((1, LANES, n_sel_pad), lambda i: (i, 0, 0))],
        out_shape=[jax.ShapeDtypeStruct((n, LANES, LANES), F32), jax.ShapeDtypeStruct((n, LANES, n_sel_pad), BF16)],
        scratch_shapes=[pltpu.VMEM((_PS_PAD + 4 * n_sel_pad, LANES), F32)],
        compiler_params=_cparams("parallel"),
        name="cmpsel_sample",
    )(qs, kc, vct, bias_t, rmat)


def _flash_t_update(h, k, vt, qt, bias_ref, mask_rows, m_ref, l_ref, acc_ref):
    s = _dot(k, qt)
    m = m_ref[h]
    l = l_ref[h]
    acc = acc_ref[h]
    n_slab = SEL_BLOCK // SUBLANES
    for c in range(k.shape[0] // SEL_BLOCK):
        rows = slice(SEL_BLOCK * c, SEL_BLOCK * (c + 1))
        sc = s[rows]
        if bias_ref is not None:
            sc = sc + bias_ref[rows, :]
        if mask_rows is not None:
            sc = jnp.where(mask_rows[c:c + 1, :] > 0.5, sc, NEG_INF)
        slabs = [sc[SUBLANES * i:SUBLANES * (i + 1)] for i in range(n_slab)]
        cm = functools.reduce(jnp.maximum, slabs)
        for shift in (4, 2, 1):
            cm = jnp.maximum(cm, pltpu.roll(cm, shift, 0))
        m_new = jnp.maximum(m, cm)
        alpha = jnp.exp2(m - m_new)
        ps = [jnp.exp2(x - m_new) for x in slabs]
        l = alpha * l + functools.reduce(jnp.add, ps)
        pv = _dot(vt[:, rows], jnp.concatenate(ps, axis=0).astype(BF16))
        acc = jnp.concatenate([alpha] * (acc.shape[0] // SUBLANES), axis=0) * acc + pv
        m = m_new
    m_ref[h] = m
    l_ref[h] = l
    acc_ref[h] = acc


def _flash_prompt_body(qt_ref, ks_ref, vst_ref, kw_ref, vwt_ref, sel_ref, near_ref, winfar_ref, osel_ref, owin_ref,
                       ms, ls, accs, mw, lw, accw, *, tq):
    qi = pl.program_id(1)
    for m_r, l_r, acc_r in ((ms, ls, accs), (mw, lw, accw)):
        m_r[...] = jnp.full(m_r.shape, M_INIT, F32)
        l_r[...] = jnp.zeros(l_r.shape, F32)
        acc_r[...] = jnp.zeros(acc_r.shape, F32)
    blocks = tq // SEL_BLOCK

    def sel_tile(kt, near, parity=None):
        k0 = pl.multiple_of(kt * tq, tq)
        k = ks_ref[pl.ds(k0, tq), :]
        r0 = pl.multiple_of((kt // 2) * SUBLANES, SUBLANES)
        for g in range(NSA_KV_HEADS):
            r8 = sel_ref[0, g, pl.ds(r0, SUBLANES), :]
            if parity is None:
                r4 = jnp.where((kt & 1) == 1, r8[blocks:2 * blocks], r8[0:blocks])
            else:
                r4 = r8[blocks * parity:blocks * (parity + 1)]
            vt = vst_ref[HEAD_DIM * g:HEAD_DIM * (g + 1), pl.ds(k0, tq)]
            for hh in range(HPG):
                h = HPG * g + hh
                bias = None if near is None else near_ref.at[near, h]
                _flash_t_update(h, k, vt, qt_ref[LANES * h:LANES * (h + 1), :], bias, r4, ms, ls, accs)

    def win_tile(kt, near):
        k0 = pl.multiple_of(kt * tq, tq)
        k = kw_ref[pl.ds(k0, tq), :]
        for g in range(NSA_KV_HEADS):
            vt = vwt_ref[HEAD_DIM * g:HEAD_DIM * (g + 1), pl.ds(k0, tq)]
            for hh in range(HPG):
                h = HPG * g + hh
                bias = winfar_ref if near is None else near_ref.at[near, h]
                _flash_t_update(h, k, vt, qt_ref[LANES * h:LANES * (h + 1), :], bias, None, mw, lw, accw)

    n_far = jnp.maximum(qi - 1, 0)

    def far_body(j, carry):
        sel_tile(2 * j, None, 0)
        sel_tile(2 * j + 1, None, 1)
        return carry

    lax.fori_loop(0, n_far // 2, far_body, 0)

    @pl.when(n_far % 2 == 1)
    def _():
        sel_tile(n_far - 1, None, 0)

    @pl.when(qi >= 1)
    def _():
        sel_tile(qi - 1, 0)
        win_tile(qi - 1, 0)

    sel_tile(qi, 1)

    @pl.when(qi >= 2)
    def _():
        win_tile(qi - 2, None)

    win_tile(qi, 1)

    for out_ref, l_r, acc_r in ((osel_ref, ls, accs), (owin_ref, lw, accw)):
        o_t = jnp.concatenate([acc_r[h] * (1.0 / jnp.sum(l_r[h], axis=0, keepdims=True))
                               for h in range(NSA_HEADS)], axis=0)
        out_ref[...] = o_t.T


def _flash_prompt(kvb, qvt, sel_t, near, winfar, n_batch, seq, tq):
    nq = seq // tq
    assert WINDOW == 2 * tq and tq == 4 * SEL_BLOCK, "window = two key tiles; a key tile = 4 selection blocks"
    n_sel_pad = sel_t.shape[2]
    k_spec = lambda col: pl.BlockSpec((seq, LANES), lambda n, i: (n, col))
    vt_spec = lambda row: pl.BlockSpec((LANES, seq), lambda n, i: (row, n))
    full = lambda a: pl.BlockSpec(a.shape, lambda n, i: (0,) * a.ndim)
    out_spec = pl.BlockSpec((tq, NSA_HEADS * HEAD_DIM), lambda n, i: (n * nq + i, 0))
    state = [pltpu.VMEM((NSA_HEADS, SUBLANES, tq), F32), pltpu.VMEM((NSA_HEADS, SUBLANES, tq), F32),
             pltpu.VMEM((NSA_HEADS, HEAD_DIM, tq), F32)]
    out = jax.ShapeDtypeStruct((n_batch * seq, NSA_HEADS * HEAD_DIM), F32)
    return pl.pallas_call(
        functools.partial(_flash_prompt_body, tq=tq),
        grid=(n_batch, nq),
        in_specs=[pl.BlockSpec((NSA_HEADS * LANES, tq), lambda n, i: (0, n * nq + i)),
                  k_spec(2), vt_spec(NSA_HEADS), k_spec(4), vt_spec(NSA_HEADS + 1),
                  pl.BlockSpec((1, NSA_KV_HEADS, n_sel_pad, tq), lambda n, i: (n, 0, 0, i)),
                  full(near), full(winfar)],
        out_specs=[out_spec, out_spec],
        out_shape=[out, out],
        scratch_shapes=state + state,
        compiler_params=_cparams("parallel", "parallel"),
        name="flash_prompt",
    )(qvt, kvb, qvt, kvb, qvt, sel_t, near, winfar)


def _flash_sample_body(pt_ref, q_ref, kpool_ref, vpool_ref, ksn_ref, vsn_ref, sel_ref, e_ref, tailb_ref, newb_ref,
                       kwt_ref, vwt_ref, winb_ref, osel_ref, owin_ref, kbuf, vbuf, sem, s_scr, *, tk, n_pages, page):
    slot = _fetch_pages(pt_ref, (kpool_ref, vpool_ref), (kbuf, vbuf), sem, n_pages, page)
    q = q_ref[0]
    nt = n_pages * page // tk
    s_new = _dot_nt(q, ksn_ref[0]) + newb_ref[...]

    def score_tile(kt, tail):
        k0 = pl.multiple_of(kt * tk, tk)
        mask = _dot(sel_ref[0], e_ref[:, pl.ds(k0, tk)]) > 0.5
        s = _dot(q, kbuf[slot, :, pl.ds(k0, tk)].astype(BF16))
        if tail:
            s = s + tailb_ref[...]
        s = jnp.where(mask, s, NEG_INF)
        s_scr[:, pl.ds(k0, tk)] = s
        return jnp.max(s, axis=1, keepdims=True)

    m = lax.fori_loop(0, nt - 1, lambda kt, m: jnp.maximum(m, score_tile(kt, False)),
                      jnp.max(s_new, axis=1, keepdims=True), unroll=3)
    m = jnp.maximum(m, score_tile(nt - 1, True))

    def pv_tile(kt, carry):
        l, acc = carry
        k0 = pl.multiple_of(kt * tk, tk)
        p = jnp.exp2(s_scr[:, pl.ds(k0, tk)] - m)
        return (l + jnp.sum(p, axis=1, keepdims=True),
                acc + _dot_nt(p.astype(BF16), vbuf[slot, :, pl.ds(k0, tk)].astype(BF16)))

    p_new = jnp.exp2(s_new - m)
    l, acc = lax.fori_loop(0, nt, pv_tile, (jnp.sum(p_new, axis=1, keepdims=True),
                                            _dot(p_new.astype(BF16), vsn_ref[0])), unroll=4)
    osel_ref[0] = acc * (1.0 / l)

    s_w = _dot(q, kwt_ref[0].astype(BF16)) + winb_ref[...]
    s_new = _dot_nt(q, ksn_ref[1]) + newb_ref[...]
    m = jnp.maximum(jnp.max(s_w, axis=1, keepdims=True), jnp.max(s_new, axis=1, keepdims=True))
    p_w = jnp.exp2(s_w - m)
    p_new = jnp.exp2(s_new - m)
    l = jnp.sum(p_w, axis=1, keepdims=True) + jnp.sum(p_new, axis=1, keepdims=True)
    acc = _dot_nt(p_w.astype(BF16), vwt_ref[0].astype(BF16)) + _dot(p_new.astype(BF16), vsn_ref[1])
    owin_ref[0] = acc * (1.0 / l)


def _flash_sample(page_table, qs, kpool, vpool, kvn, selrows, e_all, tailb, newb, kw, vw, winb, tk):
    n, n_pages = page_table.shape
    page = kpool.shape[2]
    assert (n_pages * page // tk - 1) % 3 == 0 and (n_pages * page // tk) % 4 == 0, "unroll factors of the key loops"
    past = n_pages * page
    per_n = lambda a: pl.BlockSpec((1,) + a.shape[1:], lambda i, pt: (i,) + (0,) * (a.ndim - 1))
    full = lambda a: pl.BlockSpec(a.shape, lambda i, pt: (0,) * a.ndim)
    any_spec = pl.BlockSpec(memory_space=pl.ANY)
    new_spec = pl.BlockSpec((2, LANES, LANES), lambda i, pt: (i, 0, 0))
    out = jax.ShapeDtypeStruct((n, LANES, LANES), F32)
    grid_spec = pltpu.PrefetchScalarGridSpec(
        num_scalar_prefetch=1,
        grid=(n,),
        in_specs=[per_n(qs), any_spec, any_spec, new_spec, new_spec, per_n(selrows), full(e_all), full(tailb),
                  full(newb), per_n(kw), per_n(vw), full(winb)],
        out_specs=[per_n(out), per_n(out)],
        scratch_shapes=[pltpu.VMEM((2, LANES, past), F32), pltpu.VMEM((2, LANES, past), F32),
                        pltpu.SemaphoreType.DMA((2, 2)), pltpu.VMEM((LANES, past), F32)],
    )
    return pl.pallas_call(
        functools.partial(_flash_sample_body, tk=tk, n_pages=n_pages, page=page),
        grid_spec=grid_spec,
        out_shape=[out, out],
        compiler_params=_cparams("arbitrary"),
        name="flash_sample",
    )(page_table, qs, kpool, vpool, kvn[0], kvn[1], selrows, e_all, tailb, newb, kw, vw, winb)


def _mixout_body(an_ref, oc_ref, os_ref, ow_ref, gate_ref, x_ref, gb_ref, woa_ref, wob_ref, y_ref):
    gates = gate_ref[...]
    tm = gates.shape[0]
    lane = lax.broadcasted_iota(jnp.int32, (tm, LANES), 1)
    tiles = []
    for j in range(NSA_HEADS // 2):
        cols = slice(LANES * j, LANES * (j + 1))
        acc = None
        for br, o_ref in enumerate((oc_ref, os_ref, ow_ref)):
            c0 = 3 * (2 * j) + br
            c1 = 3 * (2 * j + 1) + br
            gcol = jnp.where(lane < HEAD_DIM, gates[:, c0:c0 + 1], gates[:, c1:c1 + 1])
            term = gcol * o_ref[:, cols]
            acc = term if acc is None else acc + term
        tiles.append(acc)
    bn = _rms(jnp.concatenate(tiles, axis=1), gb_ref[...]).astype(BF16)
    y_ref[...] = x_ref[...] + _dot(an_ref[...].astype(BF16), woa_ref[...]) + _dot(bn, wob_ref[...])


def _mixout(an, oc, osel, owin, gates, x, g_b, wo_a, wo_b, tm):
    m, d = x.shape
    row = lambda a: pl.BlockSpec((tm, a.shape[1]), lambda i: (i, 0))
    full = lambda a: pl.BlockSpec(a.shape, lambda i: (0,) * a.ndim)
    return pl.pallas_call(
        _mixout_body,
        grid=(m // tm,),
        in_specs=[row(an), row(oc), row(osel), row(owin), row(gates), row(x), full(g_b), full(wo_a), full(wo_b)],
        out_specs=row(x),
        out_shape=jax.ShapeDtypeStruct((m, d), F32),
        compiler_params=_cparams("parallel"),
        name="mix_out",
    )(an, oc, osel, owin, gates, x, g_b, wo_a, wo_b)


def _normmm_body(x_ref, g_ref, w_ref, o_ref):
    o_ref[...] = _dot(_rms(x_ref[...], g_ref[...]).astype(BF16), w_ref[...]).astype(o_ref.dtype)


def _norm_matmul(x, g, w, out_dtype, tm):
    m, d = x.shape
    n = w.shape[1]
    return pl.pallas_call(
        _normmm_body,
        grid=(m // tm,),
        in_specs=[pl.BlockSpec((tm, d), lambda i: (i, 0)), pl.BlockSpec(g.shape, lambda i: (0, 0)),
                  pl.BlockSpec(w.shape, lambda i: (0, 0))],
        out_specs=pl.BlockSpec((tm, n), lambda i: (i, 0)),
        out_shape=jax.ShapeDtypeStruct((m, n), out_dtype),
        compiler_params=_cparams("parallel"),
        name="norm_matmul",
    )(x, g, w)


def _mmres_body(a_ref, w_ref, r_ref, o_ref):
    o_ref[...] = r_ref[...] + _dot(a_ref[...].astype(BF16), w_ref[...])


def _matmul_residual(a, w, res, tm):
    m, k = a.shape
    n = w.shape[1]
    return pl.pallas_call(
        _mmres_body,
        grid=(m // tm,),
        in_specs=[pl.BlockSpec((tm, k), lambda i: (i, 0)), pl.BlockSpec(w.shape, lambda i: (0, 0)),
                  pl.BlockSpec((tm, n), lambda i: (i, 0))],
        out_specs=pl.BlockSpec((tm, n), lambda i: (i, 0)),
        out_shape=jax.ShapeDtypeStruct((m, n), F32),
        compiler_params=_cparams("parallel"),
        name="matmul_residual",
    )(a, w, res)


def _xattn_body(q_ref, mk_ref, mv_ref, o_ref, *, n_heads):
    hd = q_ref.shape[2] // n_heads
    scale = hd ** -0.5
    for h in range(n_heads):
        cols = slice(hd * h, hd * (h + 1))
        s = _dot_nt(q_ref[0, :, cols], mk_ref[0, :, cols].astype(BF16)) * scale
        e = jnp.exp(s - jnp.max(s, axis=1, keepdims=True))
        p = e * (1.0 / jnp.sum(e, axis=1, keepdims=True))
        o_ref[0, :, cols] = _dot(p.astype(BF16), mv_ref[0, :, cols].astype(BF16))


def _xattn_core(q, mk, mv, tm):
    n, t, d = q.shape
    return pl.pallas_call(
        functools.partial(_xattn_body, n_heads=XA_HEADS),
        grid=(n, t // tm),
        in_specs=[pl.BlockSpec((1, tm, d), lambda b, i: (b, i, 0)),
                  pl.BlockSpec((1,) + mk.shape[1:], lambda b, i: (b, 0, 0)),
                  pl.BlockSpec((1,) + mv.shape[1:], lambda b, i: (b, 0, 0))],
        out_specs=pl.BlockSpec((1, tm, d), lambda b, i: (b, i, 0)),
        out_shape=jax.ShapeDtypeStruct((n, t, d), F32),
        compiler_params=_cparams("parallel", "parallel"),
        name="xattn_core",
    )(q, mk, mv)


def _ffn_body(x_ref, g_ref, wu_ref, wg_ref, cw_ref, cb_ref, wd_ref, gf_ref, s1_ref, s2_ref, y_ref, a_ref, h_scr,
              acc_scr, carry_scr, *, tf, dec_seq):
    i = pl.program_id(1)
    j = pl.program_id(2)
    tm = x_ref.shape[0]

    @pl.when(j == 0)
    def _():
        h_scr[...] = _rms(x_ref[...], g_ref[...]).astype(BF16)
        acc_scr[...] = jnp.zeros(acc_scr.shape, F32)

    a = _dot(h_scr[...], wu_ref[...])
    gt = _dot(h_scr[...], wg_ref[...])
    row = lax.broadcasted_iota(jnp.int32, (tm, tf), 0)
    r1 = pltpu.roll(a, 1, 0)
    r2 = pltpu.roll(a, 2, 0)
    if dec_seq is None:
        cols = pl.ds(pl.multiple_of(j * tf, LANES), tf)

        @pl.when(i == 0)
        def _():
            carry_scr[:, cols] = jnp.zeros((SUBLANES, tf), F32)

        p0 = carry_scr[SUBLANES - 2:SUBLANES - 1, cols]
        p1 = carry_scr[SUBLANES - 1:SUBLANES, cols]
        a1 = jnp.where(row == 0, p1, r1)
        a2 = jnp.where(row == 0, p0, jnp.where(row == 1, p1, r2))
        carry_scr[:, cols] = a[tm - SUBLANES:tm, :]
        a_ref[0] = a[tm - SUBLANES:tm, :]
    else:
        t = row & (dec_seq - 1)
        a1 = jnp.where(t == 0, s1_ref[...], r1)
        a2 = jnp.where(t < 2, s2_ref[...], r2)
        a_ref[...] = a
    c = cb_ref[...] + a2 * cw_ref[0:1, :] + a1 * cw_ref[1:2, :] + a * cw_ref[2:3, :]
    acc_scr[...] += _dot((_gelu(c) * gt).astype(BF16), wd_ref[...])

    @pl.when(j == pl.num_programs(2) - 1)
    def _():
        y_ref[...] = _rms(x_ref[...] + acc_scr[...], gf_ref[...])


def _ffn(x, g, wu, wg, cw, cb, wd, g_final, s1, s2, n_seq, tm, tf, dec_seq):
    m, d = x.shape
    f = wu.shape[1]
    nt = m // n_seq // tm
    nf = f // tf
    row = pl.BlockSpec((tm, d), lambda n, i, j: (n * nt + i, 0))
    vec = lambda a: pl.BlockSpec(a.shape, lambda n, i, j: (0,) * a.ndim)
    fcol = lambda a: pl.BlockSpec((a.shape[0], tf), lambda n, i, j: (0, j))
    if dec_seq is None:
        st_spec = pl.BlockSpec((SUBLANES, tf), lambda n, i, j: (0, j))
        a_spec = pl.BlockSpec((1, SUBLANES, tf), lambda n, i, j: (n * nt + i, 0, j))
        a_shape = jax.ShapeDtypeStruct((n_seq * nt, SUBLANES, f), F32)
    else:
        st_spec = pl.BlockSpec((tm, tf), lambda n, i, j: (n * nt + i, j))
        a_spec = st_spec
        a_shape = jax.ShapeDtypeStruct((m, f), F32)
    return pl.pallas_call(
        functools.partial(_ffn_body, tf=tf, dec_seq=dec_seq),
        grid=(n_seq, nt, nf),
        in_specs=[row, vec(g), fcol(wu), fcol(wg), fcol(cw), fcol(cb),
                  pl.BlockSpec((tf, d), lambda n, i, j: (j, 0)), vec(g_final), st_spec, st_spec],
        out_specs=[row, a_spec],
        out_shape=[jax.ShapeDtypeStruct((m, d), F32), a_shape],
        scratch_shapes=[pltpu.VMEM((tm, d), BF16), pltpu.VMEM((tm, d), F32), pltpu.VMEM((SUBLANES, f), F32)],
        compiler_params=_cparams("parallel", "arbitrary", "arbitrary"),
        name="conv_ffn",
    )(x, g, wu, wg, cw, cb, wd, g_final, s1, s2)


def _t5_bucket_np(dist):
    n = np.maximum(dist, 0)
    max_exact = REL_BUCKETS // 2
    nf = np.maximum(n, 1).astype(np.float64)
    large = max_exact + (np.log(nf / max_exact) / math.log(REL_MAX_DIST / max_exact)
                         * (REL_BUCKETS - max_exact)).astype(np.int32)
    large = np.minimum(large, REL_BUCKETS - 1)
    return np.where(n < max_exact, n, large).astype(np.int32)


def _bias_vectors(rel_bias, dist, live, masked):
    dist, live, masked = np.broadcast_arrays(dist, live, masked)
    onehot = np.zeros(dist.shape + (REL_BUCKETS,), np.float32)
    np.put_along_axis(onehot, _t5_bucket_np(dist)[..., None], 1.0, axis=-1)
    onehot[..., REL_BUCKETS - 1] -= 1.0
    onehot *= live[..., None]
    add = np.where(masked, NEG_INF, 0.0).astype(np.float32)
    table = jnp.einsum("...b,bh->h...", jnp.asarray(onehot), rel_bias, precision=lax.Precision.HIGHEST)
    return table * LOG2E + add


def _toeplitz(v, rows, cols, stride):
    length = v.shape[-1]
    w = length - stride
    assert cols <= w
    t = jnp.tile(v, (1,) * (v.ndim - 1) + (rows,))[..., :rows * w]
    return t.reshape(v.shape[:-1] + (rows, w))[..., :cols]


def _bias_tables_prompt(rel_bias, tq):
    k = np.arange(2 * tq)
    upper = k < tq
    diag = _bias_vectors(rel_bias, k, upper, ~upper)
    prev = _bias_vectors(rel_bias, np.where(upper, k + tq, k - tq), True, False)
    near = jnp.stack([_toeplitz(prev, tq, tq, 1), _toeplitz(diag, tq, tq, 1)])
    winfar = _toeplitz(jnp.asarray(np.where(upper, NEG_INF, 0.0), F32), tq, tq, 1)
    length = 4 * LANES
    off = 2 * SUBLANES * CMP_STRIDE - (CMP_LEN - 1)
    kk = np.arange(length)
    d = np.where(kk < LANES, kk + off, kk - length + off)
    band = _toeplitz(_bias_vectors(rel_bias, d, (d >= 0) & ((kk < LANES) | (kk >= LANES + CMP_STRIDE)), False),
                     3 * SUBLANES, LANES, CMP_STRIDE)
    return near, winfar, band


def _bias_tables_sample(rel_bias, past, dec_seq, n_cmp, n_sub, tk, win_buf):
    t = np.arange(dec_seq)[:, None]

    def rows(dist, live, masked):
        tab = _bias_vectors(rel_bias, dist, live, masked)
        tab = tab.reshape(NSA_HEADS * dec_seq, -1)
        return jnp.pad(tab, ((0, LANES - NSA_HEADS * dec_seq), (0, 0)))

    c = np.arange(n_sub)[None, :]
    d_cmp = past + t - (c * CMP_STRIDE + CMP_LEN - 1)
    ok = (d_cmp >= 0) & (c < n_cmp)
    cmp_t = rows(d_cmp, ok, ~ok).T
    tail = rows(tk + t - np.arange(tk)[None, :], True, False)
    jn = np.arange(LANES)[None, :]
    ok = (jn <= t) & (jn < dec_seq)
    newb = rows(t - jn, ok, ~ok)
    d_win = win_buf + t - np.arange(win_buf)[None, :]
    ok = d_win < WINDOW
    winb = rows(d_win, ok, ~ok)
    return cmp_t, tail, newb, winb


def _row_heads(dec_seq):
    rows = np.arange(LANES)
    live = rows < NSA_HEADS * dec_seq
    return np.where(live, rows // dec_seq, 0), rows % dec_seq, live


def _prep_in_proj(w_in, d_a):
    d_b = NSA_HEADS * HEAD_DIM
    o_q = 2 * d_a
    o_kv = o_q + d_b
    o_g = o_kv + 6 * NSA_KV_HEADS * HEAD_DIM
    wq = w_in[:, o_q:o_kv].reshape(-1, NSA_HEADS, HEAD_DIM) * (ATT_SCALE * LOG2E)
    slot = jnp.zeros((w_in.shape[0], NSA_HEADS, NSA_KV_HEADS, HEAD_DIM), w_in.dtype)
    for h in range(NSA_HEADS):
        slot = slot.at[:, h, h // HPG].set(wq[:, h])
    wg = jnp.pad(w_in[:, o_g:], ((0, 0), (0, LANES - (w_in.shape[1] - o_g))))
    w_pad = jnp.concatenate([w_in[:, :o_q], slot.reshape(w_in.shape[0], -1), w_in[:, o_kv:o_g], wg], axis=1)
    kvw = NSA_KV_HEADS * HEAD_DIM
    w_vt = jnp.concatenate([slot.reshape(w_in.shape[0], -1), w_in[:, o_kv + 3 * kvw:o_kv + 4 * kvw],
                            w_in[:, o_kv + 5 * kvw:o_kv + 6 * kvw]], axis=1).T
    return w_pad.astype(BF16), w_vt.astype(BF16)


def _prep_compress(w1, b1, w2, b2, pe):
    eye = jnp.eye(NSA_KV_HEADS, dtype=w1.dtype)
    hid = w1.shape[-1]
    kvw = NSA_KV_HEADS * HEAD_DIM
    w1p = jnp.einsum("apdh,gk->pgdakh", w1.reshape(2, CMP_STRIDE, HEAD_DIM, hid), eye)
    w1p = w1p.reshape(CMP_STRIDE // 2, 2 * kvw, 2 * NSA_KV_HEADS * hid)
    pe_rows = jnp.broadcast_to(pe.reshape(2, CMP_STRIDE, 1, HEAD_DIM), (2, CMP_STRIDE, NSA_KV_HEADS, HEAD_DIM))
    pe_rows = pe_rows.reshape(2, CMP_STRIDE // 2, 2 * kvw).transpose(1, 0, 2)
    pe_rows = jnp.pad(pe_rows, ((0, 0), (0, 2 * SUBLANES - 2), (0, 0)))
    w2big = jnp.einsum("hd,gk->ghkd", w2, eye).reshape(NSA_KV_HEADS * hid, kvw)
    return (w1p.astype(BF16), pe_rows.astype(BF16), jnp.tile(b1, NSA_KV_HEADS)[None, :], w2big.astype(BF16),
            jnp.tile(b2, NSA_KV_HEADS)[None, :])


def _prep_compress_kv(p):
    w1k, pek, b1k, w2k, b2k = _prep_compress(p["w_cmp1_k"], p["b_cmp1_k"], p["w_cmp2_k"], p["b_cmp2_k"], p["pe_cmp_k"])
    w1v, pev, b1v, w2v, b2v = _prep_compress(p["w_cmp1_v"], p["b_cmp1_v"], p["w_cmp2_v"], p["b_cmp2_v"], p["pe_cmp_v"])
    return dict(w1k=w1k, w1v=w1v, pek=pek, pev=pev, b1k=b1k, b1v=b1v, w2k=w2k, w2vt=w2v.T, b2k=b2k, b2vt=b2v.T)


def _block_expand(n_sel_pad, n_keys):
    s = np.arange(n_sel_pad)[:, None]
    k = np.arange(n_keys)[None, :]
    return jnp.asarray((k // SEL_BLOCK == s), dtype=BF16)


def _round_up(x, m):
    return -(-x // m) * m


def _shared_front(x2d, p, w_in_pad, w_vt, period, mix_w, mix_b, tm):
    d_a = p["g_a"].shape[0]
    u, v, qpad, kc, vc, ks, vs, kw, vw, kvb, vt, gates = _in_proj(x2d, p["g_mix"][None, :], w_in_pad, w_vt, d_a, tm)
    an, v_rows = _gmlp(u, v, p["ln_v_g"][None, :], p["ln_v_b"][None, :], mix_w, mix_b, p["g_a"][None, :], period, tm)
    return an, v_rows, qpad, (kc, vc, ks, vs, kw, vw), kvb, vt, gates


def _shared_back(x2d, an, oc, osel, owin, gates, p, wts, mk, mv, n_seq, s1, s2, g_final, dec_seq, tm):
    d = x2d.shape[1]
    x1 = _mixout(an, oc, osel, owin, gates, x2d, p["g_b"][None, :], wts["wo_a"], wts["wo_b"], tm)
    qx = _norm_matmul(x1, p["g_xa"][None, :], wts["w_xq"], BF16, tm)
    t = x1.shape[0] // n_seq
    if dec_seq is None:
        ox = _xattn_core(qx.reshape(n_seq, t, d), mk, mv, tm).reshape(-1, d)
    else:
        qx3 = jnp.pad(qx.reshape(n_seq, t, d), ((0, 0), (0, SUBLANES - t), (0, 0)))
        ox = _xattn_core(qx3, mk, mv, SUBLANES)[:, :t].reshape(-1, d)
    x2 = _matmul_residual(ox, wts["w_xo"], x1, tm)
    n_ffn_seq = n_seq if dec_seq is None else 1
    return _ffn(x2, p["g_ffn"][None, :], wts["w_up"], wts["w_gate"], wts["conv_w"], p["conv_b"][None, :],
                wts["w_down"], g_final[None, :], s1, s2, n_ffn_seq, tm, wts["tf"], dec_seq)


def _heads_to_rows(qpad, n, t):
    q = qpad.reshape(n, t, NSA_HEADS, LANES).transpose(0, 2, 1, 3).reshape(n, NSA_HEADS * t, LANES)
    return jnp.pad(q, ((0, 0), (0, LANES - NSA_HEADS * t), (0, 0)))


def _rows_to_tokens(o, n, t):
    o = o[:, :NSA_HEADS * t].reshape(n, NSA_KV_HEADS, HPG, t, NSA_KV_HEADS, HEAD_DIM)
    o = jnp.stack([o[:, g, :, :, g] for g in range(NSA_KV_HEADS)], axis=1)
    return o.transpose(0, 3, 1, 2, 4).reshape(n * t, NSA_HEADS * HEAD_DIM)


def kernel(x_prompt, x_sample, mem_prompt, cache_cmp_k, cache_cmp_v, cache_sel_k, cache_sel_v, cache_win_k,
           cache_win_v, cache_mem_k, cache_mem_v, state_conv, page_table, g_mix, w_in, w_s, b_s, ln_v_g, ln_v_b,
           w_cmp1_k, b_cmp1_k, w_cmp2_k, b_cmp2_k, pe_cmp_k, w_cmp1_v, b_cmp1_v, w_cmp2_v, b_cmp2_v, pe_cmp_v,
           rel_bias, g_a, g_b, w_o, g_xa, g_mem, w_xq, w_mk, w_mv, w_xo, g_ffn, w_up, w_gate, conv_w, conv_b,
           w_down, g_final):
    depth = w_in.shape[0]
    assert depth == 1, "the layer loop is written for a single layer"
    bsz, seq, d = x_prompt.shape
    nd, dec_seq, _ = x_sample.shape
    mem_len = mem_prompt.shape[1]
    d_a = g_a.shape[1]
    d_ff = w_up.shape[2]
    n_pages = page_table.shape[1]
    page = cache_cmp_k.shape[2]
    past = n_pages * page
    kvw = NSA_KV_HEADS * HEAD_DIM
    win_buf = cache_win_k.shape[2]
    assert kvw == LANES and seq % 256 == 0 and past % 512 == 0 and dec_seq == 4 and nd * dec_seq % LANES == 0
    tm = 512
    tq = 256
    tk_s = 512
    tf = d_ff // 2 if (d_ff // 2) % LANES == 0 else d_ff

    l = 0
    p = dict(g_mix=g_mix[l], ln_v_g=ln_v_g[l], ln_v_b=ln_v_b[l], w_cmp1_k=w_cmp1_k[l], b_cmp1_k=b_cmp1_k[l],
             w_cmp2_k=w_cmp2_k[l], b_cmp2_k=b_cmp2_k[l], pe_cmp_k=pe_cmp_k[l], w_cmp1_v=w_cmp1_v[l],
             b_cmp1_v=b_cmp1_v[l], w_cmp2_v=w_cmp2_v[l], b_cmp2_v=b_cmp2_v[l], pe_cmp_v=pe_cmp_v[l], g_a=g_a[l],
             g_b=g_b[l], g_xa=g_xa[l], g_ffn=g_ffn[l], conv_b=conv_b[l])
    w_in_pad, w_vt = _prep_in_proj(w_in[l], d_a)
    cw = _prep_compress_kv(p)
    wts = dict(wo_a=w_o[l, :d_a].astype(BF16), wo_b=w_o[l, d_a:].astype(BF16), w_xq=w_xq[l].astype(BF16),
               w_xo=w_xo[l].astype(BF16), w_up=w_up[l].astype(BF16), w_gate=w_gate[l].astype(BF16),
               w_down=w_down[l].astype(BF16), conv_w=jnp.pad(conv_w[l], ((0, SUBLANES - CONV_W), (0, 0))), tf=tf)
    bias_lanes = jnp.repeat(b_s[l].T, d_a // A_GROUPS, axis=1)

    xp = x_prompt.reshape(bsz * seq, d)
    an, _, _, kv6, kvb, qvt, gates = _shared_front(xp, p, w_in_pad, w_vt, CHUNK, w_s[l], bias_lanes, tm)
    kc, vct = _compress_prompt(kv6[0], kv6[1], cw, bsz, seq)
    n_sel = seq // SEL_BLOCK
    n_sel_pad = _round_up(n_sel, LANES)
    near, winfar, band = _bias_tables_prompt(rel_bias, tq)
    oc, sel_t = _cmpsel_prompt(qvt, kc, vct, band, bsz, seq, n_sel_pad, min(SEL_TOP, n_sel))
    osel, owin = _flash_prompt(kvb, qvt, sel_t, near, winfar, bsz, seq, tq)
    mem_kv = _norm_matmul(mem_prompt.reshape(bsz * mem_len, d), g_mem[l][None, :],
                          jnp.concatenate([w_mk[l], w_mv[l]], axis=1).astype(BF16), F32, min(tm, bsz * mem_len))
    mk_p = mem_kv[:, :d].reshape(bsz, mem_len, d)
    mv_p = mem_kv[:, d:].reshape(bsz, mem_len, d)
    zero_state = jnp.zeros((SUBLANES, d_ff), F32)
    yp, a_tail = _shared_back(xp, an, oc, osel, owin, gates, p, wts, mk_p, mv_p, bsz, zero_state, zero_state,
                              g_final, None, tm)
    keep = min(WINDOW, seq)
    shp = lambda a: a.reshape(1, bsz, seq, NSA_KV_HEADS, HEAD_DIM)
    prompt_state = (shp(kv6[0]), shp(kv6[1]), shp(kv6[2]), shp(kv6[3]), shp(kv6[4])[:, :, -keep:],
                    shp(kv6[5])[:, :, -keep:], mk_p.reshape(1, bsz, mem_len, XA_HEADS, d // XA_HEADS),
                    mv_p.reshape(1, bsz, mem_len, XA_HEADS, d // XA_HEADS),
                    a_tail.reshape(bsz, -1, SUBLANES, d_ff)[None, :, -1, SUBLANES - (CONV_W - 1):])

    xs = x_sample.reshape(nd * dec_seq, d)
    reps = CHUNK // dec_seq
    mix_w_s = jnp.tile(w_s[l][:, :dec_seq, :dec_seq], (1, reps, reps))
    bias_s = jnp.repeat(jnp.tile(b_s[l][:, :dec_seq].T, (reps, 1)), d_a // A_GROUPS, axis=1)
    tms = min(tm, nd * dec_seq)
    an_s, v_rows, qpad_s, kv6s, kvb_s, _, gates_s = _shared_front(xs, p, w_in_pad, w_vt, dec_seq, mix_w_s, bias_s,
                                                                 tms)
    pools = [c[l].transpose(0, 2, 3, 1).reshape(c.shape[1], kvw, page)
             for c in (cache_cmp_k, cache_cmp_v, cache_sel_k, cache_sel_v)]
    kc_s, vct_s = _compress_sample(page_table, pools[0], pools[1], cw)
    n_sub_s = past // CMP_STRIDE
    n_cmp_s = (past + dec_seq) // CMP_STRIDE - 1
    n_sel_s = -(-(past + dec_seq) // SEL_BLOCK)
    n_sel_pad_s = _round_up(n_sel_s, LANES)
    cmp_t, tail, newb, winb = _bias_tables_sample(rel_bias, past, dec_seq, n_cmp_s, n_sub_s, tk_s, win_buf)
    h_of, t_of, live = _row_heads(dec_seq)
    same = (live[:, None] & live[None, :] & ((h_of[:, None] // HPG) == (h_of[None, :] // HPG))
            & (t_of[:, None] == t_of[None, :]))
    qs = _heads_to_rows(qpad_s, nd, dec_seq)
    oct_s, selrows = _cmpsel_sample(qs, kc_s, vct_s, cmp_t, jnp.asarray(same, dtype=BF16), n_sel_pad_s,
                                    min(SEL_TOP, n_sel_s), past, dec_seq)
    newrows = lambda a: jnp.pad(a.reshape(nd, dec_seq, LANES), ((0, 0), (0, LANES - dec_seq), (0, 0)))
    kvn = (jnp.stack([newrows(kvb_s[:, 2 * LANES:3 * LANES]), newrows(kvb_s[:, 4 * LANES:5 * LANES])], axis=1)
           .reshape(2 * nd, LANES, LANES),
           jnp.stack([newrows(kvb_s[:, 3 * LANES:4 * LANES]), newrows(kvb_s[:, 5 * LANES:6 * LANES])], axis=1)
           .reshape(2 * nd, LANES, LANES))
    wk = cache_win_k[l].transpose(0, 2, 3, 1).reshape(nd, kvw, win_buf)
    wv = cache_win_v[l].transpose(0, 2, 3, 1).reshape(nd, kvw, win_buf)
    osel_r, owin_r = _flash_sample(page_table, qs, pools[2], pools[3], kvn, selrows,
                                   _block_expand(n_sel_pad_s, past), tail, newb, wk, wv, winb, tk_s)
    oc_s = _rows_to_tokens(oct_s.transpose(0, 2, 1), nd, dec_seq)
    osel_s = _rows_to_tokens(osel_r, nd, dec_seq)
    owin_s = _rows_to_tokens(owin_r, nd, dec_seq)
    st = state_conv[l]
    zrow = jnp.zeros_like(st[:, :1])
    s1 = jnp.concatenate([st[:, 1:2]] + [zrow] * (dec_seq - 1), axis=1).reshape(nd * dec_seq, d_ff)
    s2 = jnp.concatenate([st[:, 0:1], st[:, 1:2]] + [zrow] * (dec_seq - 2), axis=1).reshape(nd * dec_seq, d_ff)
    mk_s = cache_mem_k[l].reshape(nd, mem_len, d)
    mv_s = cache_mem_v[l].reshape(nd, mem_len, d)
    ys, a_full = _shared_back(xs, an_s, oc_s, osel_s, owin_s, gates_s, p, wts, mk_s, mv_s, nd, s1, s2, g_final,
                              dec_seq, tms)
    keep_s = min(WINDOW, past + dec_seq)
    shs = lambda a: a.reshape(1, nd, dec_seq, NSA_KV_HEADS, HEAD_DIM)
    win_new = lambda cache, new: jnp.concatenate([cache[l], shs(new)[0]], axis=1)[None, :, -keep_s:]
    sample_state = (shs(kv6s[0]), shs(kv6s[1]), shs(kv6s[2]), shs(kv6s[3]), win_new(cache_win_k, kv6s[4]),
                    win_new(cache_win_v, kv6s[5]), v_rows.reshape(1, nd, dec_seq, d_a),
                    a_full.reshape(1, nd, dec_seq, d_ff)[:, :, dec_seq - (CONV_W - 1):])

    return (yp.reshape(bsz, seq, d), ys.reshape(nd, dec_seq, d)) + prompt_state + sample_state
```

```python
import functools
import math

import numpy as np
import jax
import jax.numpy as jnp
from jax import lax
from jax.experimental import pallas as pl
from jax.experimental.pallas import tpu as pltpu

F32 = jnp.float32
BF16 = jnp.bfloat16

LANES = 128
SUBLANES = 8
VMEM_LIMIT_BYTES = 56 * 1024 * 1024

NORM_EPS = 1e-6
NEG_INF = -1e30
M_INIT = -1e29
KNOCKED_OUT = -3e38
FORCE_BONUS = 1e4

A_GROUPS = 8
CHUNK = 128
NSA_HEADS = 8
HEAD_DIM = 64
NSA_KV_HEADS = 2
HPG = NSA_HEADS // NSA_KV_HEADS
CMP_LEN = 32
CMP_STRIDE = 16
SEL_BLOCK = 64
SEL_TOP = 16
WINDOW = 512
REL_BUCKETS = 32
REL_MAX_DIST = 128
XA_HEADS = 4
CONV_W = 3
ATT_SCALE = HEAD_DIM ** -0.5
LOG2E = math.log2(math.e)

_NT = (((1,), (1,)), ((), ()))


def _cparams(*sem):
    return pltpu.CompilerParams(dimension_semantics=sem, vmem_limit_bytes=VMEM_LIMIT_BYTES)


def _gelu(x):
    return 0.5 * x * (1.0 + jnp.tanh(0.7978845608028654 * (x + 0.044715 * (x * x * x))))


def _rms(x, g):
    return x * lax.rsqrt(jnp.mean(x * x, axis=-1, keepdims=True) + NORM_EPS) * g


def _dot(a, b):
    return jnp.dot(a, b, preferred_element_type=F32)


def _dot_nt(a, b):
    return lax.dot_general(a, b, _NT, preferred_element_type=F32)


def _inproj_body(x_ref, g_ref, w_ref, wvt_ref, u_ref, v_ref, q_ref, kc_ref, vc_ref, ks_ref, vs_ref, kw_ref,
                 vw_ref, kvb_ref, vt_ref, gate_ref, *, d_a):
    h = _rms(x_ref[...], g_ref[...]).astype(BF16)

    def mm(lo, hi):
        return _dot(h, w_ref[:, lo:hi])

    u_ref[...] = mm(0, d_a)
    v_ref[...] = mm(d_a, 2 * d_a)
    o = 2 * d_a
    q_ref[...] = mm(o, o + NSA_HEADS * LANES).astype(BF16)
    o += NSA_HEADS * LANES
    for i, r in enumerate((kc_ref, vc_ref, ks_ref, vs_ref, kw_ref, vw_ref)):
        z = mm(o + LANES * i, o + LANES * (i + 1))
        r[...] = z
        kvb_ref[:, LANES * i:LANES * (i + 1)] = z.astype(BF16)
    o += 6 * LANES
    gate_ref[...] = 1.0 / (1.0 + jnp.exp(-mm(o, o + LANES)))
    vt_ref[...] = _dot_nt(wvt_ref[...], h).astype(BF16)


def _in_proj(x, g, w_pad, w_vt, d_a, tm):
    m, d = x.shape
    row = lambda n: pl.BlockSpec((tm, n), lambda i: (i, 0))
    full = lambda a: pl.BlockSpec(a.shape, lambda i: (0,) * a.ndim)
    kv = jax.ShapeDtypeStruct((m, LANES), F32)
    return pl.pallas_call(
        functools.partial(_inproj_body, d_a=d_a),
        grid=(m // tm,),
        in_specs=[row(d), full(g), full(w_pad), full(w_vt)],
        out_specs=[row(d_a), row(d_a), row(NSA_HEADS * LANES)] + [row(LANES)] * 6
                  + [row(6 * LANES), pl.BlockSpec((w_vt.shape[0], tm), lambda i: (0, i)), row(LANES)],
        out_shape=[jax.ShapeDtypeStruct((m, d_a), F32), jax.ShapeDtypeStruct((m, d_a), F32),
                   jax.ShapeDtypeStruct((m, NSA_HEADS * LANES), BF16)] + [kv] * 6
                  + [jax.ShapeDtypeStruct((m, 6 * LANES), BF16), jax.ShapeDtypeStruct((w_vt.shape[0], m), BF16),
                     jax.ShapeDtypeStruct((m, LANES), F32)],
        compiler_params=_cparams("parallel"),
        name="in_proj",
    )(x, g, w_pad, w_vt)


def _gmlp_body(u_ref, v_ref, lng_ref, lnb_ref, w_ref, bias_ref, ga_ref, a_ref, vr_ref, *, period_log2, rows):
    ri = lax.broadcasted_iota(jnp.int32, (CHUNK, CHUNK), 0)
    ci = lax.broadcasted_iota(jnp.int32, (CHUNK, CHUNK), 1)
    mask = (ci <= ri) & ((ri >> period_log2) == (ci >> period_log2))
    wm = [jnp.where(mask, w_ref[g], 0.0).astype(BF16) for g in range(A_GROUPS)]
    lane = lax.broadcasted_iota(jnp.int32, (CHUNK, LANES), 1)
    for c in range(rows // CHUNK):
        rs = slice(CHUNK * c, CHUNK * (c + 1))
        gv = _gelu(v_ref[rs, :])
        mu = jnp.mean(gv, axis=-1, keepdims=True)
        var = jnp.mean(jnp.square(gv - mu), axis=-1, keepdims=True)
        vn = (gv - mu) * lax.rsqrt(var + NORM_EPS) * lng_ref[...] + lnb_ref[...]
        vr_ref[rs, :] = vn
        vb = vn.astype(BF16)
        tiles = []
        for j in range(A_GROUPS // 2):
            vj = vb[:, LANES * j:LANES * (j + 1)]
            tiles.append(jnp.where(lane < LANES // 2, _dot(wm[2 * j], vj), _dot(wm[2 * j + 1], vj)))
        mixed = jnp.concatenate(tiles, axis=1) + bias_ref[...]
        a_ref[rs, :] = _rms(_gelu(u_ref[rs, :]) * mixed, ga_ref[...])


def _gmlp(u, v, ln_g, ln_b, w_mix, bias_full, g_a, period, rows):
    m, d_a = u.shape
    row = pl.BlockSpec((rows, d_a), lambda i: (i, 0))
    full = lambda a: pl.BlockSpec(a.shape, lambda i: (0,) * a.ndim)
    return pl.pallas_call(
        functools.partial(_gmlp_body, period_log2=int(math.log2(period)), rows=rows),
        grid=(m // rows,),
        in_specs=[row, row, full(ln_g), full(ln_b), full(w_mix), full(bias_full), full(g_a)],
        out_specs=[row, row],
        out_shape=[jax.ShapeDtypeStruct((m, d_a), F32)] * 2,
        compiler_params=_cparams("parallel"),
        name="gmlp",
    )(u, v, ln_g, ln_b, w_mix, bias_full, g_a)


_CW_KEYS = ("w1k", "w1v", "pek", "pev", "b1k", "b1v", "w2k", "w2vt", "b2k", "b2vt")


def _compress_rows(xk_ref, xv_ref, w, kc_ref, vct_ref):
    n_sub = kc_ref.shape[1]
    hid_w = w["b1k"].shape[1]

    def hidden(x_ref, w1_ref, pe_ref, b1_ref):
        fs = jnp.zeros((n_sub, 2 * hid_w), F32)
        pc = jnp.zeros((2 * SUBLANES, 2 * hid_w), F32)
        for pp in range(CMP_STRIDE // 2):
            xp = jnp.concatenate([x_ref[pl.ds(2 * pp + j, n_sub, stride=CMP_STRIDE), :] for j in range(2)],
                                 axis=1).astype(BF16)
            fs = fs + _dot(xp, w1_ref[pp])
            pc = pc + _dot(pe_ref[pp], w1_ref[pp])
        const = pc[0:1, :hid_w] + pc[1:2, hid_w:] + b1_ref[...]
        hid = fs[:, :hid_w] + pltpu.roll(fs[:, hid_w:], n_sub - 1, 0) + const
        return _gelu(hid).astype(BF16)

    kc_ref[0] = (_dot(hidden(xk_ref, w["w1k"], w["pek"], w["b1k"]), w["w2k"][...]) + w["b2k"][...]).astype(BF16)
    vct_ref[0] = (_dot_nt(w["w2vt"][...], hidden(xv_ref, w["w1v"], w["pev"], w["b1v"]))
                  + w["b2vt"][...]).astype(BF16)


def _compress_prompt_body(xk_ref, xv_ref, *refs):
    w = dict(zip(_CW_KEYS, refs[:len(_CW_KEYS)]))
    _compress_rows(xk_ref, xv_ref, w, *refs[len(_CW_KEYS):])


def _compress_prompt(kc_rows, vc_rows, cw, n_batch, seq):
    n_sub = seq // CMP_STRIDE
    blk = pl.BlockSpec((seq, LANES), lambda i: (i, 0))
    full = lambda a: pl.BlockSpec(a.shape, lambda i: (0,) * a.ndim)
    ws = [cw[k] for k in _CW_KEYS]
    return pl.pallas_call(
        _compress_prompt_body,
        grid=(n_batch,),
        in_specs=[blk, blk] + [full(a) for a in ws],
        out_specs=[pl.BlockSpec((1, n_sub, LANES), lambda i: (i, 0, 0)),
                   pl.BlockSpec((1, LANES, n_sub), lambda i: (i, 0, 0))],
        out_shape=[jax.ShapeDtypeStruct((n_batch, n_sub, LANES), BF16),
                   jax.ShapeDtypeStruct((n_batch, LANES, n_sub), BF16)],
        compiler_params=_cparams("parallel"),
        name="compress_prompt",
    )(kc_rows, vc_rows, *ws)


def _page_copies(pt_ref, n, pools, bufs, sem, slot, n_pages, page):
    for a, (pool, buf) in enumerate(zip(pools, bufs)):
        for p in range(n_pages):
            yield pltpu.make_async_copy(pool.at[pt_ref[n, p]], buf.at[slot, :, pl.ds(p * page, page)],
                                        sem.at[a, slot])


def _fetch_pages(pt_ref, pools, bufs, sem, n_pages, page):
    n = pl.program_id(0)
    slot = n % 2
    copies = functools.partial(_page_copies, pt_ref, pools=pools, bufs=bufs, sem=sem, n_pages=n_pages, page=page)

    @pl.when(n == 0)
    def _():
        for c in copies(n=n, slot=slot):
            c.start()

    @pl.when(n + 1 < pl.num_programs(0))
    def _():
        for c in copies(n=n + 1, slot=1 - slot):
            c.start()

    for c in copies(n=n, slot=slot):
        c.wait()
    return slot


def _compress_sample_body(pt_ref, kpool_ref, vpool_ref, *refs, n_pages, page):
    nw = len(_CW_KEYS)
    w = dict(zip(_CW_KEYS, refs[:nw]))
    kc_ref, vct_ref, kbuf, vbuf, sem, xk_scr, xv_scr = refs[nw:]
    slot = _fetch_pages(pt_ref, (kpool_ref, vpool_ref), (kbuf, vbuf), sem, n_pages, page)
    for buf, x_scr in ((kbuf, xk_scr), (vbuf, xv_scr)):
        for p in range(n_pages):
            x_scr[page * p:page * (p + 1), :] = buf[slot, :, page * p:page * (p + 1)].T
    _compress_rows(xk_scr, xv_scr, w, kc_ref, vct_ref)


def _compress_sample(page_table, kpool, vpool, cw):
    n, n_pages = page_table.shape
    page = kpool.shape[2]
    past = n_pages * page
    n_sub = past // CMP_STRIDE
    any_spec = pl.BlockSpec(memory_space=pl.ANY)
    full = lambda a: pl.BlockSpec(a.shape, lambda i, pt: (0,) * a.ndim)
    ws = [cw[k] for k in _CW_KEYS]
    grid_spec = pltpu.PrefetchScalarGridSpec(
        num_scalar_prefetch=1,
        grid=(n,),
        in_specs=[any_spec, any_spec] + [full(a) for a in ws],
        out_specs=[pl.BlockSpec((1, n_sub, LANES), lambda i, pt: (i, 0, 0)),
                   pl.BlockSpec((1, LANES, n_sub), lambda i, pt: (i, 0, 0))],
        scratch_shapes=[pltpu.VMEM((2, LANES, past), F32), pltpu.VMEM((2, LANES, past), F32),
                        pltpu.SemaphoreType.DMA((2, 2)),
                        pltpu.VMEM((past, LANES), F32), pltpu.VMEM((past, LANES), F32)],
    )
    return pl.pallas_call(
        functools.partial(_compress_sample_body, n_pages=n_pages, page=page),
        grid_spec=grid_spec,
        out_shape=[jax.ShapeDtypeStruct((n, n_sub, LANES), BF16), jax.ShapeDtypeStruct((n, LANES, n_sub), BF16)],
        compiler_params=_cparams("arbitrary"),
        name="compress_sample",
    )(page_table, kpool, vpool, *ws)


_PS_PAD = 16


def _importance(ps_scr, n_sel_pad):
    imp = ps_scr[pl.ds(_PS_PAD - 1, n_sel_pad, stride=4), :]
    for j in range(1, 5):
        imp = imp + ps_scr[pl.ds(_PS_PAD - 1 + j, n_sel_pad, stride=4), :]
    return imp


def _select_blocks(imp, t, n_top):
    s_idx = lax.broadcasted_iota(jnp.int32, imp.shape, 0)
    cur = t >> int(math.log2(SEL_BLOCK))
    valid = s_idx * SEL_BLOCK <= t
    forced = (s_idx == 0) | (s_idx == cur) | (s_idx == cur - 1)
    score = jnp.where(valid, imp + jnp.where(forced, FORCE_BONUS, 0.0), NEG_INF)
    s_f = s_idx.astype(F32)
    sel = jnp.zeros(imp.shape, F32)
    for _ in range(n_top):
        mx = jnp.max(score, axis=0, keepdims=True)
        first = jnp.min(jnp.where(score == mx, s_f, 1e9), axis=0, keepdims=True)
        hit = s_f == first
        sel = jnp.where(hit & (mx > 0.5 * NEG_INF), 1.0, sel)
        score = jnp.where(hit, KNOCKED_OUT, score)
    return sel


def _cmpsel_prompt_body(qt_ref, kc_ref, vct_ref, band_ref, oc_ref, sel_ref, s_scr, ps_scr, o_scr, *, n_sub,
                        n_sel_pad, n_top):
    i = pl.program_id(1)
    t0 = i * LANES
    band_rows = band_ref.shape[1]
    base = pl.multiple_of(SUBLANES * i, SUBLANES)
    s_scr[:, 0:_PS_PAD, :] = jnp.zeros((NSA_HEADS, _PS_PAD, LANES), F32)
    ps_scr[...] = jnp.zeros(ps_scr.shape, F32)

    def attend(rows):
        n_chunk = rows // LANES
        c_idx = lax.broadcasted_iota(jnp.int32, (LANES, LANES), 0)
        q_idx = lax.broadcasted_iota(jnp.int32, (LANES, LANES), 1)
        d0 = t0 + q_idx - CMP_STRIDE * c_idx - (CMP_LEN - 1)
        valid = [d0 - CMP_STRIDE * LANES * c >= 0 for c in range(n_chunk)]
        chunk = lambda c: slice(_PS_PAD + LANES * c, _PS_PAD + LANES * (c + 1))
        n_slab = LANES // SUBLANES

        def slabs(x):
            return [x[SUBLANES * i:SUBLANES * (i + 1)] for i in range(n_slab)]

        def allreduce(x, op):
            for shift in (4, 2, 1):
                x = op(x, pltpu.roll(x, shift, 0))
            return x

        for h in range(NSA_HEADS):
            for c in range(n_chunk):
                s_scr[h, chunk(c), :] = _dot(kc_ref[0, LANES * c:LANES * (c + 1), :],
                                             qt_ref[LANES * h:LANES * (h + 1), :])
            s_scr[h, pl.ds(base, band_rows), :] = s_scr[h, pl.ds(base, band_rows), :] + band_ref[h]
        for g in range(NSA_KV_HEADS):
            stats = []
            for hh in range(HPG):
                h = HPG * g + hh
                m = functools.reduce(jnp.maximum, [x for c in range(n_chunk)
                                                   for x in slabs(jnp.where(valid[c], s_scr[h, chunk(c), :], NEG_INF))])
                m = allreduce(m, jnp.maximum)
                l = functools.reduce(jnp.add, [x for c in range(n_chunk) for x in slabs(
                    jnp.where(valid[c], jnp.exp2(s_scr[h, chunk(c), :] - jnp.concatenate([m] * n_slab, axis=0)), 0.0))])
                l = allreduce(l, jnp.add)
                stats.append((jnp.concatenate([m] * n_slab, axis=0),
                              jnp.concatenate([1.0 / jnp.where(l > 0.0, l, 1.0)] * n_slab, axis=0)))
            o_acc = [jnp.zeros((HEAD_DIM, LANES), F32) for _ in range(HPG)]
            for c in range(n_chunk):
                psum = jnp.zeros((LANES, LANES), F32)
                for hh in range(HPG):
                    h = HPG * g + hh
                    m_full, inv_full = stats[hh]
                    p = jnp.where(valid[c], jnp.exp2(s_scr[h, chunk(c), :] - m_full), 0.0) * inv_full
                    psum = psum + p
                    o_acc[hh] = o_acc[hh] + _dot(
                        vct_ref[0, HEAD_DIM * g:HEAD_DIM * (g + 1), LANES * c:LANES * (c + 1)], p.astype(BF16))
                ps_scr[g, chunk(c), :] = psum
            for hh in range(HPG):
                h = HPG * g + hh
                o_scr[HEAD_DIM * h:HEAD_DIM * (h + 1), :] = o_acc[hh]

    blocks_per_case = LANES // SUBLANES
    for case in range(n_sub // LANES):
        @pl.when((i >= blocks_per_case * case) & (i < blocks_per_case * (case + 1)))
        def _():
            attend(LANES * (case + 1))

    t = t0 + lax.broadcasted_iota(jnp.int32, (n_sel_pad, LANES), 1)
    for g in range(NSA_KV_HEADS):
        sel_ref[0, g] = _select_blocks(_importance(ps_scr.at[g], n_sel_pad), t, n_top)
    oc_ref[...] = o_scr[...].T


def _cmpsel_prompt(qvt, kc, vct, band, n_batch, seq, n_sel_pad, n_top):
    n_sub = kc.shape[1]
    nblk = seq // LANES
    full = lambda a: pl.BlockSpec(a.shape, lambda n, i: (0,) * a.ndim)
    return pl.pallas_call(
        functools.partial(_cmpsel_prompt_body, n_sub=n_sub, n_sel_pad=n_sel_pad, n_top=n_top),
        grid=(n_batch, nblk),
        in_specs=[pl.BlockSpec((NSA_HEADS * LANES, LANES), lambda n, i: (0, n * nblk + i)),
                  pl.BlockSpec((1, n_sub, LANES), lambda n, i: (n, 0, 0)),
                  pl.BlockSpec((1, LANES, n_sub), lambda n, i: (n, 0, 0)),
                  full(band)],
        out_specs=[pl.BlockSpec((LANES, NSA_HEADS * HEAD_DIM), lambda n, i: (n * nblk + i, 0)),
                   pl.BlockSpec((1, NSA_KV_HEADS, n_sel_pad, LANES), lambda n, i: (n, 0, 0, i))],
        out_shape=[jax.ShapeDtypeStruct((n_batch * seq, NSA_HEADS * HEAD_DIM), F32),
                   jax.ShapeDtypeStruct((n_batch, NSA_KV_HEADS, n_sel_pad, seq), F32)],
        scratch_shapes=[pltpu.VMEM((NSA_HEADS, _PS_PAD + n_sub, LANES), F32),
                        pltpu.VMEM((NSA_KV_HEADS, _PS_PAD + 4 * n_sel_pad, LANES), F32),
                        pltpu.VMEM((NSA_HEADS * HEAD_DIM, LANES), F32)],
        compiler_params=_cparams("parallel", "parallel"),
        name="cmpsel_prompt",
    )(qvt, kc, vct, band)


def _cmpsel_sample_body(q_ref, kc_ref, vct_ref, bias_ref, rmat_ref, oct_ref, sel_ref, ps_scr, *, n_sub, n_sel_pad,
                        n_top, past, dec_seq):
    s = _dot_nt(kc_ref[0], q_ref[0]) + bias_ref[...]
    m = jnp.max(s, axis=0, keepdims=True)
    e = jnp.exp2(s - m)
    p = e * (1.0 / jnp.sum(e, axis=0, keepdims=True))
    oct_ref[0] = _dot(vct_ref[0], p.astype(BF16))
    hi = p.astype(BF16)
    lo = (p - hi.astype(F32)).astype(BF16)
    psum = _dot(hi, rmat_ref[...]) + _dot(lo, rmat_ref[...])
    ps_scr[...] = jnp.zeros(ps_scr.shape, F32)
    ps_scr[_PS_PAD:_PS_PAD + n_sub, :] = psum
    col = lax.broadcasted_iota(jnp.int32, (n_sel_pad, LANES), 1)
    t = past + (col & (dec_seq - 1))
    sel = _select_blocks(_importance(ps_scr, n_sel_pad), t, n_top)
    sel_ref[0] = sel.T.astype(BF16)


def _cmpsel_sample(qs, kc, vct, bias_t, rmat, n_sel_pad, n_top, past, dec_seq):
    n, n_sub, _ = kc.shape
    full = lambda a: pl.BlockSpec(a.shape, lambda i: (0,) * a.ndim)
    return pl.pallas_call(
        functools.partial(_cmpsel_sample_body, n_sub=n_sub, n_sel_pad=n_sel_pad, n_top=n_top, past=past,
                          dec_seq=dec_seq),
        grid=(n,),
        in_specs=[pl.BlockSpec((1, LANES, LANES), lambda i: (i, 0, 0)),
                  pl.BlockSpec((1, n_sub, LANES), lambda i: (i, 0, 0)),
                  pl.BlockSpec((1, LANES, n_sub), lambda i: (i, 0, 0)),
                  full(bias_t), full(rmat)],
        out_specs=[pl.BlockSpec((1, LANES, LANES), lambda i: (i, 0, 0)),
                   pl.BlockSpec((1, LANES, n_sel_pad), lambda i: (i, 0, 0))],
        out_shape=[jax.ShapeDtypeStruct((n, LANES, LANES), F32), jax.ShapeDtypeStruct((n, LANES, n_sel_pad), BF16)],
        scratch_shapes=[pltpu.VMEM((_PS_PAD + 4 * n_sel_pad, LANES), F32)],
        compiler_params=_cparams("parallel"),
        name="cmpsel_sample",
    )(qs, kc, vct, bias_t, rmat)


FAR_UNROLL = 4


def _flash_t_update(h, k, vt, qt, bias_ref, mask_rows, m_ref, l_ref, acc_ref):
    s = _dot(k, qt)
    m = m_ref[h]
    l = l_ref[h]
    acc = acc_ref[h]
    n_slab = SEL_BLOCK // SUBLANES
    for c in range(k.shape[0] // SEL_BLOCK):
        rows = slice(SEL_BLOCK * c, SEL_BLOCK * (c + 1))
        sc = s[rows]
        if bias_ref is not None:
            sc = sc + bias_ref[rows, :]
        if mask_rows is not None:
            sc = jnp.where(mask_rows[c:c + 1, :] > 0.5, sc, NEG_INF)
        slabs = [sc[SUBLANES * i:SUBLANES * (i + 1)] for i in range(n_slab)]
        cm = functools.reduce(jnp.maximum, slabs)
        for shift in (4, 2, 1):
            cm = jnp.maximum(cm, pltpu.roll(cm, shift, 0))
        m_new = jnp.maximum(m, cm)
        alpha = jnp.exp2(m - m_new)
        ps = [jnp.exp2(x - m_new) for x in slabs]
        l = alpha * l + functools.reduce(jnp.add, ps)
        pv = _dot(vt[:, rows], jnp.concatenate(ps, axis=0).astype(BF16))
        acc = jnp.concatenate([alpha] * (acc.shape[0] // SUBLANES), axis=0) * acc + pv
        m = m_new
    m_ref[h] = m
    l_ref[h] = l
    acc_ref[h] = acc


def _flash_prompt_body(qt_ref, ks_ref, vst_ref, kw_ref, vwt_ref, sel_ref, near_ref, winfar_ref, osel_ref, owin_ref,
                       ms, ls, accs, mw, lw, accw, *, tq):
    qi = pl.program_id(1)
    for m_r, l_r, acc_r in ((ms, ls, accs), (mw, lw, accw)):
        m_r[...] = jnp.full(m_r.shape, M_INIT, F32)
        l_r[...] = jnp.zeros(l_r.shape, F32)
        acc_r[...] = jnp.zeros(acc_r.shape, F32)
    blocks = tq // SEL_BLOCK

    def aligned(x, m):
        return x if isinstance(x, int) else pl.multiple_of(x, m)

    def sel_tile(kt, near, parity=None):
        k0 = aligned(kt * tq, tq)
        k = ks_ref[pl.ds(k0, tq), :]
        r0 = aligned((kt // 2) * SUBLANES, SUBLANES)
        for g in range(NSA_KV_HEADS):
            r8 = sel_ref[0, g, pl.ds(r0, SUBLANES), :]
            if parity is None:
                r4 = jnp.where((kt & 1) == 1, r8[blocks:2 * blocks], r8[0:blocks])
            else:
                r4 = r8[blocks * parity:blocks * (parity + 1)]
            vt = vst_ref[HEAD_DIM * g:HEAD_DIM * (g + 1), pl.ds(k0, tq)]
            for hh in range(HPG):
                h = HPG * g + hh
                bias = None if near is None else near_ref.at[near, h]
                _flash_t_update(h, k, vt, qt_ref[LANES * h:LANES * (h + 1), :], bias, r4, ms, ls, accs)

    def win_tile(kt, near):
        k0 = aligned(kt * tq, tq)
        k = kw_ref[pl.ds(k0, tq), :]
        for g in range(NSA_KV_HEADS):
            vt = vwt_ref[HEAD_DIM * g:HEAD_DIM * (g + 1), pl.ds(k0, tq)]
            for hh in range(HPG):
                h = HPG * g + hh
                bias = winfar_ref if near is None else near_ref.at[near, h]
                _flash_t_update(h, k, vt, qt_ref[LANES * h:LANES * (h + 1), :], bias, None, mw, lw, accw)

    n_far = jnp.maximum(qi - 1, 0)
    n_group = n_far // FAR_UNROLL

    def far_body(j, carry):
        for u in range(FAR_UNROLL):
            sel_tile(FAR_UNROLL * j + u, None, u % 2)
        return carry

    def far_rest(kt, carry):
        sel_tile(kt, None)
        return carry

    lax.fori_loop(0, n_group, far_body, 0)
    lax.fori_loop(FAR_UNROLL * n_group, n_far, far_rest, 0)

    @pl.when(qi >= 2)
    def _():
        sel_tile(qi - 1, 0)
        win_tile(qi - 2, None)
        win_tile(qi - 1, 0)
        sel_tile(qi, 1)
        win_tile(qi, 1)

    @pl.when(qi == 1)
    def _():
        sel_tile(0, 0, 0)
        win_tile(0, 0)
        sel_tile(1, 1, 1)
        win_tile(1, 1)

    @pl.when(qi == 0)
    def _():
        sel_tile(0, 1, 0)
        win_tile(0, 1)

    for out_ref, l_r, acc_r in ((osel_ref, ls, accs), (owin_ref, lw, accw)):
        o_t = jnp.concatenate([acc_r[h] * (1.0 / jnp.sum(l_r[h], axis=0, keepdims=True))
                               for h in range(NSA_HEADS)], axis=0)
        out_ref[...] = o_t.T


def _flash_prompt(kvb, qvt, sel_t, near, winfar, n_batch, seq, tq):
    nq = seq // tq
    assert WINDOW == 2 * tq and tq == 4 * SEL_BLOCK, "window = two key tiles; a key tile = 4 selection blocks"
    n_sel_pad = sel_t.shape[2]
    k_spec = lambda col: pl.BlockSpec((seq, LANES), lambda n, i: (n, col))
    vt_spec = lambda row: pl.BlockSpec((LANES, seq), lambda n, i: (row, n))
    full = lambda a: pl.BlockSpec(a.shape, lambda n, i: (0,) * a.ndim)
    out_spec = pl.BlockSpec((tq, NSA_HEADS * HEAD_DIM), lambda n, i: (n * nq + i, 0))
    state = [pltpu.VMEM((NSA_HEADS, SUBLANES, tq), F32), pltpu.VMEM((NSA_HEADS, SUBLANES, tq), F32),
             pltpu.VMEM((NSA_HEADS, HEAD_DIM, tq), F32)]
    out = jax.ShapeDtypeStruct((n_batch * seq, NSA_HEADS * HEAD_DIM), F32)
    return pl.pallas_call(
        functools.partial(_flash_prompt_body, tq=tq),
        grid=(n_batch, nq),
        in_specs=[pl.BlockSpec((NSA_HEADS * LANES, tq), lambda n, i: (0, n * nq + i)),
                  k_spec(2), vt_spec(NSA_HEADS), k_spec(4), vt_spec(NSA_HEADS + 1),
                  pl.BlockSpec((1, NSA_KV_HEADS, n_sel_pad, tq), lambda n, i: (n, 0, 0, i)),
                  full(near), full(winfar)],
        out_specs=[out_spec, out_spec],
        out_shape=[out, out],
        scratch_shapes=state + state,
        compiler_params=_cparams("parallel", "parallel"),
        name="flash_prompt",
    )(qvt, kvb, qvt, kvb, qvt, sel_t, near, winfar)


def _flash_sample_body(pt_ref, q_ref, kpool_ref, vpool_ref, ksn_ref, vsn_ref, sel_ref, e_ref, tailb_ref, newb_ref,
                       kwt_ref, vwt_ref, winb_ref, osel_ref, owin_ref, kbuf, vbuf, sem, s_scr, *, tk, n_pages, page):
    slot = _fetch_pages(pt_ref, (kpool_ref, vpool_ref), (kbuf, vbuf), sem, n_pages, page)
    q = q_ref[0]
    nt = n_pages * page // tk
    s_new = _dot_nt(q, ksn_ref[0]) + newb_ref[...]

    def score_tile(kt, tail):
        k0 = pl.multiple_of(kt * tk, tk)
        mask = _dot(sel_ref[0], e_ref[:, pl.ds(k0, tk)]) > 0.5
        s = _dot(q, kbuf[slot, :, pl.ds(k0, tk)].astype(BF16))
        if tail:
            s = s + tailb_ref[...]
        s = jnp.where(mask, s, NEG_INF)
        s_scr[:, pl.ds(k0, tk)] = s
        return jnp.max(s, axis=1, keepdims=True)

    m = lax.fori_loop(0, nt - 1, lambda kt, m: jnp.maximum(m, score_tile(kt, False)),
                      jnp.max(s_new, axis=1, keepdims=True), unroll=3)
    m = jnp.maximum(m, score_tile(nt - 1, True))

    def pv_tile(kt, carry):
        l, acc = carry
        k0 = pl.multiple_of(kt * tk, tk)
        p = jnp.exp2(s_scr[:, pl.ds(k0, tk)] - m)
        return (l + jnp.sum(p, axis=1, keepdims=True),
                acc + _dot_nt(p.astype(BF16), vbuf[slot, :, pl.ds(k0, tk)].astype(BF16)))

    p_new = jnp.exp2(s_new - m)
    l, acc = lax.fori_loop(0, nt, pv_tile, (jnp.sum(p_new, axis=1, keepdims=True),
                                            _dot(p_new.astype(BF16), vsn_ref[0])), unroll=4)
    osel_ref[0] = acc * (1.0 / l)

    s_w = _dot(q, kwt_ref[0].astype(BF16)) + winb_ref[...]
    s_new = _dot_nt(q, ksn_ref[1]) + newb_ref[...]
    m = jnp.maximum(jnp.max(s_w, axis=1, keepdims=True), jnp.max(s_new, axis=1, keepdims=True))
    p_w = jnp.exp2(s_w - m)
    p_new = jnp.exp2(s_new - m)
    l = jnp.sum(p_w, axis=1, keepdims=True) + jnp.sum(p_new, axis=1, keepdims=True)
    acc = _dot_nt(p_w.astype(BF16), vwt_ref[0].astype(BF16)) + _dot(p_new.astype(BF16), vsn_ref[1])
    owin_ref[0] = acc * (1.0 / l)


def _flash_sample(page_table, qs, kpool, vpool, kvn, selrows, e_all, tailb, newb, kw, vw, winb, tk):
    n, n_pages = page_table.shape
    page = kpool.shape[2]
    assert (n_pages * page // tk - 1) % 3 == 0 and (n_pages * page // tk) % 4 == 0, "unroll factors of the key loops"
    past = n_pages * page
    per_n = lambda a: pl.BlockSpec((1,) + a.shape[1:], lambda i, pt: (i,) + (0,) * (a.ndim - 1))
    full = lambda a: pl.BlockSpec(a.shape, lambda i, pt: (0,) * a.ndim)
    any_spec = pl.BlockSpec(memory_space=pl.ANY)
    new_spec = pl.BlockSpec((2, LANES, LANES), lambda i, pt: (i, 0, 0))
    out = jax.ShapeDtypeStruct((n, LANES, LANES), F32)
    grid_spec = pltpu.PrefetchScalarGridSpec(
        num_scalar_prefetch=1,
        grid=(n,),
        in_specs=[per_n(qs), any_spec, any_spec, new_spec, new_spec, per_n(selrows), full(e_all), full(tailb),
                  full(newb), per_n(kw), per_n(vw), full(winb)],
        out_specs=[per_n(out), per_n(out)],
        scratch_shapes=[pltpu.VMEM((2, LANES, past), F32), pltpu.VMEM((2, LANES, past), F32),
                        pltpu.SemaphoreType.DMA((2, 2)), pltpu.VMEM((LANES, past), F32)],
    )
    return pl.pallas_call(
        functools.partial(_flash_sample_body, tk=tk, n_pages=n_pages, page=page),
        grid_spec=grid_spec,
        out_shape=[out, out],
        compiler_params=_cparams("arbitrary"),
        name="flash_sample",
    )(page_table, qs, kpool, vpool, kvn[0], kvn[1], selrows, e_all, tailb, newb, kw, vw, winb)


def _mixout_body(an_ref, oc_ref, os_ref, ow_ref, gate_ref, x_ref, gb_ref, woa_ref, wob_ref, y_ref):
    gates = gate_ref[...]
    tm = gates.shape[0]
    lane = lax.broadcasted_iota(jnp.int32, (tm, LANES), 1)
    tiles = []
    for j in range(NSA_HEADS // 2):
        cols = slice(LANES * j, LANES * (j + 1))
        acc = None
        for br, o_ref in enumerate((oc_ref, os_ref, ow_ref)):
            c0 = 3 * (2 * j) + br
            c1 = 3 * (2 * j + 1) + br
            gcol = jnp.where(lane < HEAD_DIM, gates[:, c0:c0 + 1], gates[:, c1:c1 + 1])
            term = gcol * o_ref[:, cols]
            acc = term if acc is None else acc + term
        tiles.append(acc)
    bn = _rms(jnp.concatenate(tiles, axis=1), gb_ref[...]).astype(BF16)
    y_ref[...] = x_ref[...] + _dot(an_ref[...].astype(BF16), woa_ref[...]) + _dot(bn, wob_ref[...])


def _mixout(an, oc, osel, owin, gates, x, g_b, wo_a, wo_b, tm):
    m, d = x.shape
    row = lambda a: pl.BlockSpec((tm, a.shape[1]), lambda i: (i, 0))
    full = lambda a: pl.BlockSpec(a.shape, lambda i: (0,) * a.ndim)
    return pl.pallas_call(
        _mixout_body,
        grid=(m // tm,),
        in_specs=[row(an), row(oc), row(osel), row(owin), row(gates), row(x), full(g_b), full(wo_a), full(wo_b)],
        out_specs=row(x),
        out_shape=jax.ShapeDtypeStruct((m, d), F32),
        compiler_params=_cparams("parallel"),
        name="mix_out",
    )(an, oc, osel, owin, gates, x, g_b, wo_a, wo_b)


def _normmm_body(x_ref, g_ref, w_ref, o_ref):
    o_ref[...] = _dot(_rms(x_ref[...], g_ref[...]).astype(BF16), w_ref[...]).astype(o_ref.dtype)


def _norm_matmul(x, g, w, out_dtype, tm):
    m, d = x.shape
    n = w.shape[1]
    return pl.pallas_call(
        _normmm_body,
        grid=(m // tm,),
        in_specs=[pl.BlockSpec((tm, d), lambda i: (i, 0)), pl.BlockSpec(g.shape, lambda i: (0, 0)),
                  pl.BlockSpec(w.shape, lambda i: (0, 0))],
        out_specs=pl.BlockSpec((tm, n), lambda i: (i, 0)),
        out_shape=jax.ShapeDtypeStruct((m, n), out_dtype),
        compiler_params=_cparams("parallel"),
        name="norm_matmul",
    )(x, g, w)


def _mmres_body(a_ref, w_ref, r_ref, o_ref):
    o_ref[...] = r_ref[...] + _dot(a_ref[...].astype(BF16), w_ref[...])


def _matmul_residual(a, w, res, tm):
    m, k = a.shape
    n = w.shape[1]
    return pl.pallas_call(
        _mmres_body,
        grid=(m // tm,),
        in_specs=[pl.BlockSpec((tm, k), lambda i: (i, 0)), pl.BlockSpec(w.shape, lambda i: (0, 0)),
                  pl.BlockSpec((tm, n), lambda i: (i, 0))],
        out_specs=pl.BlockSpec((tm, n), lambda i: (i, 0)),
        out_shape=jax.ShapeDtypeStruct((m, n), F32),
        compiler_params=_cparams("parallel"),
        name="matmul_residual",
    )(a, w, res)


def _xattn_body(q_ref, mk_ref, mv_ref, o_ref, *, n_heads):
    hd = q_ref.shape[2] // n_heads
    scale = hd ** -0.5
    for h in range(n_heads):
        cols = slice(hd * h, hd * (h + 1))
        s = _dot_nt(q_ref[0, :, cols], mk_ref[0, :, cols].astype(BF16)) * scale
        e = jnp.exp(s - jnp.max(s, axis=1, keepdims=True))
        p = e * (1.0 / jnp.sum(e, axis=1, keepdims=True))
        o_ref[0, :, cols] = _dot(p.astype(BF16), mv_ref[0, :, cols].astype(BF16))


def _xattn_core(q, mk, mv, tm):
    n, t, d = q.shape
    return pl.pallas_call(
        functools.partial(_xattn_body, n_heads=XA_HEADS),
        grid=(n, t // tm),
        in_specs=[pl.BlockSpec((1, tm, d), lambda b, i: (b, i, 0)),
                  pl.BlockSpec((1,) + mk.shape[1:], lambda b, i: (b, 0, 0)),
                  pl.BlockSpec((1,) + mv.shape[1:], lambda b, i: (b, 0, 0))],
        out_specs=pl.BlockSpec((1, tm, d), lambda b, i: (b, i, 0)),
        out_shape=jax.ShapeDtypeStruct((n, t, d), F32),
        compiler_params=_cparams("parallel", "parallel"),
        name="xattn_core",
    )(q, mk, mv)


def _ffn_body(x_ref, g_ref, wu_ref, wg_ref, cw_ref, cb_ref, wd_ref, gf_ref, s1_ref, s2_ref, y_ref, a_ref, h_scr,
              acc_scr, carry_scr, *, tf, dec_seq):
    i = pl.program_id(1)
    j = pl.program_id(2)
    tm = x_ref.shape[0]

    @pl.when(j == 0)
    def _():
        h_scr[...] = _rms(x_ref[...], g_ref[...]).astype(BF16)
        acc_scr[...] = jnp.zeros(acc_scr.shape, F32)

    a = _dot(h_scr[...], wu_ref[...])
    gt = _dot(h_scr[...], wg_ref[...])
    row = lax.broadcasted_iota(jnp.int32, (tm, tf), 0)
    r1 = pltpu.roll(a, 1, 0)
    r2 = pltpu.roll(a, 2, 0)
    if dec_seq is None:
        cols = pl.ds(pl.multiple_of(j * tf, LANES), tf)

        @pl.when(i == 0)
        def _():
            carry_scr[:, cols] = jnp.zeros((SUBLANES, tf), F32)

        p0 = carry_scr[SUBLANES - 2:SUBLANES - 1, cols]
        p1 = carry_scr[SUBLANES - 1:SUBLANES, cols]
        a1 = jnp.where(row == 0, p1, r1)
        a2 = jnp.where(row == 0, p0, jnp.where(row == 1, p1, r2))
        carry_scr[:, cols] = a[tm - SUBLANES:tm, :]
        a_ref[0] = a[tm - SUBLANES:tm, :]
    else:
        t = row & (dec_seq - 1)
        a1 = jnp.where(t == 0, s1_ref[...], r1)
        a2 = jnp.where(t < 2, s2_ref[...], r2)
        a_ref[...] = a
    c = cb_ref[...] + a2 * cw_ref[0:1, :] + a1 * cw_ref[1:2, :] + a * cw_ref[2:3, :]
    acc_scr[...] += _dot((_gelu(c) * gt).astype(BF16), wd_ref[...])

    @pl.when(j == pl.num_programs(2) - 1)
    def _():
        y_ref[...] = _rms(x_ref[...] + acc_scr[...], gf_ref[...])


def _ffn(x, g, wu, wg, cw, cb, wd, g_final, s1, s2, n_seq, tm, tf, dec_seq):
    m, d = x.shape
    f = wu.shape[1]
    nt = m // n_seq // tm
    nf = f // tf
    row = pl.BlockSpec((tm, d), lambda n, i, j: (n * nt + i, 0))
    vec = lambda a: pl.BlockSpec(a.shape, lambda n, i, j: (0,) * a.ndim)
    fcol = lambda a: pl.BlockSpec((a.shape[0], tf), lambda n, i, j: (0, j))
    if dec_seq is None:
        st_spec = pl.BlockSpec((SUBLANES, tf), lambda n, i, j: (0, j))
        a_spec = pl.BlockSpec((1, SUBLANES, tf), lambda n, i, j: (n * nt + i, 0, j))
        a_shape = jax.ShapeDtypeStruct((n_seq * nt, SUBLANES, f), F32)
    else:
        st_spec = pl.BlockSpec((tm, tf), lambda n, i, j: (n * nt + i, j))
        a_spec = st_spec
        a_shape = jax.ShapeDtypeStruct((m, f), F32)
    return pl.pallas_call(
        functools.partial(_ffn_body, tf=tf, dec_seq=dec_seq),
        grid=(n_seq, nt, nf),
        in_specs=[row, vec(g), fcol(wu), fcol(wg), fcol(cw), fcol(cb),
                  pl.BlockSpec((tf, d), lambda n, i, j: (j, 0)), vec(g_final), st_spec, st_spec],
        out_specs=[row, a_spec],
        out_shape=[jax.ShapeDtypeStruct((m, d), F32), a_shape],
        scratch_shapes=[pltpu.VMEM((tm, d), BF16), pltpu.VMEM((tm, d), F32), pltpu.VMEM((SUBLANES, f), F32)],
        compiler_params=_cparams("parallel", "arbitrary", "arbitrary"),
        name="conv_ffn",
    )(x, g, wu, wg, cw, cb, wd, g_final, s1, s2)


def _t5_bucket_np(dist):
    n = np.maximum(dist, 0)
    max_exact = REL_BUCKETS // 2
    nf = np.maximum(n, 1).astype(np.float64)
    large = max_exact + (np.log(nf / max_exact) / math.log(REL_MAX_DIST / max_exact)
                         * (REL_BUCKETS - max_exact)).astype(np.int32)
    large = np.minimum(large, REL_BUCKETS - 1)
    return np.where(n < max_exact, n, large).astype(np.int32)


def _bias_vectors(rel_bias, dist, live, masked):
    dist, live, masked = np.broadcast_arrays(dist, live, masked)
    onehot = np.zeros(dist.shape + (REL_BUCKETS,), np.float32)
    np.put_along_axis(onehot, _t5_bucket_np(dist)[..., None], 1.0, axis=-1)
    onehot[..., REL_BUCKETS - 1] -= 1.0
    onehot *= live[..., None]
    add = np.where(masked, NEG_INF, 0.0).astype(np.float32)
    table = jnp.einsum("...b,bh->h...", jnp.asarray(onehot), rel_bias, precision=lax.Precision.HIGHEST)
    return table * LOG2E + add


def _toeplitz(v, rows, cols, stride):
    length = v.shape[-1]
    w = length - stride
    assert cols <= w
    t = jnp.tile(v, (1,) * (v.ndim - 1) + (rows,))[..., :rows * w]
    return t.reshape(v.shape[:-1] + (rows, w))[..., :cols]


def _bias_tables_prompt(rel_bias, tq):
    k = np.arange(2 * tq)
    upper = k < tq
    diag = _bias_vectors(rel_bias, k, upper, ~upper)
    prev = _bias_vectors(rel_bias, np.where(upper, k + tq, k - tq), True, False)
    near = jnp.stack([_toeplitz(prev, tq, tq, 1), _toeplitz(diag, tq, tq, 1)])
    winfar = _toeplitz(jnp.asarray(np.where(upper, NEG_INF, 0.0), F32), tq, tq, 1)
    length = 4 * LANES
    off = 2 * SUBLANES * CMP_STRIDE - (CMP_LEN - 1)
    kk = np.arange(length)
    d = np.where(kk < LANES, kk + off, kk - length + off)
    band = _toeplitz(_bias_vectors(rel_bias, d, (d >= 0) & ((kk < LANES) | (kk >= LANES + CMP_STRIDE)), False),
                     3 * SUBLANES, LANES, CMP_STRIDE)
    return near, winfar, band


def _bias_tables_sample(rel_bias, past, dec_seq, n_cmp, n_sub, tk, win_buf):
    t = np.arange(dec_seq)[:, None]

    def rows(dist, live, masked):
        tab = _bias_vectors(rel_bias, dist, live, masked)
        tab = tab.reshape(NSA_HEADS * dec_seq, -1)
        return jnp.pad(tab, ((0, LANES - NSA_HEADS * dec_seq), (0, 0)))

    c = np.arange(n_sub)[None, :]
    d_cmp = past + t - (c * CMP_STRIDE + CMP_LEN - 1)
    ok = (d_cmp >= 0) & (c < n_cmp)
    cmp_t = rows(d_cmp, ok, ~ok).T
    tail = rows(tk + t - np.arange(tk)[None, :], True, False)
    jn = np.arange(LANES)[None, :]
    ok = (jn <= t) & (jn < dec_seq)
    newb = rows(t - jn, ok, ~ok)
    d_win = win_buf + t - np.arange(win_buf)[None, :]
    ok = d_win < WINDOW
    winb = rows(d_win, ok, ~ok)
    return cmp_t, tail, newb, winb


def _row_heads(dec_seq):
    rows = np.arange(LANES)
    live = rows < NSA_HEADS * dec_seq
    return np.where(live, rows // dec_seq, 0), rows % dec_seq, live


def _prep_in_proj(w_in, d_a):
    d_b = NSA_HEADS * HEAD_DIM
    o_q = 2 * d_a
    o_kv = o_q + d_b
    o_g = o_kv + 6 * NSA_KV_HEADS * HEAD_DIM
    wq = w_in[:, o_q:o_kv].reshape(-1, NSA_HEADS, HEAD_DIM) * (ATT_SCALE * LOG2E)
    slot = jnp.zeros((w_in.shape[0], NSA_HEADS, NSA_KV_HEADS, HEAD_DIM), w_in.dtype)
    for h in range(NSA_HEADS):
        slot = slot.at[:, h, h // HPG].set(wq[:, h])
    wg = jnp.pad(w_in[:, o_g:], ((0, 0), (0, LANES - (w_in.shape[1] - o_g))))
    w_pad = jnp.concatenate([w_in[:, :o_q], slot.reshape(w_in.shape[0], -1), w_in[:, o_kv:o_g], wg], axis=1)
    kvw = NSA_KV_HEADS * HEAD_DIM
    w_vt = jnp.concatenate([slot.reshape(w_in.shape[0], -1), w_in[:, o_kv + 3 * kvw:o_kv + 4 * kvw],
                            w_in[:, o_kv + 5 * kvw:o_kv + 6 * kvw]], axis=1).T
    return w_pad.astype(BF16), w_vt.astype(BF16)


def _prep_compress(w1, b1, w2, b2, pe):
    eye = jnp.eye(NSA_KV_HEADS, dtype=w1.dtype)
    hid = w1.shape[-1]
    kvw = NSA_KV_HEADS * HEAD_DIM
    w1p = jnp.einsum("apdh,gk->pgdakh", w1.reshape(2, CMP_STRIDE, HEAD_DIM, hid), eye)
    w1p = w1p.reshape(CMP_STRIDE // 2, 2 * kvw, 2 * NSA_KV_HEADS * hid)
    pe_rows = jnp.broadcast_to(pe.reshape(2, CMP_STRIDE, 1, HEAD_DIM), (2, CMP_STRIDE, NSA_KV_HEADS, HEAD_DIM))
    pe_rows = pe_rows.reshape(2, CMP_STRIDE // 2, 2 * kvw).transpose(1, 0, 2)
    pe_rows = jnp.pad(pe_rows, ((0, 0), (0, 2 * SUBLANES - 2), (0, 0)))
    w2big = jnp.einsum("hd,gk->ghkd", w2, eye).reshape(NSA_KV_HEADS * hid, kvw)
    return (w1p.astype(BF16), pe_rows.astype(BF16), jnp.tile(b1, NSA_KV_HEADS)[None, :], w2big.astype(BF16),
            jnp.tile(b2, NSA_KV_HEADS)[None, :])


def _prep_compress_kv(p):
    w1k, pek, b1k, w2k, b2k = _prep_compress(p["w_cmp1_k"], p["b_cmp1_k"], p["w_cmp2_k"], p["b_cmp2_k"], p["pe_cmp_k"])
    w1v, pev, b1v, w2v, b2v = _prep_compress(p["w_cmp1_v"], p["b_cmp1_v"], p["w_cmp2_v"], p["b_cmp2_v"], p["pe_cmp_v"])
    return dict(w1k=w1k, w1v=w1v, pek=pek, pev=pev, b1k=b1k, b1v=b1v, w2k=w2k, w2vt=w2v.T, b2k=b2k, b2vt=b2v.T)


def _block_expand(n_sel_pad, n_keys):
    s = np.arange(n_sel_pad)[:, None]
    k = np.arange(n_keys)[None, :]
    return jnp.asarray((k // SEL_BLOCK == s), dtype=BF16)


def _round_up(x, m):
    return -(-x // m) * m


def _shared_front(x2d, p, w_in_pad, w_vt, period, mix_w, mix_b, tm):
    d_a = p["g_a"].shape[0]
    u, v, qpad, kc, vc, ks, vs, kw, vw, kvb, vt, gates = _in_proj(x2d, p["g_mix"][None, :], w_in_pad, w_vt, d_a, tm)
    an, v_rows = _gmlp(u, v, p["ln_v_g"][None, :], p["ln_v_b"][None, :], mix_w, mix_b, p["g_a"][None, :], period, tm)
    return an, v_rows, qpad, (kc, vc, ks, vs, kw, vw), kvb, vt, gates


def _shared_back(x2d, an, oc, osel, owin, gates, p, wts, mk, mv, n_seq, s1, s2, g_final, dec_seq, tm):
    d = x2d.shape[1]
    x1 = _mixout(an, oc, osel, owin, gates, x2d, p["g_b"][None, :], wts["wo_a"], wts["wo_b"], tm)
    qx = _norm_matmul(x1, p["g_xa"][None, :], wts["w_xq"], BF16, tm)
    t = x1.shape[0] // n_seq
    if dec_seq is None:
        ox = _xattn_core(qx.reshape(n_seq, t, d), mk, mv, tm).reshape(-1, d)
    else:
        qx3 = jnp.pad(qx.reshape(n_seq, t, d), ((0, 0), (0, SUBLANES - t), (0, 0)))
        ox = _xattn_core(qx3, mk, mv, SUBLANES)[:, :t].reshape(-1, d)
    x2 = _matmul_residual(ox, wts["w_xo"], x1, tm)
    n_ffn_seq = n_seq if dec_seq is None else 1
    return _ffn(x2, p["g_ffn"][None, :], wts["w_up"], wts["w_gate"], wts["conv_w"], p["conv_b"][None, :],
                wts["w_down"], g_final[None, :], s1, s2, n_ffn_seq, tm, wts["tf"], dec_seq)


def _heads_to_rows(qpad, n, t):
    q = qpad.reshape(n, t, NSA_HEADS, LANES).transpose(0, 2, 1, 3).reshape(n, NSA_HEADS * t, LANES)
    return jnp.pad(q, ((0, 0), (0, LANES - NSA_HEADS * t), (0, 0)))


def _rows_to_tokens(o, n, t):
    o = o[:, :NSA_HEADS * t].reshape(n, NSA_KV_HEADS, HPG, t, NSA_KV_HEADS, HEAD_DIM)
    o = jnp.stack([o[:, g, :, :, g] for g in range(NSA_KV_HEADS)], axis=1)
    return o.transpose(0, 3, 1, 2, 4).reshape(n * t, NSA_HEADS * HEAD_DIM)


def kernel(x_prompt, x_sample, mem_prompt, cache_cmp_k, cache_cmp_v, cache_sel_k, cache_sel_v, cache_win_k,
           cache_win_v, cache_mem_k, cache_mem_v, state_conv, page_table, g_mix, w_in, w_s, b_s, ln_v_g, ln_v_b,
           w_cmp1_k, b_cmp1_k, w_cmp2_k, b_cmp2_k, pe_cmp_k, w_cmp1_v, b_cmp1_v, w_cmp2_v, b_cmp2_v, pe_cmp_v,
           rel_bias, g_a, g_b, w_o, g_xa, g_mem, w_xq, w_mk, w_mv, w_xo, g_ffn, w_up, w_gate, conv_w, conv_b,
           w_down, g_final):
    depth = w_in.shape[0]
    assert depth == 1, "the layer loop is written for a single layer"
    bsz, seq, d = x_prompt.shape
    nd, dec_seq, _ = x_sample.shape
    mem_len = mem_prompt.shape[1]
    d_a = g_a.shape[1]
    d_ff = w_up.shape[2]
    n_pages = page_table.shape[1]
    page = cache_cmp_k.shape[2]
    past = n_pages * page
    kvw = NSA_KV_HEADS * HEAD_DIM
    win_buf = cache_win_k.shape[2]
    assert kvw == LANES and seq % 256 == 0 and past % 512 == 0 and dec_seq == 4 and nd * dec_seq % LANES == 0
    tm = 512
    tq = 256
    tk_s = 512
    tf = d_ff // 2 if (d_ff // 2) % LANES == 0 else d_ff

    l = 0
    p = dict(g_mix=g_mix[l], ln_v_g=ln_v_g[l], ln_v_b=ln_v_b[l], w_cmp1_k=w_cmp1_k[l], b_cmp1_k=b_cmp1_k[l],
             w_cmp2_k=w_cmp2_k[l], b_cmp2_k=b_cmp2_k[l], pe_cmp_k=pe_cmp_k[l], w_cmp1_v=w_cmp1_v[l],
             b_cmp1_v=b_cmp1_v[l], w_cmp2_v=w_cmp2_v[l], b_cmp2_v=b_cmp2_v[l], pe_cmp_v=pe_cmp_v[l], g_a=g_a[l],
             g_b=g_b[l], g_xa=g_xa[l], g_ffn=g_ffn[l], conv_b=conv_b[l])
    w_in_pad, w_vt = _prep_in_proj(w_in[l], d_a)
    cw = _prep_compress_kv(p)
    wts = dict(wo_a=w_o[l, :d_a].astype(BF16), wo_b=w_o[l, d_a:].astype(BF16), w_xq=w_xq[l].astype(BF16),
               w_xo=w_xo[l].astype(BF16), w_up=w_up[l].astype(BF16), w_gate=w_gate[l].astype(BF16),
               w_down=w_down[l].astype(BF16), conv_w=jnp.pad(conv_w[l], ((0, SUBLANES - CONV_W), (0, 0))), tf=tf)
    bias_lanes = jnp.repeat(b_s[l].T, d_a // A_GROUPS, axis=1)

    xp = x_prompt.reshape(bsz * seq, d)
    an, _, _, kv6, kvb, qvt, gates = _shared_front(xp, p, w_in_pad, w_vt, CHUNK, w_s[l], bias_lanes, tm)
    kc, vct = _compress_prompt(kv6[0], kv6[1], cw, bsz, seq)
    n_sel = seq // SEL_BLOCK
    n_sel_pad = _round_up(n_sel, LANES)
    near, winfar, band = _bias_tables_prompt(rel_bias, tq)
    oc, sel_t = _cmpsel_prompt(qvt, kc, vct, band, bsz, seq, n_sel_pad, min(SEL_TOP, n_sel))
    osel, owin = _flash_prompt(kvb, qvt, sel_t, near, winfar, bsz, seq, tq)
    mem_kv = _norm_matmul(mem_prompt.reshape(bsz * mem_len, d), g_mem[l][None, :],
                          jnp.concatenate([w_mk[l], w_mv[l]], axis=1).astype(BF16), F32, min(tm, bsz * mem_len))
    mk_p = mem_kv[:, :d].reshape(bsz, mem_len, d)
    mv_p = mem_kv[:, d:].reshape(bsz, mem_len, d)
    zero_state = jnp.zeros((SUBLANES, d_ff), F32)
    yp, a_tail = _shared_back(xp, an, oc, osel, owin, gates, p, wts, mk_p, mv_p, bsz, zero_state, zero_state,
                              g_final, None, tm)
    keep = min(WINDOW, seq)
    shp = lambda a: a.reshape(1, bsz, seq, NSA_KV_HEADS, HEAD_DIM)
    prompt_state = (shp(kv6[0]), shp(kv6[1]), shp(kv6[2]), shp(kv6[3]), shp(kv6[4])[:, :, -keep:],
                    shp(kv6[5])[:, :, -keep:], mk_p.reshape(1, bsz, mem_len, XA_HEADS, d // XA_HEADS),
                    mv_p.reshape(1, bsz, mem_len, XA_HEADS, d // XA_HEADS),
                    a_tail.reshape(bsz, -1, SUBLANES, d_ff)[None, :, -1, SUBLANES - (CONV_W - 1):])

    xs = x_sample.reshape(nd * dec_seq, d)
    reps = CHUNK // dec_seq
    mix_w_s = jnp.tile(w_s[l][:, :dec_seq, :dec_seq], (1, reps, reps))
    bias_s = jnp.repeat(jnp.tile(b_s[l][:, :dec_seq].T, (reps, 1)), d_a // A_GROUPS, axis=1)
    tms = min(tm, nd * dec_seq)
    an_s, v_rows, qpad_s, kv6s, kvb_s, _, gates_s = _shared_front(xs, p, w_in_pad, w_vt, dec_seq, mix_w_s, bias_s,
                                                                 tms)
    pools = [c[l].transpose(0, 2, 3, 1).reshape(c.shape[1], kvw, page)
             for c in (cache_cmp_k, cache_cmp_v, cache_sel_k, cache_sel_v)]
    kc_s, vct_s = _compress_sample(page_table, pools[0], pools[1], cw)
    n_sub_s = past // CMP_STRIDE
    n_cmp_s = (past + dec_seq) // CMP_STRIDE - 1
    n_sel_s = -(-(past + dec_seq) // SEL_BLOCK)
    n_sel_pad_s = _round_up(n_sel_s, LANES)
    cmp_t, tail, newb, winb = _bias_tables_sample(rel_bias, past, dec_seq, n_cmp_s, n_sub_s, tk_s, win_buf)
    h_of, t_of, live = _row_heads(dec_seq)
    same = (live[:, None] & live[None, :] & ((h_of[:, None] // HPG) == (h_of[None, :] // HPG))
            & (t_of[:, None] == t_of[None, :]))
    qs = _heads_to_rows(qpad_s, nd, dec_seq)
    oct_s, selrows = _cmpsel_sample(qs, kc_s, vct_s, cmp_t, jnp.asarray(same, dtype=BF16), n_sel_pad_s,
                                    min(SEL_TOP, n_sel_s), past, dec_seq)
    newrows = lambda a: jnp.pad(a.reshape(nd, dec_seq, LANES), ((0, 0), (0, LANES - dec_seq), (0, 0)))
    kvn = (jnp.stack([newrows(kvb_s[:, 2 * LANES:3 * LANES]), newrows(kvb_s[:, 4 * LANES:5 * LANES])], axis=1)
           .reshape(2 * nd, LANES, LANES),
           jnp.stack([newrows(kvb_s[:, 3 * LANES:4 * LANES]), newrows(kvb_s[:, 5 * LANES:6 * LANES])], axis=1)
           .reshape(2 * nd, LANES, LANES))
    wk = cache_win_k[l].transpose(0, 2, 3, 1).reshape(nd, kvw, win_buf)
    wv = cache_win_v[l].transpose(0, 2, 3, 1).reshape(nd, kvw, win_buf)
    osel_r, owin_r = _flash_sample(page_table, qs, pools[2], pools[3], kvn, selrows,
                                   _block_expand(n_sel_pad_s, past), tail, newb, wk, wv, winb, tk_s)
    oc_s = _rows_to_tokens(oct_s.transpose(0, 2, 1), nd, dec_seq)
    osel_s = _rows_to_tokens(osel_r, nd, dec_seq)
    owin_s = _rows_to_tokens(owin_r, nd, dec_seq)
    st = state_conv[l]
    zrow = jnp.zeros_like(st[:, :1])
    s1 = jnp.concatenate([st[:, 1:2]] + [zrow] * (dec_seq - 1), axis=1).reshape(nd * dec_seq, d_ff)
    s2 = jnp.concatenate([st[:, 0:1], st[:, 1:2]] + [zrow] * (dec_seq - 2), axis=1).reshape(nd * dec_seq, d_ff)
    mk_s = cache_mem_k[l].reshape(nd, mem_len, d)
    mv_s = cache_mem_v[l].reshape(nd, mem_len, d)
    ys, a_full = _shared_back(xs, an_s, oc_s, osel_s, owin_s, gates_s, p, wts, mk_s, mv_s, nd, s1, s2, g_final,
                              dec_seq, tms)
    keep_s = min(WINDOW, past + dec_seq)
    shs = lambda a: a.reshape(1, nd, dec_seq, NSA_KV_HEADS, HEAD_DIM)
    win_new = lambda cache, new: jnp.concatenate([cache[l], shs(new)[0]], axis=1)[None, :, -keep_s:]
    sample_state = (shs(kv6s[0]), shs(kv6s[1]), shs(kv6s[2]), shs(kv6s[3]), win_new(cache_win_k, kv6s[4]),
                    win_new(cache_win_v, kv6s[5]), v_rows.reshape(1, nd, dec_seq, d_a),
                    a_full.reshape(1, nd, dec_seq, d_ff)[:, :, dec_seq - (CONV_W - 1):])

    return (yp.reshape(bsz, seq, d), ys.reshape(nd, dec_seq, d)) + prompt_state + sample_state
```

```python
import functools
import math

import numpy as np
import jax
import jax.numpy as jnp
from jax import lax
from jax.experimental import pallas as pl
from jax.experimental.pallas import tpu as pltpu

F32 = jnp.float32
BF16 = jnp.bfloat16

LANES = 128
SUBLANES = 8
VMEM_LIMIT_BYTES = 56 * 1024 * 1024

NORM_EPS = 1e-6
NEG_INF = -1e30
M_INIT = -1e29
KNOCKED_OUT = -3e38
FORCE_BONUS = 1e4

A_GROUPS = 8
CHUNK = 128
NSA_HEADS = 8
HEAD_DIM = 64
NSA_KV_HEADS = 2
HPG = NSA_HEADS // NSA_KV_HEADS
CMP_LEN = 32
CMP_STRIDE = 16
SEL_BLOCK = 64
SEL_TOP = 16
WINDOW = 512
REL_BUCKETS = 32
REL_MAX_DIST = 128
XA_HEADS = 4
CONV_W = 3
ATT_SCALE = HEAD_DIM ** -0.5
LOG2E = math.log2(math.e)

_NT = (((1,), (1,)), ((), ()))


def _cparams(*sem):
    return pltpu.CompilerParams(dimension_semantics=sem, vmem_limit_bytes=VMEM_LIMIT_BYTES)


def _gelu(x):
    return 0.5 * x * (1.0 + jnp.tanh(0.7978845608028654 * (x + 0.044715 * (x * x * x))))


def _rms(x, g):
    return x * lax.rsqrt(jnp.mean(x * x, axis=-1, keepdims=True) + NORM_EPS) * g


def _dot(a, b):
    return jnp.dot(a, b, preferred_element_type=F32)


def _dot_nt(a, b):
    return lax.dot_general(a, b, _NT, preferred_element_type=F32)


def _inproj_body(x_ref, g_ref, w_ref, wvt_ref, u_ref, v_ref, q_ref, kc_ref, vc_ref, ks_ref, vs_ref, kw_ref,
                 vw_ref, kvb_ref, vt_ref, gate_ref, *, d_a, token_major_q):
    h = _rms(x_ref[...], g_ref[...]).astype(BF16)

    def mm(lo, hi):
        return _dot(h, w_ref[:, lo:hi])

    u_ref[...] = mm(0, d_a)
    v_ref[...] = mm(d_a, 2 * d_a)
    o = 2 * d_a
    if token_major_q:
        q_ref[...] = mm(o, o + NSA_HEADS * LANES).astype(BF16)
    else:
        q_ref[...] = jnp.zeros(q_ref.shape, BF16)
    o += NSA_HEADS * LANES
    for i, r in enumerate((kc_ref, vc_ref, ks_ref, vs_ref, kw_ref, vw_ref)):
        z = mm(o + LANES * i, o + LANES * (i + 1))
        r[...] = z
        kvb_ref[:, LANES * i:LANES * (i + 1)] = z.astype(BF16)
    o += 6 * LANES
    gate_ref[...] = 1.0 / (1.0 + jnp.exp(-mm(o, o + LANES)))
    vt_ref[...] = _dot_nt(wvt_ref[...], h).astype(BF16)


def _in_proj(x, g, w_pad, w_vt, d_a, tm, token_major_q):
    m, d = x.shape
    row = lambda n: pl.BlockSpec((tm, n), lambda i: (i, 0))
    full = lambda a: pl.BlockSpec(a.shape, lambda i: (0,) * a.ndim)
    kv = jax.ShapeDtypeStruct((m, LANES), F32)
    q_stub = pl.BlockSpec((2 * SUBLANES, LANES), lambda i: (i, 0))
    return pl.pallas_call(
        functools.partial(_inproj_body, d_a=d_a, token_major_q=token_major_q),
        grid=(m // tm,),
        in_specs=[row(d), full(g), full(w_pad), full(w_vt)],
        out_specs=[row(d_a), row(d_a), row(NSA_HEADS * LANES) if token_major_q else q_stub] + [row(LANES)] * 6
                  + [row(6 * LANES), pl.BlockSpec((w_vt.shape[0], tm), lambda i: (0, i)), row(LANES)],
        out_shape=[jax.ShapeDtypeStruct((m, d_a), F32), jax.ShapeDtypeStruct((m, d_a), F32),
                   jax.ShapeDtypeStruct((m, NSA_HEADS * LANES) if token_major_q
                                        else (m // tm * 2 * SUBLANES, LANES), BF16)] + [kv] * 6
                  + [jax.ShapeDtypeStruct((m, 6 * LANES), BF16), jax.ShapeDtypeStruct((w_vt.shape[0], m), BF16),
                     jax.ShapeDtypeStruct((m, LANES), F32)],
        compiler_params=_cparams("parallel"),
        name="in_proj",
    )(x, g, w_pad, w_vt)


def _gmlp_body(u_ref, v_ref, lng_ref, lnb_ref, w_ref, bias_ref, ga_ref, a_ref, vr_ref, *, period_log2, rows, emit_v):
    if not emit_v:
        vr_ref[...] = jnp.zeros(vr_ref.shape, F32)
    ri = lax.broadcasted_iota(jnp.int32, (CHUNK, CHUNK), 0)
    ci = lax.broadcasted_iota(jnp.int32, (CHUNK, CHUNK), 1)
    mask = (ci <= ri) & ((ri >> period_log2) == (ci >> period_log2))
    wm = [jnp.where(mask, w_ref[g], 0.0).astype(BF16) for g in range(A_GROUPS)]
    lane = lax.broadcasted_iota(jnp.int32, (CHUNK, LANES), 1)
    for c in range(rows // CHUNK):
        rs = slice(CHUNK * c, CHUNK * (c + 1))
        gv = _gelu(v_ref[rs, :])
        mu = jnp.mean(gv, axis=-1, keepdims=True)
        var = jnp.mean(jnp.square(gv - mu), axis=-1, keepdims=True)
        vn = (gv - mu) * lax.rsqrt(var + NORM_EPS) * lng_ref[...] + lnb_ref[...]
        if emit_v:
            vr_ref[rs, :] = vn
        vb = vn.astype(BF16)
        tiles = []
        for j in range(A_GROUPS // 2):
            vj = vb[:, LANES * j:LANES * (j + 1)]
            tiles.append(jnp.where(lane < LANES // 2, _dot(wm[2 * j], vj), _dot(wm[2 * j + 1], vj)))
        mixed = jnp.concatenate(tiles, axis=1) + bias_ref[...]
        a_ref[rs, :] = _rms(_gelu(u_ref[rs, :]) * mixed, ga_ref[...])


def _gmlp(u, v, ln_g, ln_b, w_mix, bias_full, g_a, period, rows, emit_v):
    m, d_a = u.shape
    row = pl.BlockSpec((rows, d_a), lambda i: (i, 0))
    full = lambda a: pl.BlockSpec(a.shape, lambda i: (0,) * a.ndim)
    return pl.pallas_call(
        functools.partial(_gmlp_body, period_log2=int(math.log2(period)), rows=rows, emit_v=emit_v),
        grid=(m // rows,),
        in_specs=[row, row, full(ln_g), full(ln_b), full(w_mix), full(bias_full), full(g_a)],
        out_specs=[row, row if emit_v else pl.BlockSpec((SUBLANES, LANES), lambda i: (i, 0))],
        out_shape=[jax.ShapeDtypeStruct((m, d_a), F32),
                   jax.ShapeDtypeStruct((m, d_a) if emit_v else (m // rows * SUBLANES, LANES), F32)],
        compiler_params=_cparams("parallel"),
        name="gmlp",
    )(u, v, ln_g, ln_b, w_mix, bias_full, g_a)


_CW_KEYS = ("w1k", "w1v", "pek", "pev", "b1k", "b1v", "w2k", "w2vt", "b2k", "b2vt")


def _compress_rows(xk_ref, xv_ref, w, kc_ref, vct_ref):
    n_sub = kc_ref.shape[1]
    hid_w = w["b1k"].shape[1]

    def hidden(x_ref, w1_ref, pe_ref, b1_ref):
        fs = jnp.zeros((n_sub, 2 * hid_w), F32)
        pc = jnp.zeros((2 * SUBLANES, 2 * hid_w), F32)
        for pp in range(CMP_STRIDE // 2):
            xp = jnp.concatenate([x_ref[pl.ds(2 * pp + j, n_sub, stride=CMP_STRIDE), :] for j in range(2)],
                                 axis=1).astype(BF16)
            fs = fs + _dot(xp, w1_ref[pp])
            pc = pc + _dot(pe_ref[pp], w1_ref[pp])
        const = pc[0:1, :hid_w] + pc[1:2, hid_w:] + b1_ref[...]
        hid = fs[:, :hid_w] + pltpu.roll(fs[:, hid_w:], n_sub - 1, 0) + const
        return _gelu(hid).astype(BF16)

    kc_ref[0] = (_dot(hidden(xk_ref, w["w1k"], w["pek"], w["b1k"]), w["w2k"][...]) + w["b2k"][...]).astype(BF16)
    vct_ref[0] = (_dot_nt(w["w2vt"][...], hidden(xv_ref, w["w1v"], w["pev"], w["b1v"]))
                  + w["b2vt"][...]).astype(BF16)


def _compress_prompt_body(xk_ref, xv_ref, *refs):
    w = dict(zip(_CW_KEYS, refs[:len(_CW_KEYS)]))
    _compress_rows(xk_ref, xv_ref, w, *refs[len(_CW_KEYS):])


def _compress_prompt(kc_rows, vc_rows, cw, n_batch, seq):
    n_sub = seq // CMP_STRIDE
    blk = pl.BlockSpec((seq, LANES), lambda i: (i, 0))
    full = lambda a: pl.BlockSpec(a.shape, lambda i: (0,) * a.ndim)
    ws = [cw[k] for k in _CW_KEYS]
    return pl.pallas_call(
        _compress_prompt_body,
        grid=(n_batch,),
        in_specs=[blk, blk] + [full(a) for a in ws],
        out_specs=[pl.BlockSpec((1, n_sub, LANES), lambda i: (i, 0, 0)),
                   pl.BlockSpec((1, LANES, n_sub), lambda i: (i, 0, 0))],
        out_shape=[jax.ShapeDtypeStruct((n_batch, n_sub, LANES), BF16),
                   jax.ShapeDtypeStruct((n_batch, LANES, n_sub), BF16)],
        compiler_params=_cparams("parallel"),
        name="compress_prompt",
    )(kc_rows, vc_rows, *ws)


def _page_copies(pt_ref, n, pools, bufs, sem, slot, n_pages, page):
    for a, (pool, buf) in enumerate(zip(pools, bufs)):
        for p in range(n_pages):
            yield pltpu.make_async_copy(pool.at[pt_ref[n, p]], buf.at[slot, :, pl.ds(p * page, page)],
                                        sem.at[a, slot])


def _fetch_pages(pt_ref, pools, bufs, sem, n_pages, page):
    n = pl.program_id(0)
    slot = n % 2
    copies = functools.partial(_page_copies, pt_ref, pools=pools, bufs=bufs, sem=sem, n_pages=n_pages, page=page)

    @pl.when(n == 0)
    def _():
        for c in copies(n=n, slot=slot):
            c.start()

    @pl.when(n + 1 < pl.num_programs(0))
    def _():
        for c in copies(n=n + 1, slot=1 - slot):
            c.start()

    for c in copies(n=n, slot=slot):
        c.wait()
    return slot


def _compress_sample_body(pt_ref, kpool_ref, vpool_ref, *refs, n_pages, page):
    nw = len(_CW_KEYS)
    w = dict(zip(_CW_KEYS, refs[:nw]))
    kc_ref, vct_ref, kbuf, vbuf, sem, xk_scr, xv_scr = refs[nw:]
    slot = _fetch_pages(pt_ref, (kpool_ref, vpool_ref), (kbuf, vbuf), sem, n_pages, page)
    for buf, x_scr in ((kbuf, xk_scr), (vbuf, xv_scr)):
        for p in range(n_pages):
            x_scr[page * p:page * (p + 1), :] = buf[slot, :, page * p:page * (p + 1)].T
    _compress_rows(xk_scr, xv_scr, w, kc_ref, vct_ref)


def _compress_sample(page_table, kpool, vpool, cw):
    n, n_pages = page_table.shape
    page = kpool.shape[2]
    past = n_pages * page
    n_sub = past // CMP_STRIDE
    any_spec = pl.BlockSpec(memory_space=pl.ANY)
    full = lambda a: pl.BlockSpec(a.shape, lambda i, pt: (0,) * a.ndim)
    ws = [cw[k] for k in _CW_KEYS]
    grid_spec = pltpu.PrefetchScalarGridSpec(
        num_scalar_prefetch=1,
        grid=(n,),
        in_specs=[any_spec, any_spec] + [full(a) for a in ws],
        out_specs=[pl.BlockSpec((1, n_sub, LANES), lambda i, pt: (i, 0, 0)),
                   pl.BlockSpec((1, LANES, n_sub), lambda i, pt: (i, 0, 0))],
        scratch_shapes=[pltpu.VMEM((2, LANES, past), F32), pltpu.VMEM((2, LANES, past), F32),
                        pltpu.SemaphoreType.DMA((2, 2)),
                        pltpu.VMEM((past, LANES), F32), pltpu.VMEM((past, LANES), F32)],
    )
    return pl.pallas_call(
        functools.partial(_compress_sample_body, n_pages=n_pages, page=page),
        grid_spec=grid_spec,
        out_shape=[jax.ShapeDtypeStruct((n, n_sub, LANES), BF16), jax.ShapeDtypeStruct((n, LANES, n_sub), BF16)],
        compiler_params=_cparams("arbitrary"),
        name="compress_sample",
    )(page_table, kpool, vpool, *ws)


_PS_PAD = 16


def _importance(ps_scr, n_sel_pad):
    imp = ps_scr[pl.ds(_PS_PAD - 1, n_sel_pad, stride=4), :]
    for j in range(1, 5):
        imp = imp + ps_scr[pl.ds(_PS_PAD - 1 + j, n_sel_pad, stride=4), :]
    return imp


def _select_blocks(imp, t, n_top):
    s_idx = lax.broadcasted_iota(jnp.int32, imp.shape, 0)
    cur = t >> int(math.log2(SEL_BLOCK))
    valid = s_idx * SEL_BLOCK <= t
    forced = (s_idx == 0) | (s_idx == cur) | (s_idx == cur - 1)
    score = jnp.where(valid, imp + jnp.where(forced, FORCE_BONUS, 0.0), NEG_INF)
    s_f = s_idx.astype(F32)
    sel = jnp.zeros(imp.shape, F32)
    for _ in range(n_top):
        mx = jnp.max(score, axis=0, keepdims=True)
        first = jnp.min(jnp.where(score == mx, s_f, 1e9), axis=0, keepdims=True)
        hit = s_f == first
        sel = jnp.where(hit & (mx > 0.5 * NEG_INF), 1.0, sel)
        score = jnp.where(hit, KNOCKED_OUT, score)
    return sel


def _cmpsel_prompt_body(qt_ref, kc_ref, vct_ref, band_ref, oc_ref, sel_ref, s_scr, ps_scr, o_scr, *, n_sub,
                        n_sel_pad, n_top):
    i = pl.program_id(1)
    t0 = i * LANES
    band_rows = band_ref.shape[1]
    base = pl.multiple_of(SUBLANES * i, SUBLANES)
    s_scr[:, 0:_PS_PAD, :] = jnp.zeros((NSA_HEADS, _PS_PAD, LANES), F32)
    ps_scr[...] = jnp.zeros(ps_scr.shape, F32)

    def attend(rows):
        n_chunk = rows // LANES
        c_idx = lax.broadcasted_iota(jnp.int32, (LANES, LANES), 0)
        q_idx = lax.broadcasted_iota(jnp.int32, (LANES, LANES), 1)
        d0 = t0 + q_idx - CMP_STRIDE * c_idx - (CMP_LEN - 1)
        valid = [d0 - CMP_STRIDE * LANES * c >= 0 for c in range(n_chunk)]
        chunk = lambda c: slice(_PS_PAD + LANES * c, _PS_PAD + LANES * (c + 1))
        n_slab = LANES // SUBLANES

        def slabs(x):
            return [x[SUBLANES * i:SUBLANES * (i + 1)] for i in range(n_slab)]

        def allreduce(x, op):
            for shift in (4, 2, 1):
                x = op(x, pltpu.roll(x, shift, 0))
            return x

        for h in range(NSA_HEADS):
            for c in range(n_chunk):
                s_scr[h, chunk(c), :] = _dot(kc_ref[0, LANES * c:LANES * (c + 1), :],
                                             qt_ref[LANES * h:LANES * (h + 1), :])
            s_scr[h, pl.ds(base, band_rows), :] = s_scr[h, pl.ds(base, band_rows), :] + band_ref[h]
        for g in range(NSA_KV_HEADS):
            stats = []
            for hh in range(HPG):
                h = HPG * g + hh
                m = functools.reduce(jnp.maximum, [x for c in range(n_chunk)
                                                   for x in slabs(jnp.where(valid[c], s_scr[h, chunk(c), :], NEG_INF))])
                m = allreduce(m, jnp.maximum)
                l = functools.reduce(jnp.add, [x for c in range(n_chunk) for x in slabs(
                    jnp.where(valid[c], jnp.exp2(s_scr[h, chunk(c), :] - jnp.concatenate([m] * n_slab, axis=0)), 0.0))])
                l = allreduce(l, jnp.add)
                stats.append((jnp.concatenate([m] * n_slab, axis=0),
                              jnp.concatenate([1.0 / jnp.where(l > 0.0, l, 1.0)] * n_slab, axis=0)))
            o_acc = [jnp.zeros((HEAD_DIM, LANES), F32) for _ in range(HPG)]
            for c in range(n_chunk):
                psum = jnp.zeros((LANES, LANES), F32)
                for hh in range(HPG):
                    h = HPG * g + hh
                    m_full, inv_full = stats[hh]
                    p = jnp.where(valid[c], jnp.exp2(s_scr[h, chunk(c), :] - m_full), 0.0) * inv_full
                    psum = psum + p
                    o_acc[hh] = o_acc[hh] + _dot(
                        vct_ref[0, HEAD_DIM * g:HEAD_DIM * (g + 1), LANES * c:LANES * (c + 1)], p.astype(BF16))
                ps_scr[g, chunk(c), :] = psum
            for hh in range(HPG):
                h = HPG * g + hh
                o_scr[HEAD_DIM * h:HEAD_DIM * (h + 1), :] = o_acc[hh]

    blocks_per_case = LANES // SUBLANES
    for case in range(n_sub // LANES):
        @pl.when((i >= blocks_per_case * case) & (i < blocks_per_case * (case + 1)))
        def _():
            attend(LANES * (case + 1))

    t = t0 + lax.broadcasted_iota(jnp.int32, (n_sel_pad, LANES), 1)
    for g in range(NSA_KV_HEADS):
        sel_ref[0, g] = _select_blocks(_importance(ps_scr.at[g], n_sel_pad), t, n_top)
    oc_ref[...] = o_scr[...].T


def _cmpsel_prompt(qvt, kc, vct, band, n_batch, seq, n_sel_pad, n_top):
    n_sub = kc.shape[1]
    nblk = seq // LANES
    full = lambda a: pl.BlockSpec(a.shape, lambda n, i: (0,) * a.ndim)
    return pl.pallas_call(
        functools.partial(_cmpsel_prompt_body, n_sub=n_sub, n_sel_pad=n_sel_pad, n_top=n_top),
        grid=(n_batch, nblk),
        in_specs=[pl.BlockSpec((NSA_HEADS * LANES, LANES), lambda n, i: (0, n * nblk + i)),
                  pl.BlockSpec((1, n_sub, LANES), lambda n, i: (n, 0, 0)),
                  pl.BlockSpec((1, LANES, n_sub), lambda n, i: (n, 0, 0)),
                  full(band)],
        out_specs=[pl.BlockSpec((LANES, NSA_HEADS * HEAD_DIM), lambda n, i: (n * nblk + i, 0)),
                   pl.BlockSpec((1, NSA_KV_HEADS, n_sel_pad, LANES), lambda n, i: (n, 0, 0, i))],
        out_shape=[jax.ShapeDtypeStruct((n_batch * seq, NSA_HEADS * HEAD_DIM), F32),
                   jax.ShapeDtypeStruct((n_batch, NSA_KV_HEADS, n_sel_pad, seq), F32)],
        scratch_shapes=[pltpu.VMEM((NSA_HEADS, _PS_PAD + n_sub, LANES), F32),
                        pltpu.VMEM((NSA_KV_HEADS, _PS_PAD + 4 * n_sel_pad, LANES), F32),
                        pltpu.VMEM((NSA_HEADS * HEAD_DIM, LANES), F32)],
        compiler_params=_cparams("parallel", "parallel"),
        name="cmpsel_prompt",
    )(qvt, kc, vct, band)


def _cmpsel_sample_body(q_ref, kc_ref, vct_ref, bias_ref, rmat_ref, oct_ref, sel_ref, ps_scr, *, n_sub, n_sel_pad,
                        n_top, past, dec_seq):
    s = _dot_nt(kc_ref[0], q_ref[0]) + bias_ref[...]
    m = jnp.max(s, axis=0, keepdims=True)
    e = jnp.exp2(s - m)
    p = e * (1.0 / jnp.sum(e, axis=0, keepdims=True))
    oct_ref[0] = _dot(vct_ref[0], p.astype(BF16))
    hi = p.astype(BF16)
    lo = (p - hi.astype(F32)).astype(BF16)
    psum = _dot(hi, rmat_ref[...]) + _dot(lo, rmat_ref[...])
    ps_scr[...] = jnp.zeros(ps_scr.shape, F32)
    ps_scr[_PS_PAD:_PS_PAD + n_sub, :] = psum
    col = lax.broadcasted_iota(jnp.int32, (n_sel_pad, LANES), 1)
    t = past + (col & (dec_seq - 1))
    sel = _select_blocks(_importance(ps_scr, n_sel_pad), t, n_top)
    sel_ref[0] = sel.T.astype(BF16)


def _cmpsel_sample(qs, kc, vct, bias_t, rmat, n_sel_pad, n_top, past, dec_seq):
    n, n_sub, _ = kc.shape
    full = lambda a: pl.BlockSpec(a.shape, lambda i: (0,) * a.ndim)
    return pl.pallas_call(
        functools.partial(_cmpsel_sample_body, n_sub=n_sub, n_sel_pad=n_sel_pad, n_top=n_top, past=past,
                          dec_seq=dec_seq),
        grid=(n,),
        in_specs=[pl.BlockSpec((1, LANES, LANES), lambda i: (i, 0, 0)),
                  pl.BlockSpec((1, n_sub, LANES), lambda i: (i, 0, 0)),
                  pl.BlockSpec((1, LANES, n_sub), lambda i: (i, 0, 0)),
                  full(bias_t), full(rmat)],
        out_specs=[pl.BlockSpec((1, LANES, LANES), lambda i: (i, 0, 0)),
                   pl.BlockSpec((1, LANES, n_sel_pad), lambda i: (i, 0, 0))],
        out_shape=[jax.ShapeDtypeStruct((n, LANES, LANES), F32), jax.ShapeDtypeStruct((n, LANES, n_sel_pad), BF16)],
        scratch_shapes=[pltpu.VMEM((_PS_PAD + 4 * n_sel_pad, LANES), F32)],
        compiler_params=_cparams("parallel"),
        name="cmpsel_sample",
    )(qs, kc, vct, bias_t, rmat)


FAR_UNROLL = 4


def _flash_t_update(h, k, vt, qt, bias_ref, mask_rows, m_ref, l_ref, acc_ref):
    s = _dot(k, qt)
    m = m_ref[h]
    l = l_ref[h]
    acc = acc_ref[h]
    n_slab = SEL_BLOCK // SUBLANES
    for c in range(k.shape[0] // SEL_BLOCK):
        rows = slice(SEL_BLOCK * c, SEL_BLOCK * (c + 1))
        sc = s[rows]
        if bias_ref is not None:
            sc = sc + bias_ref[rows, :]
        if mask_rows is not None:
            sc = jnp.where(mask_rows[c:c + 1, :] > 0.5, sc, NEG_INF)
        slabs = [sc[SUBLANES * i:SUBLANES * (i + 1)] for i in range(n_slab)]
        cm = functools.reduce(jnp.maximum, slabs)
        for shift in (4, 2, 1):
            cm = jnp.maximum(cm, pltpu.roll(cm, shift, 0))
        m_new = jnp.maximum(m, cm)
        alpha = jnp.exp2(m - m_new)
        ps = [jnp.exp2(x - m_new) for x in slabs]
        l = alpha * l + functools.reduce(jnp.add, ps)
        pv = _dot(vt[:, rows], jnp.concatenate(ps, axis=0).astype(BF16))
        acc = jnp.concatenate([alpha] * (acc.shape[0] // SUBLANES), axis=0) * acc + pv
        m = m_new
    m_ref[h] = m
    l_ref[h] = l
    acc_ref[h] = acc


def _flash_prompt_body(qt_ref, ks_ref, vst_ref, kw_ref, vwt_ref, sel_ref, near_ref, winfar_ref, osel_ref, owin_ref,
                       ms, ls, accs, mw, lw, accw, *, tq):
    qi = pl.program_id(1)
    for m_r, l_r, acc_r in ((ms, ls, accs), (mw, lw, accw)):
        m_r[...] = jnp.full(m_r.shape, M_INIT, F32)
        l_r[...] = jnp.zeros(l_r.shape, F32)
        acc_r[...] = jnp.zeros(acc_r.shape, F32)
    blocks = tq // SEL_BLOCK

    def aligned(x, m):
        return x if isinstance(x, int) else pl.multiple_of(x, m)

    def sel_tile(kt, near, parity=None):
        k0 = aligned(kt * tq, tq)
        k = ks_ref[pl.ds(k0, tq), :]
        r0 = aligned((kt // 2) * SUBLANES, SUBLANES)
        for g in range(NSA_KV_HEADS):
            r8 = sel_ref[0, g, pl.ds(r0, SUBLANES), :]
            if parity is None:
                r4 = jnp.where((kt & 1) == 1, r8[blocks:2 * blocks], r8[0:blocks])
            else:
                r4 = r8[blocks * parity:blocks * (parity + 1)]
            vt = vst_ref[HEAD_DIM * g:HEAD_DIM * (g + 1), pl.ds(k0, tq)]
            for hh in range(HPG):
                h = HPG * g + hh
                bias = None if near is None else near_ref.at[near, h]
                _flash_t_update(h, k, vt, qt_ref[LANES * h:LANES * (h + 1), :], bias, r4, ms, ls, accs)

    def win_tile(kt, near):
        k0 = aligned(kt * tq, tq)
        k = kw_ref[pl.ds(k0, tq), :]
        for g in range(NSA_KV_HEADS):
            vt = vwt_ref[HEAD_DIM * g:HEAD_DIM * (g + 1), pl.ds(k0, tq)]
            for hh in range(HPG):
                h = HPG * g + hh
                bias = winfar_ref if near is None else near_ref.at[near, h]
                _flash_t_update(h, k, vt, qt_ref[LANES * h:LANES * (h + 1), :], bias, None, mw, lw, accw)

    n_far = jnp.maximum(qi - 1, 0)
    n_group = n_far // FAR_UNROLL

    def far_body(j, carry):
        for u in range(FAR_UNROLL):
            sel_tile(FAR_UNROLL * j + u, None, u % 2)
        return carry

    def far_rest(kt, carry):
        sel_tile(kt, None)
        return carry

    lax.fori_loop(0, n_group, far_body, 0)
    lax.fori_loop(FAR_UNROLL * n_group, n_far, far_rest, 0)

    @pl.when(qi >= 2)
    def _():
        sel_tile(qi - 1, 0)
        win_tile(qi - 2, None)
        win_tile(qi - 1, 0)
        sel_tile(qi, 1)
        win_tile(qi, 1)

    @pl.when(qi == 1)
    def _():
        sel_tile(0, 0, 0)
        win_tile(0, 0)
        sel_tile(1, 1, 1)
        win_tile(1, 1)

    @pl.when(qi == 0)
    def _():
        sel_tile(0, 1, 0)
        win_tile(0, 1)

    for out_ref, l_r, acc_r in ((osel_ref, ls, accs), (owin_ref, lw, accw)):
        o_t = jnp.concatenate([acc_r[h] * (1.0 / jnp.sum(l_r[h], axis=0, keepdims=True))
                               for h in range(NSA_HEADS)], axis=0)
        out_ref[...] = o_t.T


def _flash_prompt(kvb, qvt, sel_t, near, winfar, n_batch, seq, tq):
    nq = seq // tq
    assert WINDOW == 2 * tq and tq == 4 * SEL_BLOCK, "window = two key tiles; a key tile = 4 selection blocks"
    n_sel_pad = sel_t.shape[2]
    k_spec = lambda col: pl.BlockSpec((seq, LANES), lambda n, i: (n, col))
    vt_spec = lambda row: pl.BlockSpec((LANES, seq), lambda n, i: (row, n))
    full = lambda a: pl.BlockSpec(a.shape, lambda n, i: (0,) * a.ndim)
    out_spec = pl.BlockSpec((tq, NSA_HEADS * HEAD_DIM), lambda n, i: (n * nq + i, 0))
    state = [pltpu.VMEM((NSA_HEADS, SUBLANES, tq), F32), pltpu.VMEM((NSA_HEADS, SUBLANES, tq), F32),
             pltpu.VMEM((NSA_HEADS, HEAD_DIM, tq), F32)]
    out = jax.ShapeDtypeStruct((n_batch * seq, NSA_HEADS * HEAD_DIM), F32)
    return pl.pallas_call(
        functools.partial(_flash_prompt_body, tq=tq),
        grid=(n_batch, nq),
        in_specs=[pl.BlockSpec((NSA_HEADS * LANES, tq), lambda n, i: (0, n * nq + i)),
                  k_spec(2), vt_spec(NSA_HEADS), k_spec(4), vt_spec(NSA_HEADS + 1),
                  pl.BlockSpec((1, NSA_KV_HEADS, n_sel_pad, tq), lambda n, i: (n, 0, 0, i)),
                  full(near), full(winfar)],
        out_specs=[out_spec, out_spec],
        out_shape=[out, out],
        scratch_shapes=state + state,
        compiler_params=_cparams("parallel", "parallel"),
        name="flash_prompt",
    )(qvt, kvb, qvt, kvb, qvt, sel_t, near, winfar)


def _flash_sample_body(pt_ref, q_ref, kpool_ref, vpool_ref, ksn_ref, vsn_ref, sel_ref, e_ref, tailb_ref, newb_ref,
                       kwt_ref, vwt_ref, winb_ref, osel_ref, owin_ref, kbuf, vbuf, sem, s_scr, *, tk, n_pages, page):
    slot = _fetch_pages(pt_ref, (kpool_ref, vpool_ref), (kbuf, vbuf), sem, n_pages, page)
    q = q_ref[0]
    nt = n_pages * page // tk
    s_new = _dot_nt(q, ksn_ref[0]) + newb_ref[...]

    def score_tile(kt, tail):
        k0 = pl.multiple_of(kt * tk, tk)
        mask = _dot(sel_ref[0], e_ref[:, pl.ds(k0, tk)]) > 0.5
        s = _dot(q, kbuf[slot, :, pl.ds(k0, tk)].astype(BF16))
        if tail:
            s = s + tailb_ref[...]
        s = jnp.where(mask, s, NEG_INF)
        s_scr[:, pl.ds(k0, tk)] = s
        return jnp.max(s, axis=1, keepdims=True)

    m = lax.fori_loop(0, nt - 1, lambda kt, m: jnp.maximum(m, score_tile(kt, False)),
                      jnp.max(s_new, axis=1, keepdims=True), unroll=3)
    m = jnp.maximum(m, score_tile(nt - 1, True))

    def pv_tile(kt, carry):
        l, acc = carry
        k0 = pl.multiple_of(kt * tk, tk)
        p = jnp.exp2(s_scr[:, pl.ds(k0, tk)] - m)
        return (l + jnp.sum(p, axis=1, keepdims=True),
                acc + _dot_nt(p.astype(BF16), vbuf[slot, :, pl.ds(k0, tk)].astype(BF16)))

    p_new = jnp.exp2(s_new - m)
    l, acc = lax.fori_loop(0, nt, pv_tile, (jnp.sum(p_new, axis=1, keepdims=True),
                                            _dot(p_new.astype(BF16), vsn_ref[0])), unroll=4)
    osel_ref[0] = acc * (1.0 / l)

    s_w = _dot(q, kwt_ref[0].astype(BF16)) + winb_ref[...]
    s_new = _dot_nt(q, ksn_ref[1]) + newb_ref[...]
    m = jnp.maximum(jnp.max(s_w, axis=1, keepdims=True), jnp.max(s_new, axis=1, keepdims=True))
    p_w = jnp.exp2(s_w - m)
    p_new = jnp.exp2(s_new - m)
    l = jnp.sum(p_w, axis=1, keepdims=True) + jnp.sum(p_new, axis=1, keepdims=True)
    acc = _dot_nt(p_w.astype(BF16), vwt_ref[0].astype(BF16)) + _dot(p_new.astype(BF16), vsn_ref[1])
    owin_ref[0] = acc * (1.0 / l)


def _flash_sample(page_table, qs, kpool, vpool, kvn, selrows, e_all, tailb, newb, kw, vw, winb, tk):
    n, n_pages = page_table.shape
    page = kpool.shape[2]
    assert (n_pages * page // tk - 1) % 3 == 0 and (n_pages * page // tk) % 4 == 0, "unroll factors of the key loops"
    past = n_pages * page
    per_n = lambda a: pl.BlockSpec((1,) + a.shape[1:], lambda i, pt: (i,) + (0,) * (a.ndim - 1))
    full = lambda a: pl.BlockSpec(a.shape, lambda i, pt: (0,) * a.ndim)
    any_spec = pl.BlockSpec(memory_space=pl.ANY)
    new_spec = pl.BlockSpec((2, LANES, LANES), lambda i, pt: (i, 0, 0))
    out = jax.ShapeDtypeStruct((n, LANES, LANES), F32)
    grid_spec = pltpu.PrefetchScalarGridSpec(
        num_scalar_prefetch=1,
        grid=(n,),
        in_specs=[per_n(qs), any_spec, any_spec, new_spec, new_spec, per_n(selrows), full(e_all), full(tailb),
                  full(newb), per_n(kw), per_n(vw), full(winb)],
        out_specs=[per_n(out), per_n(out)],
        scratch_shapes=[pltpu.VMEM((2, LANES, past), F32), pltpu.VMEM((2, LANES, past), F32),
                        pltpu.SemaphoreType.DMA((2, 2)), pltpu.VMEM((LANES, past), F32)],
    )
    return pl.pallas_call(
        functools.partial(_flash_sample_body, tk=tk, n_pages=n_pages, page=page),
        grid_spec=grid_spec,
        out_shape=[out, out],
        compiler_params=_cparams("arbitrary"),
        name="flash_sample",
    )(page_table, qs, kpool, vpool, kvn[0], kvn[1], selrows, e_all, tailb, newb, kw, vw, winb)


def _mixout_body(an_ref, oc_ref, os_ref, ow_ref, gate_ref, x_ref, gb_ref, woa_ref, wob_ref, y_ref):
    gates = gate_ref[...]
    tm = gates.shape[0]
    lane = lax.broadcasted_iota(jnp.int32, (tm, LANES), 1)
    tiles = []
    for j in range(NSA_HEADS // 2):
        cols = slice(LANES * j, LANES * (j + 1))
        acc = None
        for br, o_ref in enumerate((oc_ref, os_ref, ow_ref)):
            c0 = 3 * (2 * j) + br
            c1 = 3 * (2 * j + 1) + br
            gcol = jnp.where(lane < HEAD_DIM, gates[:, c0:c0 + 1], gates[:, c1:c1 + 1])
            term = gcol * o_ref[:, cols]
            acc = term if acc is None else acc + term
        tiles.append(acc)
    bn = _rms(jnp.concatenate(tiles, axis=1), gb_ref[...]).astype(BF16)
    y_ref[...] = x_ref[...] + _dot(an_ref[...].astype(BF16), woa_ref[...]) + _dot(bn, wob_ref[...])


def _mixout(an, oc, osel, owin, gates, x, g_b, wo_a, wo_b, tm):
    m, d = x.shape
    row = lambda a: pl.BlockSpec((tm, a.shape[1]), lambda i: (i, 0))
    full = lambda a: pl.BlockSpec(a.shape, lambda i: (0,) * a.ndim)
    return pl.pallas_call(
        _mixout_body,
        grid=(m // tm,),
        in_specs=[row(an), row(oc), row(osel), row(owin), row(gates), row(x), full(g_b), full(wo_a), full(wo_b)],
        out_specs=row(x),
        out_shape=jax.ShapeDtypeStruct((m, d), F32),
        compiler_params=_cparams("parallel"),
        name="mix_out",
    )(an, oc, osel, owin, gates, x, g_b, wo_a, wo_b)


def _normmm_body(x_ref, g_ref, w_ref, o_ref):
    o_ref[...] = _dot(_rms(x_ref[...], g_ref[...]).astype(BF16), w_ref[...]).astype(o_ref.dtype)


def _norm_matmul(x, g, w, out_dtype, tm):
    m, d = x.shape
    n = w.shape[1]
    return pl.pallas_call(
        _normmm_body,
        grid=(m // tm,),
        in_specs=[pl.BlockSpec((tm, d), lambda i: (i, 0)), pl.BlockSpec(g.shape, lambda i: (0, 0)),
                  pl.BlockSpec(w.shape, lambda i: (0, 0))],
        out_specs=pl.BlockSpec((tm, n), lambda i: (i, 0)),
        out_shape=jax.ShapeDtypeStruct((m, n), out_dtype),
        compiler_params=_cparams("parallel"),
        name="norm_matmul",
    )(x, g, w)


def _mmres_body(a_ref, w_ref, r_ref, o_ref):
    o_ref[...] = r_ref[...] + _dot(a_ref[...].astype(BF16), w_ref[...])


def _matmul_residual(a, w, res, tm):
    m, k = a.shape
    n = w.shape[1]
    return pl.pallas_call(
        _mmres_body,
        grid=(m // tm,),
        in_specs=[pl.BlockSpec((tm, k), lambda i: (i, 0)), pl.BlockSpec(w.shape, lambda i: (0, 0)),
                  pl.BlockSpec((tm, n), lambda i: (i, 0))],
        out_specs=pl.BlockSpec((tm, n), lambda i: (i, 0)),
        out_shape=jax.ShapeDtypeStruct((m, n), F32),
        compiler_params=_cparams("parallel"),
        name="matmul_residual",
    )(a, w, res)


def _xattn_body(q_ref, mk_ref, mv_ref, o_ref, *, n_heads):
    hd = q_ref.shape[2] // n_heads
    scale = hd ** -0.5
    for h in range(n_heads):
        cols = slice(hd * h, hd * (h + 1))
        s = _dot_nt(q_ref[0, :, cols], mk_ref[0, :, cols].astype(BF16)) * scale
        e = jnp.exp(s - jnp.max(s, axis=1, keepdims=True))
        p = e * (1.0 / jnp.sum(e, axis=1, keepdims=True))
        o_ref[0, :, cols] = _dot(p.astype(BF16), mv_ref[0, :, cols].astype(BF16))


def _xattn_core(q, mk, mv, tm):
    n, t, d = q.shape
    return pl.pallas_call(
        functools.partial(_xattn_body, n_heads=XA_HEADS),
        grid=(n, t // tm),
        in_specs=[pl.BlockSpec((1, tm, d), lambda b, i: (b, i, 0)),
                  pl.BlockSpec((1,) + mk.shape[1:], lambda b, i: (b, 0, 0)),
                  pl.BlockSpec((1,) + mv.shape[1:], lambda b, i: (b, 0, 0))],
        out_specs=pl.BlockSpec((1, tm, d), lambda b, i: (b, i, 0)),
        out_shape=jax.ShapeDtypeStruct((n, t, d), F32),
        compiler_params=_cparams("parallel", "parallel"),
        name="xattn_core",
    )(q, mk, mv)


def _ffn_body(x_ref, g_ref, wu_ref, wg_ref, cw_ref, cb_ref, wd_ref, gf_ref, s1_ref, s2_ref, y_ref, a_ref, h_scr,
              acc_scr, carry_scr, *, tf, dec_seq):
    i = pl.program_id(1)
    j = pl.program_id(2)
    tm = x_ref.shape[0]

    @pl.when(j == 0)
    def _():
        h_scr[...] = _rms(x_ref[...], g_ref[...]).astype(BF16)
        acc_scr[...] = jnp.zeros(acc_scr.shape, F32)

    a = _dot(h_scr[...], wu_ref[...])
    gt = _dot(h_scr[...], wg_ref[...])
    row = lax.broadcasted_iota(jnp.int32, (tm, tf), 0)
    r1 = pltpu.roll(a, 1, 0)
    r2 = pltpu.roll(a, 2, 0)
    if dec_seq is None:
        cols = pl.ds(pl.multiple_of(j * tf, LANES), tf)

        @pl.when(i == 0)
        def _():
            carry_scr[:, cols] = jnp.zeros((SUBLANES, tf), F32)

        p0 = carry_scr[SUBLANES - 2:SUBLANES - 1, cols]
        p1 = carry_scr[SUBLANES - 1:SUBLANES, cols]
        a1 = jnp.where(row == 0, p1, r1)
        a2 = jnp.where(row == 0, p0, jnp.where(row == 1, p1, r2))
        carry_scr[:, cols] = a[tm - SUBLANES:tm, :]
        a_ref[0] = a[tm - SUBLANES:tm, :]
    else:
        t = row & (dec_seq - 1)
        a1 = jnp.where(t == 0, s1_ref[...], r1)
        a2 = jnp.where(t < 2, s2_ref[...], r2)
        a_ref[...] = a
    c = cb_ref[...] + a2 * cw_ref[0:1, :] + a1 * cw_ref[1:2, :] + a * cw_ref[2:3, :]
    acc_scr[...] += _dot((_gelu(c) * gt).astype(BF16), wd_ref[...])

    @pl.when(j == pl.num_programs(2) - 1)
    def _():
        y_ref[...] = _rms(x_ref[...] + acc_scr[...], gf_ref[...])


def _ffn(x, g, wu, wg, cw, cb, wd, g_final, s1, s2, n_seq, tm, tf, dec_seq):
    m, d = x.shape
    f = wu.shape[1]
    nt = m // n_seq // tm
    nf = f // tf
    row = pl.BlockSpec((tm, d), lambda n, i, j: (n * nt + i, 0))
    vec = lambda a: pl.BlockSpec(a.shape, lambda n, i, j: (0,) * a.ndim)
    fcol = lambda a: pl.BlockSpec((a.shape[0], tf), lambda n, i, j: (0, j))
    if dec_seq is None:
        st_spec = pl.BlockSpec((SUBLANES, tf), lambda n, i, j: (0, j))
        a_spec = pl.BlockSpec((1, SUBLANES, tf), lambda n, i, j: (n * nt + i, 0, j))
        a_shape = jax.ShapeDtypeStruct((n_seq * nt, SUBLANES, f), F32)
    else:
        st_spec = pl.BlockSpec((tm, tf), lambda n, i, j: (n * nt + i, j))
        a_spec = st_spec
        a_shape = jax.ShapeDtypeStruct((m, f), F32)
    return pl.pallas_call(
        functools.partial(_ffn_body, tf=tf, dec_seq=dec_seq),
        grid=(n_seq, nt, nf),
        in_specs=[row, vec(g), fcol(wu), fcol(wg), fcol(cw), fcol(cb),
                  pl.BlockSpec((tf, d), lambda n, i, j: (j, 0)), vec(g_final), st_spec, st_spec],
        out_specs=[row, a_spec],
        out_shape=[jax.ShapeDtypeStruct((m, d), F32), a_shape],
        scratch_shapes=[pltpu.VMEM((tm, d), BF16), pltpu.VMEM((tm, d), F32), pltpu.VMEM((SUBLANES, f), F32)],
        compiler_params=_cparams("parallel", "arbitrary", "arbitrary"),
        name="conv_ffn",
    )(x, g, wu, wg, cw, cb, wd, g_final, s1, s2)


def _t5_bucket_np(dist):
    n = np.maximum(dist, 0)
    max_exact = REL_BUCKETS // 2
    nf = np.maximum(n, 1).astype(np.float64)
    large = max_exact + (np.log(nf / max_exact) / math.log(REL_MAX_DIST / max_exact)
                         * (REL_BUCKETS - max_exact)).astype(np.int32)
    large = np.minimum(large, REL_BUCKETS - 1)
    return np.where(n < max_exact, n, large).astype(np.int32)


def _bias_vectors(rel_bias, dist, live, masked):
    dist, live, masked = np.broadcast_arrays(dist, live, masked)
    onehot = np.zeros(dist.shape + (REL_BUCKETS,), np.float32)
    np.put_along_axis(onehot, _t5_bucket_np(dist)[..., None], 1.0, axis=-1)
    onehot[..., REL_BUCKETS - 1] -= 1.0
    onehot *= live[..., None]
    add = np.where(masked, NEG_INF, 0.0).astype(np.float32)
    table = jnp.einsum("...b,bh->h...", jnp.asarray(onehot), rel_bias, precision=lax.Precision.HIGHEST)
    return table * LOG2E + add


def _toeplitz(v, rows, cols, stride):
    length = v.shape[-1]
    w = length - stride
    assert cols <= w
    t = jnp.tile(v, (1,) * (v.ndim - 1) + (rows,))[..., :rows * w]
    return t.reshape(v.shape[:-1] + (rows, w))[..., :cols]


def _bias_tables_prompt(rel_bias, tq):
    k = np.arange(2 * tq)
    upper = k < tq
    diag = _bias_vectors(rel_bias, k, upper, ~upper)
    prev = _bias_vectors(rel_bias, np.where(upper, k + tq, k - tq), True, False)
    near = jnp.stack([_toeplitz(prev, tq, tq, 1), _toeplitz(diag, tq, tq, 1)])
    winfar = _toeplitz(jnp.asarray(np.where(upper, NEG_INF, 0.0), F32), tq, tq, 1)
    length = 4 * LANES
    off = 2 * SUBLANES * CMP_STRIDE - (CMP_LEN - 1)
    kk = np.arange(length)
    d = np.where(kk < LANES, kk + off, kk - length + off)
    band = _toeplitz(_bias_vectors(rel_bias, d, (d >= 0) & ((kk < LANES) | (kk >= LANES + CMP_STRIDE)), False),
                     3 * SUBLANES, LANES, CMP_STRIDE)
    return near, winfar, band


def _bias_tables_sample(rel_bias, past, dec_seq, n_cmp, n_sub, tk, win_buf):
    t = np.arange(dec_seq)[:, None]

    def rows(dist, live, masked):
        tab = _bias_vectors(rel_bias, dist, live, masked)
        tab = tab.reshape(NSA_HEADS * dec_seq, -1)
        return jnp.pad(tab, ((0, LANES - NSA_HEADS * dec_seq), (0, 0)))

    c = np.arange(n_sub)[None, :]
    d_cmp = past + t - (c * CMP_STRIDE + CMP_LEN - 1)
    ok = (d_cmp >= 0) & (c < n_cmp)
    cmp_t = rows(d_cmp, ok, ~ok).T
    tail = rows(tk + t - np.arange(tk)[None, :], True, False)
    jn = np.arange(LANES)[None, :]
    ok = (jn <= t) & (jn < dec_seq)
    newb = rows(t - jn, ok, ~ok)
    d_win = win_buf + t - np.arange(win_buf)[None, :]
    ok = d_win < WINDOW
    winb = rows(d_win, ok, ~ok)
    return cmp_t, tail, newb, winb


def _row_heads(dec_seq):
    rows = np.arange(LANES)
    live = rows < NSA_HEADS * dec_seq
    return np.where(live, rows // dec_seq, 0), rows % dec_seq, live


def _prep_in_proj(w_in, d_a):
    d_b = NSA_HEADS * HEAD_DIM
    o_q = 2 * d_a
    o_kv = o_q + d_b
    o_g = o_kv + 6 * NSA_KV_HEADS * HEAD_DIM
    wq = w_in[:, o_q:o_kv].reshape(-1, NSA_HEADS, HEAD_DIM) * (ATT_SCALE * LOG2E)
    slot = jnp.zeros((w_in.shape[0], NSA_HEADS, NSA_KV_HEADS, HEAD_DIM), w_in.dtype)
    for h in range(NSA_HEADS):
        slot = slot.at[:, h, h // HPG].set(wq[:, h])
    wg = jnp.pad(w_in[:, o_g:], ((0, 0), (0, LANES - (w_in.shape[1] - o_g))))
    w_pad = jnp.concatenate([w_in[:, :o_q], slot.reshape(w_in.shape[0], -1), w_in[:, o_kv:o_g], wg], axis=1)
    kvw = NSA_KV_HEADS * HEAD_DIM
    w_vt = jnp.concatenate([slot.reshape(w_in.shape[0], -1), w_in[:, o_kv + 3 * kvw:o_kv + 4 * kvw],
                            w_in[:, o_kv + 5 * kvw:o_kv + 6 * kvw]], axis=1).T
    return w_pad.astype(BF16), w_vt.astype(BF16)


def _prep_compress(w1, b1, w2, b2, pe):
    eye = jnp.eye(NSA_KV_HEADS, dtype=w1.dtype)
    hid = w1.shape[-1]
    kvw = NSA_KV_HEADS * HEAD_DIM
    w1p = jnp.einsum("apdh,gk->pgdakh", w1.reshape(2, CMP_STRIDE, HEAD_DIM, hid), eye)
    w1p = w1p.reshape(CMP_STRIDE // 2, 2 * kvw, 2 * NSA_KV_HEADS * hid)
    pe_rows = jnp.broadcast_to(pe.reshape(2, CMP_STRIDE, 1, HEAD_DIM), (2, CMP_STRIDE, NSA_KV_HEADS, HEAD_DIM))
    pe_rows = pe_rows.reshape(2, CMP_STRIDE // 2, 2 * kvw).transpose(1, 0, 2)
    pe_rows = jnp.pad(pe_rows, ((0, 0), (0, 2 * SUBLANES - 2), (0, 0)))
    w2big = jnp.einsum("hd,gk->ghkd", w2, eye).reshape(NSA_KV_HEADS * hid, kvw)
    return (w1p.astype(BF16), pe_rows.astype(BF16), jnp.tile(b1, NSA_KV_HEADS)[None, :], w2big.astype(BF16),
            jnp.tile(b2, NSA_KV_HEADS)[None, :])


def _prep_compress_kv(p):
    w1k, pek, b1k, w2k, b2k = _prep_compress(p["w_cmp1_k"], p["b_cmp1_k"], p["w_cmp2_k"], p["b_cmp2_k"], p["pe_cmp_k"])
    w1v, pev, b1v, w2v, b2v = _prep_compress(p["w_cmp1_v"], p["b_cmp1_v"], p["w_cmp2_v"], p["b_cmp2_v"], p["pe_cmp_v"])
    return dict(w1k=w1k, w1v=w1v, pek=pek, pev=pev, b1k=b1k, b1v=b1v, w2k=w2k, w2vt=w2v.T, b2k=b2k, b2vt=b2v.T)


def _block_expand(n_sel_pad, n_keys):
    s = np.arange(n_sel_pad)[:, None]
    k = np.arange(n_keys)[None, :]
    return jnp.asarray((k // SEL_BLOCK == s), dtype=BF16)


def _round_up(x, m):
    return -(-x // m) * m


def _shared_front(x2d, p, w_in_pad, w_vt, period, mix_w, mix_b, tm, decode):
    d_a = p["g_a"].shape[0]
    u, v, qpad, kc, vc, ks, vs, kw, vw, kvb, vt, gates = _in_proj(x2d, p["g_mix"][None, :], w_in_pad, w_vt, d_a, tm,
                                                                  decode)
    an, v_rows = _gmlp(u, v, p["ln_v_g"][None, :], p["ln_v_b"][None, :], mix_w, mix_b, p["g_a"][None, :], period, tm,
                       decode)
    return an, v_rows, qpad, (kc, vc, ks, vs, kw, vw), kvb, vt, gates


def _shared_back(x2d, an, oc, osel, owin, gates, p, wts, mk, mv, n_seq, s1, s2, g_final, dec_seq, tm):
    d = x2d.shape[1]
    x1 = _mixout(an, oc, osel, owin, gates, x2d, p["g_b"][None, :], wts["wo_a"], wts["wo_b"], tm)
    qx = _norm_matmul(x1, p["g_xa"][None, :], wts["w_xq"], BF16, tm)
    t = x1.shape[0] // n_seq
    if dec_seq is None:
        ox = _xattn_core(qx.reshape(n_seq, t, d), mk, mv, tm).reshape(-1, d)
    else:
        qx3 = jnp.pad(qx.reshape(n_seq, t, d), ((0, 0), (0, SUBLANES - t), (0, 0)))
        ox = _xattn_core(qx3, mk, mv, SUBLANES)[:, :t].reshape(-1, d)
    x2 = _matmul_residual(ox, wts["w_xo"], x1, tm)
    n_ffn_seq = n_seq if dec_seq is None else 1
    return _ffn(x2, p["g_ffn"][None, :], wts["w_up"], wts["w_gate"], wts["conv_w"], p["conv_b"][None, :],
                wts["w_down"], g_final[None, :], s1, s2, n_ffn_seq, tm, wts["tf"], dec_seq)


def _heads_to_rows(qpad, n, t):
    q = qpad.reshape(n, t, NSA_HEADS, LANES).transpose(0, 2, 1, 3).reshape(n, NSA_HEADS * t, LANES)
    return jnp.pad(q, ((0, 0), (0, LANES - NSA_HEADS * t), (0, 0)))


def _rows_to_tokens(o, n, t):
    o = o[:, :NSA_HEADS * t].reshape(n, NSA_KV_HEADS, HPG, t, NSA_KV_HEADS, HEAD_DIM)
    o = jnp.stack([o[:, g, :, :, g] for g in range(NSA_KV_HEADS)], axis=1)
    return o.transpose(0, 3, 1, 2, 4).reshape(n * t, NSA_HEADS * HEAD_DIM)


def kernel(x_prompt, x_sample, mem_prompt, cache_cmp_k, cache_cmp_v, cache_sel_k, cache_sel_v, cache_win_k,
           cache_win_v, cache_mem_k, cache_mem_v, state_conv, page_table, g_mix, w_in, w_s, b_s, ln_v_g, ln_v_b,
           w_cmp1_k, b_cmp1_k, w_cmp2_k, b_cmp2_k, pe_cmp_k, w_cmp1_v, b_cmp1_v, w_cmp2_v, b_cmp2_v, pe_cmp_v,
           rel_bias, g_a, g_b, w_o, g_xa, g_mem, w_xq, w_mk, w_mv, w_xo, g_ffn, w_up, w_gate, conv_w, conv_b,
           w_down, g_final):
    depth = w_in.shape[0]
    assert depth == 1, "the layer loop is written for a single layer"
    bsz, seq, d = x_prompt.shape
    nd, dec_seq, _ = x_sample.shape
    mem_len = mem_prompt.shape[1]
    d_a = g_a.shape[1]
    d_ff = w_up.shape[2]
    n_pages = page_table.shape[1]
    page = cache_cmp_k.shape[2]
    past = n_pages * page
    kvw = NSA_KV_HEADS * HEAD_DIM
    win_buf = cache_win_k.shape[2]
    assert kvw == LANES and seq % 256 == 0 and past % 512 == 0 and dec_seq == 4 and nd * dec_seq % LANES == 0
    tm = 512
    tq = 256
    tk_s = 512
    tf = d_ff // 2 if (d_ff // 2) % LANES == 0 else d_ff

    l = 0
    p = dict(g_mix=g_mix[l], ln_v_g=ln_v_g[l], ln_v_b=ln_v_b[l], w_cmp1_k=w_cmp1_k[l], b_cmp1_k=b_cmp1_k[l],
             w_cmp2_k=w_cmp2_k[l], b_cmp2_k=b_cmp2_k[l], pe_cmp_k=pe_cmp_k[l], w_cmp1_v=w_cmp1_v[l],
             b_cmp1_v=b_cmp1_v[l], w_cmp2_v=w_cmp2_v[l], b_cmp2_v=b_cmp2_v[l], pe_cmp_v=pe_cmp_v[l], g_a=g_a[l],
             g_b=g_b[l], g_xa=g_xa[l], g_ffn=g_ffn[l], conv_b=conv_b[l])
    w_in_pad, w_vt = _prep_in_proj(w_in[l], d_a)
    cw = _prep_compress_kv(p)
    wts = dict(wo_a=w_o[l, :d_a].astype(BF16), wo_b=w_o[l, d_a:].astype(BF16), w_xq=w_xq[l].astype(BF16),
               w_xo=w_xo[l].astype(BF16), w_up=w_up[l].astype(BF16), w_gate=w_gate[l].astype(BF16),
               w_down=w_down[l].astype(BF16), conv_w=jnp.pad(conv_w[l], ((0, SUBLANES - CONV_W), (0, 0))), tf=tf)
    bias_lanes = jnp.repeat(b_s[l].T, d_a // A_GROUPS, axis=1)

    xp = x_prompt.reshape(bsz * seq, d)
    an, _, _, kv6, kvb, qvt, gates = _shared_front(xp, p, w_in_pad, w_vt, CHUNK, w_s[l], bias_lanes, tm, False)
    kc, vct = _compress_prompt(kv6[0], kv6[1], cw, bsz, seq)
    n_sel = seq // SEL_BLOCK
    n_sel_pad = _round_up(n_sel, LANES)
    near, winfar, band = _bias_tables_prompt(rel_bias, tq)
    oc, sel_t = _cmpsel_prompt(qvt, kc, vct, band, bsz, seq, n_sel_pad, min(SEL_TOP, n_sel))
    osel, owin = _flash_prompt(kvb, qvt, sel_t, near, winfar, bsz, seq, tq)
    mem_kv = _norm_matmul(mem_prompt.reshape(bsz * mem_len, d), g_mem[l][None, :],
                          jnp.concatenate([w_mk[l], w_mv[l]], axis=1).astype(BF16), F32, min(tm, bsz * mem_len))
    mk_p = mem_kv[:, :d].reshape(bsz, mem_len, d)
    mv_p = mem_kv[:, d:].reshape(bsz, mem_len, d)
    zero_state = jnp.zeros((SUBLANES, d_ff), F32)
    yp, a_tail = _shared_back(xp, an, oc, osel, owin, gates, p, wts, mk_p, mv_p, bsz, zero_state, zero_state,
                              g_final, None, tm)
    keep = min(WINDOW, seq)
    shp = lambda a: a.reshape(1, bsz, seq, NSA_KV_HEADS, HEAD_DIM)
    prompt_state = (shp(kv6[0]), shp(kv6[1]), shp(kv6[2]), shp(kv6[3]), shp(kv6[4])[:, :, -keep:],
                    shp(kv6[5])[:, :, -keep:], mk_p.reshape(1, bsz, mem_len, XA_HEADS, d // XA_HEADS),
                    mv_p.reshape(1, bsz, mem_len, XA_HEADS, d // XA_HEADS),
                    a_tail.reshape(bsz, -1, SUBLANES, d_ff)[None, :, -1, SUBLANES - (CONV_W - 1):])

    xs = x_sample.reshape(nd * dec_seq, d)
    reps = CHUNK // dec_seq
    mix_w_s = jnp.tile(w_s[l][:, :dec_seq, :dec_seq], (1, reps, reps))
    bias_s = jnp.repeat(jnp.tile(b_s[l][:, :dec_seq].T, (reps, 1)), d_a // A_GROUPS, axis=1)
    tms = min(tm, nd * dec_seq)
    an_s, v_rows, qpad_s, kv6s, kvb_s, _, gates_s = _shared_front(xs, p, w_in_pad, w_vt, dec_seq, mix_w_s, bias_s,
                                                                 tms, True)
    pools = [c[l].transpose(0, 2, 3, 1).reshape(c.shape[1], kvw, page)
             for c in (cache_cmp_k, cache_cmp_v, cache_sel_k, cache_sel_v)]
    kc_s, vct_s = _compress_sample(page_table, pools[0], pools[1], cw)
    n_sub_s = past // CMP_STRIDE
    n_cmp_s = (past + dec_seq) // CMP_STRIDE - 1
    n_sel_s = -(-(past + dec_seq) // SEL_BLOCK)
    n_sel_pad_s = _round_up(n_sel_s, LANES)
    cmp_t, tail, newb, winb = _bias_tables_sample(rel_bias, past, dec_seq, n_cmp_s, n_sub_s, tk_s, win_buf)
    h_of, t_of, live = _row_heads(dec_seq)
    same = (live[:, None] & live[None, :] & ((h_of[:, None] // HPG) == (h_of[None, :] // HPG))
            & (t_of[:, None] == t_of[None, :]))
    qs = _heads_to_rows(qpad_s, nd, dec_seq)
    oct_s, selrows = _cmpsel_sample(qs, kc_s, vct_s, cmp_t, jnp.asarray(same, dtype=BF16), n_sel_pad_s,
                                    min(SEL_TOP, n_sel_s), past, dec_seq)
    newrows = lambda a: jnp.pad(a.reshape(nd, dec_seq, LANES), ((0, 0), (0, LANES - dec_seq), (0, 0)))
    kvn = (jnp.stack([newrows(kvb_s[:, 2 * LANES:3 * LANES]), newrows(kvb_s[:, 4 * LANES:5 * LANES])], axis=1)
           .reshape(2 * nd, LANES, LANES),
           jnp.stack([newrows(kvb_s[:, 3 * LANES:4 * LANES]), newrows(kvb_s[:, 5 * LANES:6 * LANES])], axis=1)
           .reshape(2 * nd, LANES, LANES))
    wk = cache_win_k[l].transpose(0, 2, 3, 1).reshape(nd, kvw, win_buf)
    wv = cache_win_v[l].transpose(0, 2, 3, 1).reshape(nd, kvw, win_buf)
    osel_r, owin_r = _flash_sample(page_table, qs, pools[2], pools[3], kvn, selrows,
                                   _block_expand(n_sel_pad_s, past), tail, newb, wk, wv, winb, tk_s)
    oc_s = _rows_to_tokens(oct_s.transpose(0, 2, 1), nd, dec_seq)
    osel_s = _rows_to_tokens(osel_r, nd, dec_seq)
    owin_s = _rows_to_tokens(owin_r, nd, dec_seq)
    st = state_conv[l]
    zrow = jnp.zeros_like(st[:, :1])
    s1 = jnp.concatenate([st[:, 1:2]] + [zrow] * (dec_seq - 1), axis=1).reshape(nd * dec_seq, d_ff)
    s2 = jnp.concatenate([st[:, 0:1], st[:, 1:2]] + [zrow] * (dec_seq - 2), axis=1).reshape(nd * dec_seq, d_ff)
    mk_s = cache_mem_k[l].reshape(nd, mem_len, d)
    mv_s = cache_mem_v[l].reshape(nd, mem_len, d)
    ys, a_full = _shared_back(xs, an_s, oc_s, osel_s, owin_s, gates_s, p, wts, mk_s, mv_s, nd, s1, s2, g_final,
                              dec_seq, tms)
    keep_s = min(WINDOW, past + dec_seq)
    shs = lambda a: a.reshape(1, nd, dec_seq, NSA_KV_HEADS, HEAD_DIM)
    win_new = lambda cache, new: jnp.concatenate([cache[l], shs(new)[0]], axis=1)[None, :, -keep_s:]
    sample_state = (shs(kv6s[0]), shs(kv6s[1]), shs(kv6s[2]), shs(kv6s[3]), win_new(cache_win_k, kv6s[4]),
                    win_new(cache_win_v, kv6s[5]), v_rows.reshape(1, nd, dec_seq, d_a),
                    a_full.reshape(1, nd, dec_seq, d_ff)[:, :, dec_seq - (CONV_W - 1):])

    return (yp.reshape(bsz, seq, d), ys.reshape(nd, dec_seq, d)) + prompt_state + sample_state
```
